```python
import math
import jax, jax.numpy as jnp
from jax import lax
import numpy as np

D_MODEL = 1024
BATCH = 32
SEQ = 2048
DEPTH = 1
DEC_BATCH = 128
DEC_SEQ = 4
PAST_LEN = 8192
PAGE_SIZE = 128

D_FF = 2816
GDN_HEADS = 8
GDN_HEAD_DIM = 64
GDN_WIDTH = GDN_HEADS * GDN_HEAD_DIM
GDN_CONV = 4
GDN_CHUNK = 64
NSA_Q_HEADS = 8
NSA_KV_HEADS = 2
NSA_HEAD_DIM = 64
NSA_GROUP = NSA_Q_HEADS // NSA_KV_HEADS
NSA_WIDTH = NSA_Q_HEADS * NSA_HEAD_DIM
KV_WIDTH = NSA_KV_HEADS * NSA_HEAD_DIM
CMP_BLOCK = 32
SEL_BLOCK = 64
CMP_PER_SEL = SEL_BLOCK // CMP_BLOCK
SEL_TOPK = 16
WINDOW = 512
SEL_QBLK = 16
WIN_QBLK = 128
IN_SPLITS = (3 * GDN_WIDTH, GDN_WIDTH, GDN_HEADS, GDN_HEADS, NSA_WIDTH,
             2 * KV_WIDTH, 2 * KV_WIDTH, 2 * KV_WIDTH, 3 * NSA_Q_HEADS, 2 * D_MODEL)
D_IN = sum(IN_SPLITS)
DN_ALPHA = (2.0 * DEPTH) ** 0.25
DN_BETA = (8.0 * DEPTH) ** -0.25
NEG = -1e30
BIG = 1e4

kernel_name = 'hybrid_gdn_nsa_macaron_step'


def _split_in(p):
    offs = [0]
    for w in IN_SPLITS:
        offs.append(offs[-1] + w)
    return [p[..., offs[i]:offs[i + 1]] for i in range(len(IN_SPLITS))]


def _layer_norm(x, g, b, eps=1e-5):
    xf = x.astype(jnp.float32)
    mu = xf.mean(-1, keepdims=True)
    var = jnp.mean(jnp.square(xf - mu), -1, keepdims=True)
    return ((xf - mu) * lax.rsqrt(var + eps)).astype(x.dtype) * g + b


def _l2norm(x, eps=1e-6):
    return x * lax.rsqrt(jnp.sum(x * x, -1, keepdims=True) + eps)


def _masked_softmax(s, mask, axis=-1):
    s = jnp.where(mask, s.astype(jnp.float32), NEG)
    p = jax.nn.softmax(s, axis=axis)
    return jnp.where(mask, p, 0.0)


def _swiglu(h, w_gu, w_down):
    gt, up = jnp.split(h @ w_gu, 2, axis=-1)
    return (jax.nn.silu(gt) * up) @ w_down


def _causal_conv(x, buf, w):
    L = x.shape[1]
    xc = jnp.concatenate([buf, x], axis=1)
    y = sum(xc[:, j:j + L] * w[j] for j in range(GDN_CONV))
    return jax.nn.silu(y), xc[:, -(GDN_CONV - 1):]


def _gated_delta(q, k, v, g, beta, s0):
    B, L, H, _ = q.shape
    C = min(GDN_CHUNK, L)
    Lp = -(-L // C) * C
    pad = Lp - L

    def prep(t):
        t = jnp.pad(t, [(0, 0), (0, pad)] + [(0, 0)] * (t.ndim - 2))
        t = t.reshape((B, Lp // C, C) + t.shape[2:])
        return jnp.moveaxis(t, 3, 2)

    qc, kc, vc, gc, bc = [prep(t) for t in (q, k, v, g, beta)]
    G = jnp.cumsum(gc, axis=-1)
    ix = jnp.arange(C)
    incl = ix[:, None] >= ix[None, :]
    strict = ix[:, None] > ix[None, :]
    decay = jnp.exp(jnp.where(incl, G[..., :, None] - G[..., None, :], NEG))
    kb = kc * bc[..., None]
    a_kk = jnp.einsum('bnhid,bnhjd->bnhij', kb, kc) * jnp.where(strict, decay, 0.0)
    eye = jnp.eye(C, dtype=q.dtype)
    t_inv = lax.linalg.triangular_solve(eye + a_kk, jnp.broadcast_to(eye, a_kk.shape),
                                        left_side=True, lower=True)
    u = t_inv @ (vc * bc[..., None])
    w = t_inv @ (kb * jnp.exp(G)[..., None])
    a_qk = jnp.einsum('bnhid,bnhjd->bnhij', qc, kc) * decay
    q_dec = qc * jnp.exp(G)[..., None]
    k_dec = kc * jnp.exp(G[..., -1:] - G)[..., None]
    g_last = jnp.exp(G[..., -1])

    def step(S, xs):
        u_c, w_c, aqk_c, qd_c, kd_c, gl_c = xs
        v_new = u_c - w_c @ S
        o = qd_c @ S + aqk_c @ v_new
        S = S * gl_c[..., None, None] + jnp.einsum('bhcd,bhce->bhde', kd_c, v_new)
        return S, o

    xs = tuple(jnp.moveaxis(t, 1, 0) for t in (u, w, a_qk, q_dec, k_dec, g_last))
    s_fin, o = lax.scan(step, s0, xs)
    o = jnp.moveaxis(jnp.moveaxis(o, 0, 1), 2, 3).reshape(B, Lp, H, -1)[:, :L]
    return o, s_fin


def _gdn_mixer(qkv, z, a, b, conv_buf, s0, conv_w, a_log, dt_bias, norm_w):
    B, L, _ = qkv.shape
    qkv_c, new_buf = _causal_conv(qkv, conv_buf, conv_w)
    qkv_c = qkv_c.astype(jnp.float32)
    q, k, v = [t.reshape(B, L, GDN_HEADS, GDN_HEAD_DIM) for t in jnp.split(qkv_c, 3, axis=-1)]
    q = _l2norm(q) * GDN_HEAD_DIM ** -0.5
    k = _l2norm(k)
    g = -jnp.exp(a_log.astype(jnp.float32)) * jax.nn.softplus(a.astype(jnp.float32) + dt_bias.astype(jnp.float32))
    beta = jax.nn.sigmoid(b.astype(jnp.float32))
    o, s_fin = _gated_delta(q, k, v, g, beta, s0.astype(jnp.float32))
    o = o * lax.rsqrt(jnp.mean(o * o, -1, keepdims=True) + 1e-6)
    o = o * norm_w.astype(jnp.float32) * jax.nn.silu(z.reshape(B, L, GDN_HEADS, GDN_HEAD_DIM).astype(jnp.float32))
    return o.reshape(B, L, GDN_WIDTH).astype(qkv.dtype), new_buf, s_fin.astype(qkv.dtype)


def _nsa_compressed(qg, kv_full, w_cmp, q_pos):
    B, T = kv_full.shape[:2]
    n_cmp = T // CMP_BLOCK
    blocks = kv_full.reshape(B, n_cmp, CMP_BLOCK, 2, NSA_KV_HEADS, NSA_HEAD_DIM)
    kvb = jnp.einsum('bnjshd,sjh->bnshd', blocks, w_cmp)
    s = jnp.einsum('blhgd,bnhd->bhgln', qg, kvb[:, :, 0]) * NSA_HEAD_DIM ** -0.5
    mask = ((jnp.arange(n_cmp) + 1) * CMP_BLOCK - 1)[None, :] <= q_pos[:, None]
    p = _masked_softmax(s, mask)
    o = jnp.einsum('bhgln,bnhd->blhgd', p.astype(qg.dtype), kvb[:, :, 1])
    imp = p.sum(2).reshape(B, NSA_KV_HEADS, -1, n_cmp // CMP_PER_SEL, CMP_PER_SEL).sum(-1)
    return o, imp


def _nsa_select(imp, q_pos):
    n_sel = imp.shape[-1]
    blk = jnp.arange(n_sel)[None, :]
    cur = (q_pos // SEL_BLOCK)[:, None]
    valid = blk * SEL_BLOCK <= q_pos[:, None]
    forced = (blk == 0) | (blk == cur) | (blk == cur - 1)
    score = jnp.where(valid, jnp.where(forced, BIG, imp), -1.0)
    top_v, top_i = lax.top_k(jnp.moveaxis(score, 1, 2), min(SEL_TOPK, n_sel))
    return top_i, top_v >= 0.0


def _nsa_sel_attend(qg, kv_g, top_i, sel_ok, q_pos):
    key_pos = top_i[..., None] * SEL_BLOCK + jnp.arange(SEL_BLOCK)
    mask = sel_ok[..., None] & (key_pos <= q_pos[None, :, None, None, None])
    s = jnp.einsum('blhgd,blhkjd->blhgkj', qg, kv_g[..., 0, :]) * NSA_HEAD_DIM ** -0.5
    p = _masked_softmax(s, mask[:, :, :, None], axis=(-2, -1))
    return jnp.einsum('blhgkj,blhkjd->blhgd', p.astype(qg.dtype), kv_g[..., 1, :])


def _nsa_win_attend(qg, kv, q_pos, k_pos):
    s = jnp.einsum('blhgd,bshd->bhgls', qg, kv[:, :, 0]) * NSA_HEAD_DIM ** -0.5
    diff = q_pos[:, None] - k_pos[None, :]
    mask = (diff >= 0) & (diff < WINDOW) & (k_pos[None, :] >= 0)
    p = _masked_softmax(s, mask)
    return jnp.einsum('bhgls,bshd->blhgd', p.astype(qg.dtype), kv[:, :, 1])


def _nsa_prompt(qg, kv_cmp, kv_sel, kv_win, w_cmp):
    B, S = qg.shape[:2]
    pos = jnp.arange(S)
    o_cmp, imp = _nsa_compressed(qg, kv_cmp, w_cmp, pos)
    top_i, sel_ok = _nsa_select(imp, pos)
    sel_blocks = jnp.moveaxis(kv_sel.reshape(B, S // SEL_BLOCK, SEL_BLOCK, 2, NSA_KV_HEADS, NSA_HEAD_DIM), 4, 1)
    b_ix = jnp.arange(B)[:, None, None, None]
    h_ix = jnp.arange(NSA_KV_HEADS)[None, None, :, None]
    nqb = S // SEL_QBLK

    def to_blocks(t):
        return jnp.moveaxis(t.reshape((B, nqb, SEL_QBLK) + t.shape[2:]), 1, 0)

    def sel_block(xs):
        q_b, i_b, ok_b, pos_b = xs
        kv_g = sel_blocks[b_ix, h_ix, i_b]
        return _nsa_sel_attend(q_b, kv_g, i_b, ok_b, pos_b)

    o_sel = lax.map(sel_block, (to_blocks(qg), to_blocks(top_i), to_blocks(sel_ok), pos.reshape(nqb, SEL_QBLK)))
    o_sel = jnp.moveaxis(o_sel, 0, 1).reshape(qg.shape)
    kv_pad = jnp.pad(kv_win, ((0, 0), (WINDOW, 0), (0, 0), (0, 0), (0, 0)))

    def win_block(i):
        start = i * WIN_QBLK
        q_b = lax.dynamic_slice_in_dim(qg, start, WIN_QBLK, axis=1)
        kv_b = lax.dynamic_slice_in_dim(kv_pad, start, WIN_QBLK + WINDOW, axis=1)
        q_pos = start + jnp.arange(WIN_QBLK)
        k_pos = start - WINDOW + jnp.arange(WIN_QBLK + WINDOW)
        return _nsa_win_attend(q_b, kv_b, q_pos, k_pos)

    o_win = lax.map(win_block, jnp.arange(S // WIN_QBLK))
    o_win = jnp.moveaxis(o_win, 0, 1).reshape(qg.shape)
    return o_cmp, o_sel, o_win, kv_win[:, S - min(WINDOW, S):]


def _nsa_sample(qg, kv_cmp, kv_sel, kv_win, w_cmp, cache_cmp, cache_sel, cache_win, page_table):
    B, L = qg.shape[:2]
    n_pages = page_table.shape[1]
    past = n_pages * PAGE_SIZE
    pos = past + jnp.arange(L)
    t_pad = -(-(past + L) // SEL_BLOCK) * SEL_BLOCK

    def pad_rows(t):
        return jnp.pad(t, ((0, 0), (0, t_pad - past - L), (0, 0), (0, 0), (0, 0)))

    cmp_past = cache_cmp[page_table].reshape((B, past) + cache_cmp.shape[2:])
    o_cmp, imp = _nsa_compressed(qg, pad_rows(jnp.concatenate([cmp_past, kv_cmp], axis=1)), w_cmp, pos)
    top_i, sel_ok = _nsa_select(imp, pos)
    sub = PAGE_SIZE // SEL_BLOCK
    pool = cache_sel.reshape((cache_sel.shape[0], sub, SEL_BLOCK) + cache_sel.shape[2:])
    n_past_blk = past // SEL_BLOCK
    b_ix = jnp.arange(B)[:, None, None, None]
    h_ix = jnp.arange(NSA_KV_HEADS)[None, None, :, None]
    phys = page_table[b_ix, jnp.minimum(top_i // sub, n_pages - 1)]
    kv_past = pool[phys, top_i % sub, :, :, h_ix]
    tail = pad_rows(kv_sel).reshape(B, -1, SEL_BLOCK, 2, NSA_KV_HEADS, NSA_HEAD_DIM)
    kv_tail = tail[b_ix, jnp.clip(top_i - n_past_blk, 0, tail.shape[1] - 1), :, :, h_ix]
    kv_g = jnp.where((top_i >= n_past_blk)[..., None, None, None], kv_tail, kv_past)
    o_sel = _nsa_sel_attend(qg, kv_g, top_i, sel_ok, pos)
    wb = cache_win.shape[1]
    kv_all = jnp.concatenate([cache_win, kv_win], axis=1)
    o_win = _nsa_win_attend(qg, kv_all, pos, past - wb + jnp.arange(wb + L))
    return o_cmp, o_sel, o_win, kv_all[:, L:]


def _mix(h, mw, past):
    w_in, conv_w, a_log, dt_bias, norm_w, w_cmp, w_br_gdn, w_br_nsa, w_out = mw
    B, L, _ = h.shape
    qkv_g, z_g, a_g, b_g, q_n, kv_c, kv_s, kv_w, gate_n, merge_logits = _split_in(h @ w_in)

    def rows(t):
        return t.reshape(B, L, 2, NSA_KV_HEADS, NSA_HEAD_DIM)

    qg = q_n.reshape(B, L, NSA_KV_HEADS, NSA_GROUP, NSA_HEAD_DIM)
    kv_c, kv_s, kv_w = rows(kv_c), rows(kv_s), rows(kv_w)
    if past is None:
        conv_buf = jnp.zeros((B, GDN_CONV - 1, 3 * GDN_WIDTH), h.dtype)
        s0 = jnp.zeros((B, GDN_HEADS, GDN_HEAD_DIM, GDN_HEAD_DIM), jnp.float32)
        o_c, o_s, o_w, win_buf = _nsa_prompt(qg, kv_c, kv_s, kv_w, w_cmp)
    else:
        s0, conv_buf, cache_cmp, cache_sel, cache_win, page_table = past
        o_c, o_s, o_w, win_buf = _nsa_sample(qg, kv_c, kv_s, kv_w, w_cmp, cache_cmp, cache_sel, cache_win, page_table)
    o_gdn, conv_new, s_new = _gdn_mixer(qkv_g, z_g, a_g, b_g, conv_buf, s0, conv_w, a_log, dt_bias, norm_w)
    gn = jax.nn.sigmoid(gate_n).reshape(B, L, 3, NSA_KV_HEADS, NSA_GROUP, 1)
    o_nsa = (gn[:, :, 0] * o_c + gn[:, :, 1] * o_s + gn[:, :, 2] * o_w).reshape(B, L, NSA_WIDTH)
    m_gdn, m_nsa = jnp.split(jax.nn.sigmoid(merge_logits), 2, axis=-1)
    y = (m_gdn * (o_gdn @ w_br_gdn) + m_nsa * (o_nsa @ w_br_nsa)) @ w_out
    return y, (s_new, conv_new, kv_c, kv_s, win_buf)


def _layer(x, c, mw, past, ln_g, ln_b, w_ada, b_ada, w_ff1_gu, w_ff1_dn, w_ff2_gu, w_ff2_dn):
    mod = (jax.nn.silu(c) @ w_ada + b_ada).reshape(c.shape[0], 3, 3, 1, D_MODEL)

    def modulate(t, i):
        return t * (1.0 + mod[:, i, 1]) + mod[:, i, 0]

    x = _layer_norm(DN_ALPHA * x + 0.5 * mod[:, 0, 2] * _swiglu(modulate(x, 0), w_ff1_gu, w_ff1_dn), ln_g[0], ln_b[0])
    y, st = _mix(modulate(x, 1), mw, past)
    x = _layer_norm(DN_ALPHA * x + mod[:, 1, 2] * y, ln_g[1], ln_b[1])
    x = _layer_norm(DN_ALPHA * x + 0.5 * mod[:, 2, 2] * _swiglu(modulate(x, 2), w_ff2_gu, w_ff2_dn), ln_g[2], ln_b[2])
    return x, st


def setup_inputs(seed: int = 0) -> dict:
    key = jax.random.key(seed)
    ks = iter(jax.random.split(key, 40))

    def nrm(shape, s=1.0):
        return jax.random.normal(next(ks), shape, jnp.float32) * s

    n_pages = PAST_LEN // PAGE_SIZE
    used = DEC_BATCH * n_pages
    n_phys = used + max(1, used // 4)
    win_len = min(WINDOW, PAST_LEN)
    page_table = jax.random.permutation(next(ks), n_phys)[:used].astype(jnp.int32).reshape(DEC_BATCH, n_pages)
    Ld = DEPTH
    dt = jnp.exp(jax.random.uniform(next(ks), (Ld, GDN_HEADS), jnp.float32, math.log(1e-3), math.log(1e-1)))
    return {
        'x_prompt': nrm((BATCH, SEQ, D_MODEL)),
        'x_sample': nrm((DEC_BATCH, DEC_SEQ, D_MODEL)),
        'c_prompt': nrm((BATCH, D_MODEL)),
        'c_sample': nrm((DEC_BATCH, D_MODEL)),
        'state_gdn': nrm((Ld, DEC_BATCH, GDN_HEADS, GDN_HEAD_DIM, GDN_HEAD_DIM), 0.1),
        'state_gdn_conv': nrm((Ld, DEC_BATCH, GDN_CONV - 1, 3 * GDN_WIDTH)),
        'cache_cmp_kv': nrm((Ld, n_phys, PAGE_SIZE, 2, NSA_KV_HEADS, NSA_HEAD_DIM)),
        'cache_sel_kv': nrm((Ld, n_phys, PAGE_SIZE, 2, NSA_KV_HEADS, NSA_HEAD_DIM)),
        'cache_win_kv': nrm((Ld, DEC_BATCH, win_len, 2, NSA_KV_HEADS, NSA_HEAD_DIM)),
        'page_table': page_table,
        'ln_g': 1.0 + nrm((Ld, 3, D_MODEL), 0.02),
        'ln_b': nrm((Ld, 3, D_MODEL), 0.02),
        'w_ada': nrm((Ld, D_MODEL, 9 * D_MODEL), 0.5 * D_MODEL ** -0.5),
        'b_ada': nrm((Ld, 9 * D_MODEL), 0.02),
        'w_ff1_gu': nrm((Ld, D_MODEL, 2 * D_FF), D_MODEL ** -0.5),
        'w_ff1_dn': nrm((Ld, D_FF, D_MODEL), DN_BETA * D_FF ** -0.5),
        'w_ff2_gu': nrm((Ld, D_MODEL, 2 * D_FF), D_MODEL ** -0.5),
        'w_ff2_dn': nrm((Ld, D_FF, D_MODEL), DN_BETA * D_FF ** -0.5),
        'w_in': nrm((Ld, D_MODEL, D_IN), D_MODEL ** -0.5),
        'gdn_conv_w': nrm((Ld, GDN_CONV, 3 * GDN_WIDTH), 0.5),
        'gdn_a_log': jnp.log(jax.random.uniform(next(ks), (Ld, GDN_HEADS), jnp.float32, 1.0, 16.0)),
        'gdn_dt_bias': dt + jnp.log(-jnp.expm1(-dt)),
        'gdn_norm_w': 1.0 + nrm((Ld, GDN_HEAD_DIM), 0.02),
        'nsa_w_cmp': (1.0 + nrm((Ld, 2, CMP_BLOCK, NSA_KV_HEADS), 0.1)) / CMP_BLOCK,
        'w_br_gdn': nrm((Ld, GDN_WIDTH, D_MODEL), GDN_WIDTH ** -0.5),
        'w_br_nsa': nrm((Ld, NSA_WIDTH, D_MODEL), NSA_WIDTH ** -0.5),
        'w_out': nrm((Ld, D_MODEL, D_MODEL), DN_BETA * D_MODEL ** -0.5),
    }


def reference(x_prompt, x_sample, c_prompt, c_sample, state_gdn, state_gdn_conv, cache_cmp_kv, cache_sel_kv,
              cache_win_kv, page_table, ln_g, ln_b, w_ada, b_ada, w_ff1_gu, w_ff1_dn, w_ff2_gu, w_ff2_dn,
              w_in, gdn_conv_w, gdn_a_log, gdn_dt_bias, gdn_norm_w, nsa_w_cmp, w_br_gdn, w_br_nsa, w_out):
    y_p, y_s = x_prompt, x_sample
    p_st, s_st = [], []
    for l in range(DEPTH):
        mw = (w_in[l], gdn_conv_w[l], gdn_a_log[l], gdn_dt_bias[l], gdn_norm_w[l], nsa_w_cmp[l],
              w_br_gdn[l], w_br_nsa[l], w_out[l])
        fw = (ln_g[l], ln_b[l], w_ada[l], b_ada[l], w_ff1_gu[l], w_ff1_dn[l], w_ff2_gu[l], w_ff2_dn[l])
        y_p, st_p = _layer(y_p, c_prompt, mw, None, *fw)
        past = (state_gdn[l], state_gdn_conv[l], cache_cmp_kv[l], cache_sel_kv[l], cache_win_kv[l], page_table)
        y_s, st_s = _layer(y_s, c_sample, mw, past, *fw)
        p_st.append(st_p)
        s_st.append(st_s)
    p_gdn, p_conv, p_cmp_kv, p_sel_kv, p_win_kv = [jnp.stack(t) for t in zip(*p_st)]
    s_gdn, s_conv, s_cmp_kv, s_sel_kv, s_win_kv = [jnp.stack(t) for t in zip(*s_st)]
    return (y_p, y_s, p_gdn, p_conv, p_cmp_kv, p_sel_kv, p_win_kv, s_gdn, s_conv, s_cmp_kv, s_sel_kv, s_win_kv)
```

```python
import functools

import jax
import jax.numpy as jnp
from jax import lax
from jax.experimental import pallas as pl
from jax.experimental.pallas import tpu as pltpu

F32 = jnp.float32
BF16 = jnp.bfloat16
HIGHEST = lax.Precision.HIGHEST

GDN_HEADS = 8
GDN_HEAD_DIM = 64
GDN_WIDTH = GDN_HEADS * GDN_HEAD_DIM
GDN_CONV = 4
GDN_CHUNK = 64
NSA_Q_HEADS = 8
NSA_KV_HEADS = 2
NSA_HEAD_DIM = 64
NSA_GROUP = NSA_Q_HEADS // NSA_KV_HEADS
NSA_WIDTH = NSA_Q_HEADS * NSA_HEAD_DIM
KV_WIDTH = NSA_KV_HEADS * NSA_HEAD_DIM
CMP_BLOCK = 32
SEL_BLOCK = 64
SEL_TOPK = 16
WINDOW = 512
PAGE_SIZE = 128
NEG = -1e30
BIG = 1e4
LN_EPS = 1e-5

SUBLANES = 8
LANES = 128
VMEM_LIMIT = 52 * 1024 * 1024

NSA_TQ = 128
NSA_TK = 256
PAGES_PER_STEP = 32


def _sig(x):
    return 1.0 / (1.0 + jnp.exp(-x))


def _softplus(x):
    return jnp.maximum(x, 0.0) + jnp.log(1.0 + jnp.exp(-jnp.abs(x)))


def _layer_norm(r, g, b):
    mu = jnp.mean(r, -1, keepdims=True)
    d = r - mu
    var = jnp.mean(d * d, -1, keepdims=True)
    return d * lax.rsqrt(var + LN_EPS) * g + b


def _dot(a, b):
    return jnp.dot(a, b, preferred_element_type=F32)


def _dot_nt(a, b):
    return lax.dot_general(a, b, (((1,), (1,)), ((), ())), preferred_element_type=F32)


def _dot_tn(a, b):
    return lax.dot_general(a, b, (((0,), (0,)), ((), ())), preferred_element_type=F32)


def _params(*sem):
    return pltpu.CompilerParams(dimension_semantics=sem, vmem_limit_bytes=VMEM_LIMIT)


def _mod_spec(kind, tm, d, tiles_per_seq, k):
    if kind == "seq":
        return pl.BlockSpec((None, 1, d), lambda i, *_: ((i // tiles_per_seq) * 9 + k, 0, 0))
    return pl.BlockSpec((tm, d), lambda i, *_: (i, k))


def _ada_kernel(c_ref, w_ref, b_ref, o_ref):
    c = c_ref[...]
    h = (c * _sig(c)).astype(BF16)
    o_ref[...] = _dot(h, w_ref[...].astype(BF16)) + b_ref[...]


def _ada(c, w, b):
    r, d = c.shape
    n = w.shape[1]
    tn = d
    return pl.pallas_call(
        _ada_kernel,
        grid=(n // tn,),
        in_specs=[pl.BlockSpec((r, d), lambda j: (0, 0)),
                  pl.BlockSpec((d, tn), lambda j: (0, j)),
                  pl.BlockSpec((1, tn), lambda j: (0, j))],
        out_specs=pl.BlockSpec((r, tn), lambda j: (0, j)),
        out_shape=jax.ShapeDtypeStruct((r, n), F32),
        compiler_params=_params("arbitrary"),
        name="ada",
    )(c, w, b.reshape(1, n))


def _ffn_kernel(x_ref, sh_ref, sc_ref, gt_ref, wg_ref, wu_ref, wd_ref, lg_ref, lb_ref, o_ref,
                h_scr, acc_scr, *, alpha):
    j = pl.program_id(1)

    @pl.when(j == 0)
    def _():
        h_scr[...] = (x_ref[...] * (1.0 + sc_ref[...]) + sh_ref[...]).astype(BF16)
        acc_scr[...] = jnp.zeros_like(acc_scr)

    h = h_scr[...]
    g = _dot(h, wg_ref[...])
    u = _dot(h, wu_ref[...])
    a = (g * _sig(g) * u).astype(BF16)
    acc_scr[...] += _dot(a, wd_ref[...])

    @pl.when(j == pl.num_programs(1) - 1)
    def _():
        r = alpha * x_ref[...] + (0.5 * gt_ref[...]) * acc_scr[...]
        o_ref[...] = _layer_norm(r, lg_ref[...], lb_ref[...])


def _ffn(x, mods, kind, rows_per_seq, k0, wg, wu, wd, lg, lb, alpha, tm, tf):
    n, d = x.shape
    dff = wg.shape[1]
    tps = max(rows_per_seq // tm, 1)
    ms = lambda k: _mod_spec(kind, tm, d, tps, k)
    return pl.pallas_call(
        functools.partial(_ffn_kernel, alpha=alpha),
        grid=(n // tm, dff // tf),
        in_specs=[pl.BlockSpec((tm, d), lambda i, j: (i, 0)),
                  ms(k0), ms(k0 + 1), ms(k0 + 2),
                  pl.BlockSpec((d, tf), lambda i, j: (0, j)),
                  pl.BlockSpec((d, tf), lambda i, j: (0, j)),
                  pl.BlockSpec((tf, d), lambda i, j: (j, 0)),
                  pl.BlockSpec((1, d), lambda i, j: (0, 0)),
                  pl.BlockSpec((1, d), lambda i, j: (0, 0))],
        out_specs=pl.BlockSpec((tm, d), lambda i, j: (i, 0)),
        out_shape=jax.ShapeDtypeStruct((n, d), F32),
        scratch_shapes=[pltpu.VMEM((tm, d), BF16), pltpu.VMEM((tm, d), F32)],
        compiler_params=_params("parallel", "arbitrary"),
        name="ffn",
    )(x, mods, mods, mods, wg, wu, wd, lg, lb)


def _inproj_kernel(x_ref, sh_ref, sc_ref, wqkv, wz, wq, wkc, wks, wkw, wm, wsm,
                   oqkv, oz, oq, okc, oks, okw, om, osm):
    h = (x_ref[...] * (1.0 + sc_ref[...]) + sh_ref[...]).astype(BF16)
    oqkv[...] = _dot(h, wqkv[...])
    oz[...] = _dot(h, wz[...]).astype(BF16)
    oq[...] = (_dot(h, wq[...]) * (NSA_HEAD_DIM ** -0.5)).astype(BF16)
    okc[...] = _dot(h, wkc[...])
    oks[...] = _dot(h, wks[...])
    okw[...] = _dot(h, wkw[...])
    om[...] = _sig(_dot(h, wm[...])).astype(BF16)
    osm[...] = _dot(h, wsm[...])


def _inproj(x, mods, kind, rows_per_seq, ws, tm):
    n, d = x.shape
    tps = max(rows_per_seq // tm, 1)
    widths = [w.shape[1] for w in ws]
    dtypes = [F32, BF16, BF16, F32, F32, F32, BF16, F32]
    return pl.pallas_call(
        _inproj_kernel,
        grid=(n // tm,),
        in_specs=[pl.BlockSpec((tm, d), lambda i: (i, 0)),
                  _mod_spec(kind, tm, d, tps, 3), _mod_spec(kind, tm, d, tps, 4)]
                 + [pl.BlockSpec((d, wd), lambda i: (0, 0)) for wd in widths],
        out_specs=[pl.BlockSpec((tm, wd), lambda i: (i, 0)) for wd in widths],
        out_shape=[jax.ShapeDtypeStruct((n, wd), dt) for wd, dt in zip(widths, dtypes)],
        compiler_params=_params("parallel"),
        name="inproj",
    )(x, mods, mods, *ws)


def _gdn_kernel(qkv_ref, z_ref, sm_ref, abt_ref, cb_ref, cw_ref, prow_ref, alog_ref, dtb_ref, nw_ref,
                s0_ref, tri_ref, triu_ref, bd_ref, expg_ref, expb_ref, o_ref, s_ref, xbuf,
                *, chunk, l_valid, l_pad):
    C = chunk
    HD = GDN_HEAD_DIM
    W = GDN_WIDTH
    c = pl.program_id(1)

    @pl.when(c == 0)
    def _():
        xbuf[0:SUBLANES, :] = cb_ref[...]
        s_ref[...] = s0_ref[...]

    x = qkv_ref[...]
    xbuf[SUBLANES:SUBLANES + C, :] = x
    cw = cw_ref[...]
    y = (xbuf[5:5 + C, :] * cw[0:1] + xbuf[6:6 + C, :] * cw[1:2]
         + xbuf[7:7 + C, :] * cw[2:3] + x * cw[3:4])
    xbuf[0:SUBLANES, :] = x[C - SUBLANES:C, :]
    act = y * _sig(y)
    q = act[:, 0:W]
    k = act[:, W:2 * W]
    v = act[:, 2 * W:3 * W]

    ss = jnp.dot(jnp.concatenate([q * q, k * k], 0), bd_ref[...], precision=HIGHEST,
                 preferred_element_type=F32)
    qn = q * lax.rsqrt(ss[:C] + 1e-6) * (HD ** -0.5)
    kn = k * lax.rsqrt(ss[C:] + 1e-6)

    sm = sm_ref[...]
    pr = prow_ref[...]
    gcol = -jnp.exp(pr[0:1]) * _softplus(sm + pr[1:2])
    bcol = _sig(sm)
    ab = abt_ref[...]
    grow = -jnp.exp(alog_ref[...]) * _softplus(ab[0:GDN_HEADS] + dtb_ref[...])
    if l_pad != l_valid:
        vcol = (c * C + lax.broadcasted_iota(jnp.int32, (C, 1), 0) < l_valid).astype(F32)
        vrow = (c * C + lax.broadcasted_iota(jnp.int32, (1, C), 1) < l_valid).astype(F32)
        qn, kn, v = qn * vcol, kn * vcol, v * vcol
        gcol, bcol, grow = gcol * vcol, bcol * vcol, grow * vrow

    g_cum = jnp.dot(tri_ref[...], gcol, precision=HIGHEST, preferred_element_type=F32)
    gx = jnp.dot(g_cum, expg_ref[...], precision=HIGHEST, preferred_element_type=F32)
    bx = jnp.dot(bcol, expb_ref[...], precision=HIGHEST, preferred_element_type=F32)
    g_row = jnp.dot(grow, triu_ref[...], precision=HIGHEST, preferred_element_type=F32)

    eg = jnp.exp(gx)
    g_last = gx[C - 1:C, :]
    q_dec = qn * eg
    k_dec = kn * jnp.exp(g_last - gx)
    kb = kn * bx
    vb = v * bx
    kbg = kb * eg
    e_last = jnp.exp(g_last)

    ri = lax.broadcasted_iota(jnp.int32, (C, C), 0)
    ci = lax.broadcasted_iota(jnp.int32, (C, C), 1)
    incl = ri >= ci
    strict = ri > ci
    eye = (ri == ci).astype(F32)
    sh = min(C, SUBLANES).bit_length() - 1
    diag_blk = (ri >> sh) == (ci >> sh)
    merge_masks = []
    while (1 << sh) < C:
        merge_masks.append(((ri >> (sh + 1)) == (ci >> (sh + 1))) & ((ri >> sh) == (ci >> sh) + 1))
        sh += 1

    heads = range(GDN_HEADS)
    sl = lambda h: slice(h * HD, (h + 1) * HD)
    bfl = lambda xs: [x.astype(BF16) for x in xs]

    dec = [jnp.where(incl, jnp.exp(jnp.where(incl, gx[:, h * HD:h * HD + C] - g_row[h:h + 1, :], 0.0)), 0.0)
           for h in heads]
    r = [_dot_nt(jnp.concatenate([kb[:, sl(h)], qn[:, sl(h)]], 0).astype(BF16), kn[:, sl(h)].astype(BF16))
         for h in heads]
    a_kk = [jnp.where(strict, r[h][:C] * dec[h], 0.0) for h in heads]
    a_qk = bfl([r[h][C:] * dec[h] for h in heads])

    nd = [jnp.where(diag_blk, -a_kk[h], 0.0) for h in heads]
    ndb = bfl(nd)
    p2 = [_dot(ndb[h], ndb[h]) for h in heads]
    t = [eye + nd[h] for h in heads]
    r2 = [_dot(p2[h].astype(BF16), jnp.concatenate([t[h], p2[h]], 1).astype(BF16)) for h in heads]
    t = [t[h] + r2[h][:, :C] for h in heads]
    t = [t[h] + _dot(r2[h][:, C:].astype(BF16), t[h].astype(BF16)) for h in heads]
    for mm in merge_masks:
        tb = bfl(t)
        tl = bfl([_dot(tb[h], jnp.where(mm, a_kk[h], 0.0).astype(BF16)) for h in heads])
        t = [t[h] - _dot(tl[h], tb[h]) for h in heads]
    tb = bfl(t)

    rhs = [jnp.concatenate([vb[:, sl(h)], kbg[:, sl(h)]], 1) for h in heads]
    x0 = [_dot(tb[h], rhs[h].astype(BF16)) for h in heads]
    res = []
    for h in heads:
        ah = a_kk[h].astype(BF16)
        al = (a_kk[h] - ah.astype(F32)).astype(BF16)
        xh = x0[h].astype(BF16)
        xl = (x0[h] - xh.astype(F32)).astype(BF16)
        hh = _dot(ah, jnp.concatenate([xh, xl], 1))
        res.append(rhs[h] - x0[h] - (hh[:, :2 * HD] + hh[:, 2 * HD:] + _dot(al, xh)))
    uw = [x0[h] + _dot(tb[h], res[h].astype(BF16)) for h in heads]

    s_old = [s_ref[h] for h in heads]
    wq = [_dot(jnp.concatenate([uw[h][:, HD:], q_dec[:, sl(h)]], 0).astype(BF16), s_old[h].astype(BF16))
          for h in heads]
    v_new = bfl([uw[h][:, :HD] - wq[h][:C] for h in heads])
    o = [wq[h][C:] + _dot(a_qk[h], v_new[h]) for h in heads]
    for h in heads:
        s_ref[h] = s_old[h] * e_last[:, sl(h)] + _dot_tn(k_dec[:, sl(h)].astype(BF16), v_new[h])

    z = z_ref[...].astype(F32)
    nw = nw_ref[...]
    outs = []
    for h in heads:
        o_h = o[h] * lax.rsqrt(jnp.mean(o[h] * o[h], -1, keepdims=True) + 1e-6)
        zh = z[:, sl(h)]
        outs.append(o_h * nw * (zh * _sig(zh)))
    o_ref[...] = jnp.concatenate(outs, 1).astype(BF16)


def _gdn(qkv, z, small, conv_buf, s0, conv_w, a_log, dt_bias, norm_w, l_valid, chunk):
    b, lp, w3 = qkv.shape
    C = chunk
    nc = lp // C
    H, HD, W = GDN_HEADS, GDN_HEAD_DIM, GDN_WIDTH
    abt = small[:, :, :2 * H].reshape(b, nc, C, 2 * H).transpose(0, 1, 3, 2)
    cb = jnp.pad(conv_buf, ((0, 0), (SUBLANES - (GDN_CONV - 1), 0), (0, 0)))
    prow = jnp.zeros((2, LANES), F32).at[0, :H].set(a_log).at[1, :H].set(dt_bias)
    alog_r = jnp.broadcast_to(a_log[:, None], (H, C))
    dtb_r = jnp.broadcast_to(dt_bias[:, None], (H, C))
    ix = jnp.arange(C)
    tri = (ix[:, None] >= ix[None, :]).astype(F32)
    triu = tri.T
    hid = jnp.arange(W) // HD
    bd = (hid[:, None] == hid[None, :]).astype(F32)
    lane = jnp.arange(LANES)
    expg = (lane[:, None] == hid[None, :]).astype(F32)
    expb = (lane[:, None] == hid[None, :] + H).astype(F32)
    const = lambda shape: pl.BlockSpec(shape, lambda i, j: (0,) * len(shape))
    o, s_fin = pl.pallas_call(
        functools.partial(_gdn_kernel, chunk=C, l_valid=l_valid, l_pad=lp),
        grid=(b, nc),
        in_specs=[pl.BlockSpec((None, C, w3), lambda i, j: (i, j, 0)),
                  pl.BlockSpec((None, C, W), lambda i, j: (i, j, 0)),
                  pl.BlockSpec((None, C, LANES), lambda i, j: (i, j, 0)),
                  pl.BlockSpec((None, None, 2 * H, C), lambda i, j: (i, j, 0, 0)),
                  pl.BlockSpec((None, SUBLANES, w3), lambda i, j: (i, 0, 0)),
                  const((GDN_CONV, w3)), const((2, LANES)), const((H, C)), const((H, C)), const((1, HD)),
                  pl.BlockSpec((None, H, HD, HD), lambda i, j: (i, 0, 0, 0)),
                  const((C, C)), const((C, C)), const((W, W)), const((LANES, W)), const((LANES, W))],
        out_specs=[pl.BlockSpec((None, C, W), lambda i, j: (i, j, 0)),
                   pl.BlockSpec((None, H, HD, HD), lambda i, j: (i, 0, 0, 0))],
        out_shape=[jax.ShapeDtypeStruct((b, lp, W), BF16), jax.ShapeDtypeStruct((b, H, HD, HD), F32)],
        scratch_shapes=[pltpu.VMEM((SUBLANES + C, w3), F32)],
        compiler_params=_params("parallel", "arbitrary"),
        name="gdn",
    )(qkv, z, small, abt, cb, conv_w, prow, alog_r, dtb_r, norm_w.reshape(1, HD), s0,
      tri, triu, bd, expg, expb)
    return o, s_fin


def _pool_rows(x, w):
    nb = x.shape[0] // CMP_BLOCK
    return jnp.sum(x.reshape(nb, CMP_BLOCK, x.shape[1]) * w[None], axis=1)


def _pool_kernel(x_ref, w_ref, o_ref):
    o_ref[...] = _pool_rows(x_ref[...], w_ref[...])


def _pool(kv, wexp, rows):
    b, t, wd = kv.shape
    return pl.pallas_call(
        _pool_kernel,
        grid=(b, t // rows),
        in_specs=[pl.BlockSpec((None, rows, wd), lambda i, j: (i, j, 0)),
                  pl.BlockSpec((CMP_BLOCK, wd), lambda i, j: (0, 0))],
        out_specs=pl.BlockSpec((None, rows // CMP_BLOCK, wd), lambda i, j: (i, j, 0)),
        out_shape=jax.ShapeDtypeStruct((b, t // CMP_BLOCK, wd), F32),
        compiler_params=_params("parallel", "parallel"),
        name="pool",
    )(kv, wexp)


def _pool_paged_kernel(pt_ref, *refs, pages):
    del pt_ref
    w = refs[pages][...]
    o_ref = refs[pages + 1]
    o_ref[...] = jnp.concatenate([_pool_rows(refs[p][...], w) for p in range(pages)], 0)


def _pool_paged(cache, page_table, wexp, pages):
    _, ps, wd = cache.shape
    b, n_pages = page_table.shape
    per = ps // CMP_BLOCK
    page_spec = lambda p: pl.BlockSpec((None, ps, wd), lambda i, j, pt: (pt[i, j * pages + p], 0, 0))
    return pl.pallas_call(
        functools.partial(_pool_paged_kernel, pages=pages),
        grid_spec=pltpu.PrefetchScalarGridSpec(
            num_scalar_prefetch=1,
            grid=(b, n_pages // pages),
            in_specs=[page_spec(p) for p in range(pages)]
                     + [pl.BlockSpec((CMP_BLOCK, wd), lambda i, j, pt: (0, 0))],
            out_specs=pl.BlockSpec((None, pages * per, wd), lambda i, j, pt: (i, j, 0))),
        out_shape=jax.ShapeDtypeStruct((b, n_pages * per, wd), F32),
        compiler_params=_params("parallel", "arbitrary"),
        name="pool_paged",
    )(page_table, *([cache] * pages), wexp)


def _cmp_probs_and_select(sc, pos, tq):
    n = sc.shape[1]
    lane = lax.broadcasted_iota(jnp.int32, (1, n), 1)
    maskc = ((lane + 1) * CMP_BLOCK - 1) <= pos
    ps = []
    imp = None
    for g in range(NSA_GROUP):
        s = jnp.where(maskc, sc[g * tq:(g + 1) * tq], NEG)
        m = jnp.max(s, -1, keepdims=True)
        e = jnp.where(maskc, jnp.exp(s - m), 0.0)
        den = jnp.sum(e, -1, keepdims=True)
        p = e / jnp.where(den > 0.0, den, 1.0)
        ps.append(p)
        imp = p if imp is None else imp + p
    even = (lane & 1) == 0
    imp2 = imp + jnp.where(even, pltpu.roll(imp, n - 1, 1), pltpu.roll(imp, 1, 1))
    blk = lane >> 1
    valid = blk * SEL_BLOCK <= pos
    cur = pos >> 6
    forced = (blk == 0) | (blk == cur) | (blk == cur - 1)
    score = jnp.where(valid, jnp.where(forced, BIG, imp2), -1.0)
    work = jnp.where(even, score, -2.0)
    lanef = lane.astype(F32)
    sel = jnp.zeros(work.shape, F32)
    for _ in range(SEL_TOPK):
        m = jnp.max(work, -1, keepdims=True)
        first = jnp.min(jnp.where(work == m, lanef, 1e9), -1, keepdims=True)
        pick = lanef == first
        sel = jnp.where(pick, 1.0, sel)
        work = jnp.where(pick, -2.0, work)
    sel = jnp.where(score >= 0.0, sel, 0.0)
    return jnp.concatenate(ps, 0), sel + pltpu.roll(sel, 1, 1)


def _masked_softmax_rows(s, mf):
    s = jnp.where(mf > 0.5, s, NEG)
    m = jnp.max(s, -1, keepdims=True)
    e = jnp.exp(s - m) * mf
    den = jnp.sum(e, -1, keepdims=True)
    return e / jnp.where(den > 0.0, den, 1.0)


def _online_update(carry, s, mf, v):
    m, l, acc = carry
    s = jnp.where(mf > 0.5, s, NEG)
    m_new = jnp.maximum(m, jnp.max(s, -1, keepdims=True))
    alpha = jnp.exp(m - m_new)
    p = jnp.exp(s - m_new) * mf
    l = alpha * l + jnp.sum(p, -1, keepdims=True)
    acc = alpha * acc + _dot(p.astype(BF16), v)
    return m_new, l, acc


def _nsa_prompt_kernel(q_ref, kct_ref, vc_ref, kst_ref, vs_ref, kwt_ref, vw_ref, gate_ref, e_ref,
                       o_ref, mask_scr, *, n_kt):
    TQ, TK, G, HD = NSA_TQ, NSA_TK, NSA_GROUP, NSA_HEAD_DIM
    i = pl.program_id(2)
    t0 = i * TQ
    q = q_ref[...]
    q4 = jnp.concatenate([q[:, g * HD:(g + 1) * HD] for g in range(G)], 0)
    pos = t0 + lax.broadcasted_iota(jnp.int32, (TQ, 1), 0)

    pc, sel2 = _cmp_probs_and_select(_dot(q4, kct_ref[...]), pos, TQ)
    oc4 = _dot(pc.astype(BF16), vc_ref[...])

    selb = sel2.astype(BF16)
    for kt in range(n_kt):
        mask_scr[kt] = _dot(selb, e_ref[:, kt * TK:(kt + 1) * TK])
    n_used = (t0 + TQ + TK - 1) // TK

    def body(kt, carry):
        kpos = kt * TK + lax.broadcasted_iota(jnp.int32, (1, TK), 1)
        mf = jnp.where((mask_scr[kt] > 0.5) & (kpos <= pos), 1.0, 0.0)
        mf4 = jnp.concatenate([mf] * G, 0)
        v = vs_ref[pl.ds(pl.multiple_of(kt * TK, TK), TK), :]
        return _online_update(carry, _dot(q4, kst_ref[kt]), mf4, v)

    init = (jnp.full((G * TQ, 1), NEG, F32), jnp.zeros((G * TQ, 1), F32), jnp.zeros((G * TQ, HD), F32))
    _, l, acc = lax.fori_loop(0, n_used, body, init)
    os4 = acc / jnp.where(l > 0.0, l, 1.0)

    n_wt = WINDOW // TQ + 1
    kw = jnp.concatenate([kwt_ref[i + j] for j in range(n_wt)], 1)
    vw = vw_ref[pl.ds(pl.multiple_of(t0, TQ), WINDOW + TQ), :]
    cidx = lax.broadcasted_iota(jnp.int32, (1, WINDOW + TQ), 1)
    diff = lax.broadcasted_iota(jnp.int32, (TQ, 1), 0) + WINDOW - cidx
    mw = jnp.where((diff >= 0) & (diff < WINDOW) & (t0 - WINDOW + cidx >= 0), 1.0, 0.0)
    pw = _masked_softmax_rows(_dot(q4, kw), jnp.concatenate([mw] * G, 0))
    ow4 = _dot(pw.astype(BF16), vw)

    gs = _sig(gate_ref[...])
    outs = []
    for g in range(G):
        rs = slice(g * TQ, (g + 1) * TQ)
        outs.append(gs[:, g:g + 1] * oc4[rs] + gs[:, G + g:G + g + 1] * os4[rs]
                    + gs[:, 2 * G + g:2 * G + g + 1] * ow4[rs])
    o_ref[...] = jnp.concatenate(outs, 1).astype(BF16)


def _expand_matrix(n_lanes, n_keys):
    return (jnp.arange(n_keys)[None, :] // CMP_BLOCK == jnp.arange(n_lanes)[:, None]).astype(BF16)


def _nsa_prompt(qn, kvb, kvs, kvw, gate):
    b, l, _ = qn.shape
    TQ, TK, G, HD, KV = NSA_TQ, NSA_TK, NSA_GROUP, NSA_HEAD_DIM, NSA_KV_HEADS
    n_cmp = kvb.shape[1]
    assert n_cmp <= LANES and l % TK == 0 and WINDOW % TQ == 0
    heads = lambda t: t.reshape(b, -1, 2, KV, HD)
    kc = heads(kvb).astype(BF16)
    kct = jnp.pad(kc[:, :, 0].transpose(0, 2, 3, 1), ((0, 0), (0, 0), (0, 0), (0, LANES - n_cmp)))
    vc = jnp.pad(kc[:, :, 1].transpose(0, 2, 1, 3), ((0, 0), (0, 0), (0, LANES - n_cmp), (0, 0)))
    ks = heads(kvs).astype(BF16)
    n_kt = l // TK
    kst = ks[:, :, 0].transpose(0, 2, 3, 1).reshape(b, KV, HD, n_kt, TK).transpose(0, 1, 3, 2, 4)
    vs = ks[:, :, 1].transpose(0, 2, 1, 3)
    kw = jnp.pad(heads(kvw).astype(BF16), ((0, 0), (WINDOW, 0), (0, 0), (0, 0), (0, 0)))
    lw = l + WINDOW
    kwt = kw[:, :, 0].transpose(0, 2, 3, 1).reshape(b, KV, HD, lw // TQ, TQ).transpose(0, 1, 3, 2, 4)
    vw = kw[:, :, 1].transpose(0, 2, 1, 3)
    gate_r = gate.reshape(b, l, 3, KV, G).transpose(0, 3, 1, 2, 4).reshape(b, KV, l, 3 * G)
    emat = _expand_matrix(LANES, l)
    per_head = lambda *blk: pl.BlockSpec((None, None) + blk, lambda bi, hi, i: (bi, hi) + (0,) * len(blk))
    return pl.pallas_call(
        functools.partial(_nsa_prompt_kernel, n_kt=n_kt),
        grid=(b, KV, l // TQ),
        in_specs=[pl.BlockSpec((None, TQ, G * HD), lambda bi, hi, i: (bi, i, hi)),
                  per_head(HD, LANES), per_head(LANES, HD),
                  per_head(n_kt, HD, TK), per_head(l, HD),
                  per_head(lw // TQ, HD, TQ), per_head(lw, HD),
                  pl.BlockSpec((None, None, TQ, 3 * G), lambda bi, hi, i: (bi, hi, i, 0)),
                  pl.BlockSpec((LANES, l), lambda bi, hi, i: (0, 0))],
        out_specs=pl.BlockSpec((None, TQ, G * HD), lambda bi, hi, i: (bi, i, hi)),
        out_shape=jax.ShapeDtypeStruct((b, l, NSA_WIDTH), BF16),
        scratch_shapes=[pltpu.VMEM((n_kt, TQ, TK), F32)],
        compiler_params=_params("parallel", "parallel", "arbitrary"),
        name="nsa_prompt",
    )(qn, kct, vc, kst, vs, kwt, vw, gate_r, emat)


LS = SUBLANES
ROWS_S = NSA_KV_HEADS * NSA_GROUP * LS


def _row_token(rows):
    return lax.broadcasted_iota(jnp.int32, (rows, 1), 0) & (LS - 1)


def _nsa_sample_a_kernel(q_ref, kvb_ref, cw_ref, new_ref, oc_ref, ow_ref, sel_ref, *, past, l_new):
    HW = NSA_KV_HEADS * NSA_HEAD_DIM
    R = NSA_GROUP * LS
    qb = q_ref[...]
    kvb = kvb_ref[...]
    sc = _dot_nt(qb, kvb[:, :HW].astype(BF16))
    pos = past + lax.broadcasted_iota(jnp.int32, (LS, 1), 0)
    ps = []
    for h in range(NSA_KV_HEADS):
        p, sel2 = _cmp_probs_and_select(sc[h * R:(h + 1) * R], pos, LS)
        ps.append(p)
        sel_ref[h] = sel2
    oc_ref[...] = _dot(jnp.concatenate(ps, 0).astype(BF16), kvb[:, HW:].astype(BF16))

    cw = cw_ref[...]
    nw = new_ref[...]
    wb = cw.shape[0]
    tok = _row_token(ROWS_S)
    c1 = lax.broadcasted_iota(jnp.int32, (1, wb), 1)
    d1 = tok + wb - c1
    m1 = jnp.where((d1 >= 0) & (d1 < WINDOW) & (past - wb + c1 >= 0), 1.0, 0.0)
    c2 = lax.broadcasted_iota(jnp.int32, (1, nw.shape[0]), 1)
    d2 = tok - c2
    m2 = jnp.where((d2 >= 0) & (d2 < WINDOW) & (c2 < l_new), 1.0, 0.0)
    s1 = jnp.where(m1 > 0.5, _dot_nt(qb, cw[:, :HW].astype(BF16)), NEG)
    s2 = jnp.where(m2 > 0.5, _dot_nt(qb, nw[:, :HW].astype(BF16)), NEG)
    m = jnp.maximum(jnp.max(s1, -1, keepdims=True), jnp.max(s2, -1, keepdims=True))
    e1 = jnp.exp(s1 - m) * m1
    e2 = jnp.exp(s2 - m) * m2
    den = jnp.sum(e1, -1, keepdims=True) + jnp.sum(e2, -1, keepdims=True)
    den = jnp.where(den > 0.0, den, 1.0)
    ow_ref[...] = (_dot((e1 / den).astype(BF16), cw[:, HW:].astype(BF16))
                   + _dot((e2 / den).astype(BF16), nw[:, HW:].astype(BF16)))


def _nsa_sample_a(qbd, kvb_all, cache_win, kvw_new, past, l_new):
    b, ncp, wd = kvb_all.shape
    wb = cache_win.shape[1]
    nr = kvw_new.shape[1]
    HW = NSA_KV_HEADS * NSA_HEAD_DIM
    return pl.pallas_call(
        functools.partial(_nsa_sample_a_kernel, past=past, l_new=l_new),
        grid=(b,),
        in_specs=[pl.BlockSpec((None, ROWS_S, HW), lambda i: (i, 0, 0)),
                  pl.BlockSpec((None, ncp, wd), lambda i: (i, 0, 0)),
                  pl.BlockSpec((None, wb, wd), lambda i: (i, 0, 0)),
                  pl.BlockSpec((None, nr, wd), lambda i: (i, 0, 0))],
        out_specs=[pl.BlockSpec((None, ROWS_S, HW), lambda i: (i, 0, 0)),
                   pl.BlockSpec((None, ROWS_S, HW), lambda i: (i, 0, 0)),
                   pl.BlockSpec((None, NSA_KV_HEADS, LS, ncp), lambda i: (i, 0, 0, 0))],
        out_shape=[jax.ShapeDtypeStruct((b, ROWS_S, HW), F32), jax.ShapeDtypeStruct((b, ROWS_S, HW), F32),
                   jax.ShapeDtypeStruct((b, NSA_KV_HEADS, LS, ncp), F32)],
        compiler_params=_params("parallel"),
        name="nsa_sample_a",
    )(qbd, kvb_all, cache_win, kvw_new)


def _nsa_sample_sel_kernel(pt_ref, *refs, pages, l_new):
    del pt_ref
    page_refs = refs[:pages]
    q_ref, sel_ref, tail_ref, new_ref, e_ref, oc_ref, ow_ref, gate_ref, o_ref, m_scr, l_scr, acc_scr = refs[pages:]
    HW = NSA_KV_HEADS * NSA_HEAD_DIM
    i = pl.program_id(1)

    @pl.when(i == 0)
    def _():
        m_scr[...] = jnp.full_like(m_scr, NEG)
        l_scr[...] = jnp.zeros_like(l_scr)
        acc_scr[...] = jnp.zeros_like(acc_scr)

    qb = q_ref[...]
    kv = jnp.concatenate([r[...] for r in page_refs], 0)
    emat = e_ref[...]
    mh = [_dot(sel_ref[h].astype(BF16), emat) for h in range(NSA_KV_HEADS)]
    mf = jnp.concatenate([mh[h] for h in range(NSA_KV_HEADS) for _ in range(NSA_GROUP)], 0)
    carry = _online_update((m_scr[...], l_scr[...], acc_scr[...]),
                           _dot_nt(qb, kv[:, :HW].astype(BF16)), mf, kv[:, HW:].astype(BF16))
    m_scr[...], l_scr[...], acc_scr[...] = carry

    @pl.when(i == pl.num_programs(1) - 1)
    def _():
        nw = new_ref[...]
        tok = _row_token(ROWS_S)
        c2 = lax.broadcasted_iota(jnp.int32, (1, nw.shape[0]), 1)
        flag = jnp.concatenate(
            [tail_ref[h][:, 0:1] for h in range(NSA_KV_HEADS) for _ in range(NSA_GROUP)], 0)
        mt = jnp.where((c2 <= tok) & (c2 < l_new) & (flag > 0.5), 1.0, 0.0)
        _, l, acc = _online_update((m_scr[...], l_scr[...], acc_scr[...]),
                                   _dot_nt(qb, nw[:, :HW].astype(BF16)), mt, nw[:, HW:].astype(BF16))
        osel = acc / jnp.where(l > 0.0, l, 1.0)
        gs = _sig(gate_ref[...])
        o_ref[...] = gs[:, 0:1] * oc_ref[...] + gs[:, 1:2] * osel + gs[:, 2:3] * ow_ref[...]


def _nsa_sample_sel(qbd, selmask, cache_sel, page_table, kvs_new, oc, ow, gate_rows, pages, l_new):
    b, n_pages = page_table.shape
    _, ps, wd = cache_sel.shape
    ncp = selmask.shape[-1]
    nr = kvs_new.shape[1]
    HW = NSA_KV_HEADS * NSA_HEAD_DIM
    lanes_per_step = pages * ps // CMP_BLOCK
    assert lanes_per_step % LANES == 0 and n_pages % pages == 0
    n_steps = n_pages // pages
    emat = _expand_matrix(lanes_per_step, pages * ps)
    page_spec = lambda p: pl.BlockSpec((None, ps, wd), lambda bi, i, pt: (pt[bi, i * pages + p], 0, 0))
    rows = lambda: pl.BlockSpec((None, ROWS_S, HW), lambda bi, i, pt: (bi, 0, 0))
    tail_block = n_steps * lanes_per_step // LANES
    return pl.pallas_call(
        functools.partial(_nsa_sample_sel_kernel, pages=pages, l_new=l_new),
        grid_spec=pltpu.PrefetchScalarGridSpec(
            num_scalar_prefetch=1,
            grid=(b, n_steps),
            in_specs=[page_spec(p) for p in range(pages)] + [
                rows(),
                pl.BlockSpec((None, NSA_KV_HEADS, LS, lanes_per_step), lambda bi, i, pt: (bi, 0, 0, i)),
                pl.BlockSpec((None, NSA_KV_HEADS, LS, LANES), lambda bi, i, pt: (bi, 0, 0, tail_block)),
                pl.BlockSpec((None, nr, wd), lambda bi, i, pt: (bi, 0, 0)),
                pl.BlockSpec(emat.shape, lambda bi, i, pt: (0, 0)),
                rows(), rows(),
                pl.BlockSpec((None, ROWS_S, 3), lambda bi, i, pt: (bi, 0, 0))],
            out_specs=rows(),
            scratch_shapes=[pltpu.VMEM((ROWS_S, 1), F32), pltpu.VMEM((ROWS_S, 1), F32),
                            pltpu.VMEM((ROWS_S, HW), F32)]),
        out_shape=jax.ShapeDtypeStruct((b, ROWS_S, HW), F32),
        compiler_params=_params("parallel", "arbitrary"),
        name="nsa_sample_sel",
    )(page_table, *([cache_sel] * pages), qbd, selmask, selmask, kvs_new, emat, oc, ow, gate_rows)


def _nsa_sample(qn, kvc, kvs, kvw, gate, cache_cmp, cache_sel, cache_win, page_table, wexp):
    b, l, _ = qn.shape
    KV, G, HD = NSA_KV_HEADS, NSA_GROUP, NSA_HEAD_DIM
    n_pages = page_table.shape[1]
    ps = cache_cmp.shape[1]
    past = n_pages * ps
    assert l <= LS and ps == PAGE_SIZE
    pages = min(PAGES_PER_STEP, n_pages)
    pad_rows = lambda t: jnp.pad(t, ((0, 0), (0, ps - l), (0, 0)))
    kvb_past = _pool_paged(cache_cmp, page_table, wexp, pages)
    kvb_tail = _pool(pad_rows(kvc), wexp, ps)
    n_blk = kvb_past.shape[1] + kvb_tail.shape[1]
    ncp = -(-n_blk // LANES) * LANES
    kvb_all = jnp.pad(jnp.concatenate([kvb_past, kvb_tail], 1), ((0, 0), (0, ncp - n_blk), (0, 0)))
    q5 = jnp.pad(qn.reshape(b, l, KV, G, HD).transpose(0, 2, 3, 1, 4), ((0, 0),) * 3 + ((0, LS - l), (0, 0)))
    qbd = jnp.einsum("bhgld,hk->bhglkd", q5, jnp.eye(KV, dtype=q5.dtype)).reshape(b, ROWS_S, KV * HD)
    oc, ow, selmask = _nsa_sample_a(qbd, kvb_all, cache_win, pad_rows(kvw), past, l)
    gate_rows = jnp.pad(gate.reshape(b, l, 3, KV, G).transpose(0, 3, 4, 1, 2),
                        ((0, 0),) * 3 + ((0, LS - l), (0, 0))).reshape(b, ROWS_S, 3)
    o = _nsa_sample_sel(qbd, selmask, cache_sel, page_table, pad_rows(kvs), oc, ow, gate_rows, pages, l)
    o = o.reshape(b, KV, G, LS, KV, HD)
    o = jnp.stack([o[:, h, :, :l, h] for h in range(KV)], 1)
    return o.transpose(0, 3, 1, 2, 4).reshape(b, l, NSA_WIDTH).astype(BF16)


def _merge_kernel(x_ref, gt_ref, og_ref, on_ref, m_ref, wbg_ref, wbn_ref, wo_ref, lg_ref, lb_ref, o_ref,
                  *, alpha):
    d = x_ref.shape[1]
    m = m_ref[...].astype(F32)
    mix = m[:, :d] * _dot(og_ref[...], wbg_ref[...]) + m[:, d:] * _dot(on_ref[...], wbn_ref[...])
    y = _dot(mix.astype(BF16), wo_ref[...])
    o_ref[...] = _layer_norm(alpha * x_ref[...] + gt_ref[...] * y, lg_ref[...], lb_ref[...])


def _merge(x, mods, kind, rows_per_seq, o_gdn, o_nsa, msig, wbg, wbn, wo, lg, lb, alpha, tm):
    n, d = x.shape
    tps = max(rows_per_seq // tm, 1)
    row = lambda wd: pl.BlockSpec((tm, wd), lambda i: (i, 0))
    full = lambda a: pl.BlockSpec(a.shape, lambda i: (0, 0))
    return pl.pallas_call(
        functools.partial(_merge_kernel, alpha=alpha),
        grid=(n // tm,),
        in_specs=[row(d), _mod_spec(kind, tm, d, tps, 5), row(o_gdn.shape[1]), row(o_nsa.shape[1]),
                  row(2 * d), full(wbg), full(wbn), full(wo), full(lg), full(lb)],
        out_specs=row(d),
        out_shape=jax.ShapeDtypeStruct((n, d), F32),
        compiler_params=_params("parallel"),
        name="merge",
    )(x, mods, o_gdn, o_nsa, msig, wbg, wbn, wo, lg, lb)


def _row_tile(n, pref):
    t = min(pref, n)
    while n % t:
        t //= 2
    return t


def _ff_tile(dff):
    for parts in (2, 4, 1, 11, 22):
        if dff % parts == 0 and (dff // parts) % LANES == 0:
            return dff // parts
    return dff


def _layer(x, mod, kind, past, lw, alpha):
    b, l, d = x.shape
    n = b * l
    (wg1, wu1, wd1, wg2, wu2, wd2, w_in_parts, conv_w, a_log, dt_bias, norm_w, wexp,
     wbg, wbn, wo, ln_g, ln_b) = lw
    tm = _row_tile(l if kind == "seq" else n, 512)
    tm_in = _row_tile(l if kind == "seq" else n, 256)
    tf = _ff_tile(wg1.shape[1])
    lg = lambda i: ln_g[i].reshape(1, d)
    lb = lambda i: ln_b[i].reshape(1, d)

    x1 = _ffn(x.reshape(n, d), mod, kind, l, 0, wg1, wu1, wd1, lg(0), lb(0), alpha, tm, tf)

    qkv, z, qn, kvc, kvs, kvw, msig, small = _inproj(x1, mod, kind, l, w_in_parts, tm_in)
    seq = lambda t: t.reshape(b, l, t.shape[-1])
    qkv, z, qn, kvc, kvs, kvw, small = [seq(t) for t in (qkv, z, qn, kvc, kvs, kvw, small)]
    gate = small[:, :, 2 * GDN_HEADS:2 * GDN_HEADS + 3 * NSA_Q_HEADS]

    if past is None:
        conv_buf = jnp.zeros((b, GDN_CONV - 1, 3 * GDN_WIDTH), F32)
        s0 = jnp.zeros((b, GDN_HEADS, GDN_HEAD_DIM, GDN_HEAD_DIM), F32)
        kvb = _pool(kvc, wexp, l)
        o_nsa = _nsa_prompt(qn, kvb, kvs, kvw, gate)
        win_buf = kvw[:, l - min(WINDOW, l):]
    else:
        s0, conv_buf, cache_cmp, cache_sel, cache_win, page_table = past
        o_nsa = _nsa_sample(qn, kvc, kvs, kvw, gate, cache_cmp, cache_sel, cache_win, page_table, wexp)
        win_buf = jnp.concatenate([cache_win, kvw], 1)[:, l:]

    chunk = min(GDN_CHUNK, -(-l // SUBLANES) * SUBLANES)
    lp = -(-l // chunk) * chunk
    padl = lambda t: jnp.pad(t, ((0, 0), (0, lp - l), (0, 0)))
    o_gdn, s_new = _gdn(padl(qkv), padl(z), padl(small), conv_buf, s0, conv_w, a_log, dt_bias, norm_w, l, chunk)
    conv_new = jnp.concatenate([conv_buf, qkv], 1)[:, -(GDN_CONV - 1):]

    x2 = _merge(x1, mod, kind, l, o_gdn[:, :l].reshape(n, GDN_WIDTH), o_nsa.reshape(n, NSA_WIDTH), msig,
                wbg, wbn, wo, lg(1), lb(1), alpha, tm)
    x3 = _ffn(x2, mod, kind, l, 6, wg2, wu2, wd2, lg(2), lb(2), alpha, tm, tf)

    rows = lambda t: t.reshape(b, t.shape[1], 2, NSA_KV_HEADS, NSA_HEAD_DIM)
    return x3.reshape(b, l, d), (s_new, conv_new, rows(kvc), rows(kvs), rows(win_buf))


def _split_w_in(w_in, d):
    splits = (3 * GDN_WIDTH, GDN_WIDTH, GDN_HEADS, GDN_HEADS, NSA_WIDTH, 2 * KV_WIDTH, 2 * KV_WIDTH,
              2 * KV_WIDTH, 3 * NSA_Q_HEADS, 2 * d)
    offs = [0]
    for s in splits:
        offs.append(offs[-1] + s)
    qkv, z, a, bb, q, kc, ks, kw, gate, merge = [w_in[:, offs[i]:offs[i + 1]] for i in range(len(splits))]
    n_small = 2 * GDN_HEADS + 3 * NSA_Q_HEADS
    small = jnp.pad(jnp.concatenate([a, bb, gate], 1), ((0, 0), (0, LANES - n_small)))
    return [t.astype(BF16) for t in (qkv, z, q, kc, ks, kw, merge, small)]


def kernel(x_prompt, x_sample, c_prompt, c_sample, state_gdn, state_gdn_conv, cache_cmp_kv, cache_sel_kv, cache_win_kv, page_table, ln_g, ln_b, w_ada, b_ada, w_ff1_gu, w_ff1_dn, w_ff2_gu, w_ff2_dn, w_in, gdn_conv_w, gdn_a_log, gdn_dt_bias, gdn_norm_w, nsa_w_cmp, w_br_gdn, w_br_nsa, w_out):
    depth = w_in.shape[0]
    alpha = (2.0 * depth) ** 0.25
    bp, lp, d = x_prompt.shape
    bs, ls, _ = x_sample.shape
    y_p, y_s = x_prompt, x_sample
    p_st, s_st = [], []
    for l in range(depth):
        dff = w_ff1_dn.shape[1]
        bf = lambda t: t.astype(BF16)
        wexp = jnp.broadcast_to(nsa_w_cmp[l].transpose(1, 0, 2)[:, :, :, None],
                                (CMP_BLOCK, 2, NSA_KV_HEADS, NSA_HEAD_DIM)).reshape(CMP_BLOCK, 2 * KV_WIDTH)
        lw = (bf(w_ff1_gu[l][:, :dff]), bf(w_ff1_gu[l][:, dff:]), bf(w_ff1_dn[l]),
              bf(w_ff2_gu[l][:, :dff]), bf(w_ff2_gu[l][:, dff:]), bf(w_ff2_dn[l]),
              _split_w_in(w_in[l], d), gdn_conv_w[l], gdn_a_log[l], gdn_dt_bias[l], gdn_norm_w[l], wexp,
              bf(w_br_gdn[l]), bf(w_br_nsa[l]), bf(w_out[l]), ln_g[l], ln_b[l])
        c_all = jnp.concatenate([c_prompt, c_sample], 0)
        r = c_all.shape[0]
        rp = -(-r // SUBLANES) * SUBLANES
        mod = _ada(jnp.pad(c_all, ((0, rp - r), (0, 0))), w_ada[l], b_ada[l])
        mod_p = mod[:bp].reshape(bp * 9, 1, d)
        mod_s = jnp.repeat(mod[bp:bp + bs], ls, axis=0)
        y_p, st_p = _layer(y_p, mod_p, "seq", None, lw, alpha)
        past = (state_gdn[l], state_gdn_conv[l], cache_cmp_kv[l].reshape(-1, PAGE_SIZE, 2 * KV_WIDTH),
                cache_sel_kv[l].reshape(-1, PAGE_SIZE, 2 * KV_WIDTH),
                cache_win_kv[l].reshape(bs, -1, 2 * KV_WIDTH), page_table)
        y_s, st_s = _layer(y_s, mod_s, "tok", past, lw, alpha)
        p_st.append(st_p)
        s_st.append(st_s)
    p_out = [jnp.stack(t) for t in zip(*p_st)]
    s_out = [jnp.stack(t) for t in zip(*s_st)]
    return (y_p, y_s, *p_out, *s_out)
```

```python
import functools

import jax
import jax.numpy as jnp
from jax import lax
from jax.experimental import pallas as pl
from jax.experimental.pallas import tpu as pltpu

F32 = jnp.float32
BF16 = jnp.bfloat16
HIGHEST = lax.Precision.HIGHEST

GDN_HEADS = 8
GDN_HEAD_DIM = 64
GDN_WIDTH = GDN_HEADS * GDN_HEAD_DIM
GDN_CONV = 4
GDN_CHUNK = 64
NSA_Q_HEADS = 8
NSA_KV_HEADS = 2
NSA_HEAD_DIM = 64
NSA_GROUP = NSA_Q_HEADS // NSA_KV_HEADS
NSA_WIDTH = NSA_Q_HEADS * NSA_HEAD_DIM
KV_WIDTH = NSA_KV_HEADS * NSA_HEAD_DIM
KV2 = 2 * KV_WIDTH
CMP_BLOCK = 32
SEL_BLOCK = 64
SEL_TOPK = 16
WINDOW = 512
PAGE_SIZE = 128
NEG = -1e30
BIG = 1e4
LN_EPS = 1e-5

SUBLANES = 8
LANES = 128
VMEM_LIMIT = 52 * 1024 * 1024

NSA_TQ = 128
NSA_TK = 256
PAGES_PER_STEP = 32


def _sig(x):
    return 1.0 / (1.0 + jnp.exp(-x))


def _softplus(x):
    return jnp.maximum(x, 0.0) + jnp.log(1.0 + jnp.exp(-jnp.abs(x)))


def _layer_norm(r, g, b):
    mu = jnp.mean(r, -1, keepdims=True)
    d = r - mu
    var = jnp.mean(d * d, -1, keepdims=True)
    return d * lax.rsqrt(var + LN_EPS) * g + b


def _dot(a, b):
    return jnp.dot(a, b, preferred_element_type=F32)


def _dot_nt(a, b):
    return lax.dot_general(a, b, (((1,), (1,)), ((), ())), preferred_element_type=F32)


def _dot_tn(a, b):
    return lax.dot_general(a, b, (((0,), (0,)), ((), ())), preferred_element_type=F32)


def _dot_split2(a, sel):
    hi = a.astype(BF16)
    lo = (a - hi.astype(F32)).astype(BF16)
    return _dot(hi, sel) + _dot(lo, sel)


def _params(*sem):
    return pltpu.CompilerParams(dimension_semantics=sem, vmem_limit_bytes=VMEM_LIMIT)


def _mod_spec(kind, tm, d, tiles_per_seq, k):
    if kind == "seq":
        return pl.BlockSpec((None, 1, d), lambda i, *_: ((i // tiles_per_seq) * 9 + k, 0, 0))
    return pl.BlockSpec((tm, d), lambda i, *_: (i, k))


def _ada_kernel(c_ref, w_ref, b_ref, o_ref):
    c = c_ref[...]
    h = (c * _sig(c)).astype(BF16)
    o_ref[...] = _dot(h, w_ref[...].astype(BF16)) + b_ref[...]


def _ada(c, w, b):
    r, d = c.shape
    n = w.shape[1]
    tn = d
    return pl.pallas_call(
        _ada_kernel,
        grid=(n // tn,),
        in_specs=[pl.BlockSpec((r, d), lambda j: (0, 0)),
                  pl.BlockSpec((d, tn), lambda j: (0, j)),
                  pl.BlockSpec((1, tn), lambda j: (0, j))],
        out_specs=pl.BlockSpec((r, tn), lambda j: (0, j)),
        out_shape=jax.ShapeDtypeStruct((r, n), F32),
        compiler_params=_params("arbitrary"),
        name="ada",
    )(c, w, b.reshape(1, n))


def _ffn_kernel(x_ref, sh_ref, sc_ref, gt_ref, wg_ref, wu_ref, wd_ref, lg_ref, lb_ref, o_ref,
                h_scr, acc_scr, *, alpha):
    j = pl.program_id(1)

    @pl.when(j == 0)
    def _():
        h_scr[...] = (x_ref[...] * (1.0 + sc_ref[...]) + sh_ref[...]).astype(BF16)
        acc_scr[...] = jnp.zeros_like(acc_scr)

    h = h_scr[...]
    g = _dot(h, wg_ref[...])
    u = _dot(h, wu_ref[...])
    a = (g * _sig(g) * u).astype(BF16)
    acc_scr[...] += _dot(a, wd_ref[...])

    @pl.when(j == pl.num_programs(1) - 1)
    def _():
        r = alpha * x_ref[...] + (0.5 * gt_ref[...]) * acc_scr[...]
        o_ref[...] = _layer_norm(r, lg_ref[...], lb_ref[...])


def _ffn(x, mods, kind, rows_per_seq, k0, wg, wu, wd, lg, lb, alpha, tm, tf):
    n, d = x.shape
    dff = wg.shape[1]
    tps = max(rows_per_seq // tm, 1)
    ms = lambda k: _mod_spec(kind, tm, d, tps, k)
    return pl.pallas_call(
        functools.partial(_ffn_kernel, alpha=alpha),
        grid=(n // tm, dff // tf),
        in_specs=[pl.BlockSpec((tm, d), lambda i, j: (i, 0)),
                  ms(k0), ms(k0 + 1), ms(k0 + 2),
                  pl.BlockSpec((d, tf), lambda i, j: (0, j)),
                  pl.BlockSpec((d, tf), lambda i, j: (0, j)),
                  pl.BlockSpec((tf, d), lambda i, j: (j, 0)),
                  pl.BlockSpec((1, d), lambda i, j: (0, 0)),
                  pl.BlockSpec((1, d), lambda i, j: (0, 0))],
        out_specs=pl.BlockSpec((tm, d), lambda i, j: (i, 0)),
        out_shape=jax.ShapeDtypeStruct((n, d), F32),
        scratch_shapes=[pltpu.VMEM((tm, d), BF16), pltpu.VMEM((tm, d), F32)],
        compiler_params=_params("parallel", "arbitrary"),
        name="ffn",
    )(x, mods, mods, mods, wg, wu, wd, lg, lb)


def _inproj_kernel(x_ref, sh_ref, sc_ref, wqkv, wz, wq, wm, wsm, wkvt,
                   oqkv, oz, oq, om, osm, okc, oks, okw):
    h = (x_ref[...] * (1.0 + sc_ref[...]) + sh_ref[...]).astype(BF16)
    oqkv[...] = _dot(h, wqkv[...])
    oz[...] = _dot(h, wz[...]).astype(BF16)
    oq[...] = (_dot(h, wq[...]) * (NSA_HEAD_DIM ** -0.5)).astype(BF16)
    om[...] = _sig(_dot(h, wm[...])).astype(BF16)
    osm[...] = _dot(h, wsm[...])
    kvt = _dot_nt(wkvt[...], h)
    okc[...] = kvt[0:KV2]
    oks[...] = kvt[KV2:2 * KV2]
    okw[...] = kvt[2 * KV2:3 * KV2]


def _inproj(x, mods, kind, rows_per_seq, ws, wkvt, tm):
    n, d = x.shape
    tps = max(rows_per_seq // tm, 1)
    widths = [w.shape[1] for w in ws]
    dtypes = [F32, BF16, BF16, BF16, F32]
    if kind == "seq":
        kv_spec = pl.BlockSpec((None, KV2, tm), lambda i: (i // tps, 0, i % tps))
        kv_shape = jax.ShapeDtypeStruct((n // rows_per_seq, KV2, rows_per_seq), F32)
    else:
        kv_spec = pl.BlockSpec((KV2, tm), lambda i: (0, i))
        kv_shape = jax.ShapeDtypeStruct((KV2, n), F32)
    return pl.pallas_call(
        _inproj_kernel,
        grid=(n // tm,),
        in_specs=[pl.BlockSpec((tm, d), lambda i: (i, 0)),
                  _mod_spec(kind, tm, d, tps, 3), _mod_spec(kind, tm, d, tps, 4)]
                 + [pl.BlockSpec((d, wd), lambda i: (0, 0)) for wd in widths]
                 + [pl.BlockSpec(wkvt.shape, lambda i: (0, 0))],
        out_specs=[pl.BlockSpec((tm, wd), lambda i: (i, 0)) for wd in widths] + [kv_spec] * 3,
        out_shape=[jax.ShapeDtypeStruct((n, wd), dt) for wd, dt in zip(widths, dtypes)] + [kv_shape] * 3,
        compiler_params=_params("parallel"),
        name="inproj",
    )(x, mods, mods, *ws, wkvt)


def _gdn_kernel(qkv_ref, z_ref, sm_ref, abt_ref, cb_ref, cw_ref, prow_ref, alog_ref, dtb_ref, nw_ref,
                s0_ref, tri_ref, triu_ref, bd_ref, expg_ref, expb_ref, o_ref, s_ref, xbuf,
                *, chunk, l_valid, l_pad):
    C = chunk
    HD = GDN_HEAD_DIM
    W = GDN_WIDTH
    c = pl.program_id(1)

    @pl.when(c == 0)
    def _():
        xbuf[0:SUBLANES, :] = cb_ref[...]
        s_ref[...] = s0_ref[...]

    x = qkv_ref[...]
    xbuf[SUBLANES:SUBLANES + C, :] = x
    cw = cw_ref[...]
    y = (xbuf[5:5 + C, :] * cw[0:1] + xbuf[6:6 + C, :] * cw[1:2]
         + xbuf[7:7 + C, :] * cw[2:3] + x * cw[3:4])
    xbuf[0:SUBLANES, :] = x[C - SUBLANES:C, :]
    act = y * _sig(y)
    q = act[:, 0:W]
    k = act[:, W:2 * W]
    v = act[:, 2 * W:3 * W]

    sq = jnp.concatenate([q * q, k * k], 0)
    bd = bd_ref[...]
    hw = bd.shape[0]
    ss = jnp.concatenate([_dot_split2(sq[:, j:j + hw], bd) for j in range(0, W, hw)], 1)
    qn = q * lax.rsqrt(ss[:C] + 1e-6) * (HD ** -0.5)
    kn = k * lax.rsqrt(ss[C:] + 1e-6)

    sm = sm_ref[...]
    pr = prow_ref[...]
    gcol = -jnp.exp(pr[0:1]) * _softplus(sm + pr[1:2])
    bcol = _sig(sm)
    ab = abt_ref[...]
    grow = -jnp.exp(alog_ref[...]) * _softplus(ab[0:GDN_HEADS] + dtb_ref[...])
    if l_pad != l_valid:
        vcol = (c * C + lax.broadcasted_iota(jnp.int32, (C, 1), 0) < l_valid).astype(F32)
        vrow = (c * C + lax.broadcasted_iota(jnp.int32, (1, C), 1) < l_valid).astype(F32)
        qn, kn, v = qn * vcol, kn * vcol, v * vcol
        gcol, bcol, grow = gcol * vcol, bcol * vcol, grow * vrow

    g_cum = jnp.dot(tri_ref[...], gcol, precision=HIGHEST, preferred_element_type=F32)
    gx = _dot_split2(g_cum, expg_ref[...])
    bx = _dot_split2(bcol, expb_ref[...])
    g_row = jnp.dot(grow, triu_ref[...], precision=HIGHEST, preferred_element_type=F32)

    eg = jnp.exp(gx)
    g_last = gx[C - 1:C, :]
    q_dec = qn * eg
    k_dec = kn * jnp.exp(g_last - gx)
    kb = kn * bx
    vb = v * bx
    kbg = kb * eg
    e_last = jnp.exp(g_last)

    ri = lax.broadcasted_iota(jnp.int32, (C, C), 0)
    ci = lax.broadcasted_iota(jnp.int32, (C, C), 1)
    incl = ri >= ci
    strict = ri > ci
    eye = (ri == ci).astype(F32)
    sh = min(C, SUBLANES).bit_length() - 1
    diag_blk = (ri >> sh) == (ci >> sh)
    merge_masks = []
    while (1 << sh) < C:
        merge_masks.append(((ri >> (sh + 1)) == (ci >> (sh + 1))) & ((ri >> sh) == (ci >> sh) + 1))
        sh += 1

    heads = range(GDN_HEADS)
    sl = lambda h: slice(h * HD, (h + 1) * HD)
    bfl = lambda xs: [x.astype(BF16) for x in xs]

    dec = [jnp.where(incl, jnp.exp(jnp.where(incl, gx[:, h * HD:h * HD + C] - g_row[h:h + 1, :], 0.0)), 0.0)
           for h in heads]
    r = [_dot_nt(jnp.concatenate([kb[:, sl(h)], qn[:, sl(h)]], 0).astype(BF16), kn[:, sl(h)].astype(BF16))
         for h in heads]
    a_kk = [jnp.where(strict, r[h][:C] * dec[h], 0.0) for h in heads]
    a_qk = bfl([r[h][C:] * dec[h] for h in heads])

    nd = [jnp.where(diag_blk, -a_kk[h], 0.0) for h in heads]
    ndb = bfl(nd)
    p2 = [_dot(ndb[h], ndb[h]) for h in heads]
    t = [eye + nd[h] for h in heads]
    r2 = [_dot(p2[h].astype(BF16), jnp.concatenate([t[h], p2[h]], 1).astype(BF16)) for h in heads]
    t = [t[h] + r2[h][:, :C] for h in heads]
    t = [t[h] + _dot(r2[h][:, C:].astype(BF16), t[h].astype(BF16)) for h in heads]
    for mm in merge_masks:
        tb = bfl(t)
        tl = bfl([_dot(tb[h], jnp.where(mm, a_kk[h], 0.0).astype(BF16)) for h in heads])
        t = [t[h] - _dot(tl[h], tb[h]) for h in heads]
    tb = bfl(t)

    rhs = [jnp.concatenate([vb[:, sl(h)], kbg[:, sl(h)]], 1) for h in heads]
    x0 = [_dot(tb[h], rhs[h].astype(BF16)) for h in heads]
    res = []
    for h in heads:
        ah = a_kk[h].astype(BF16)
        al = (a_kk[h] - ah.astype(F32)).astype(BF16)
        xh = x0[h].astype(BF16)
        xl = (x0[h] - xh.astype(F32)).astype(BF16)
        hh = _dot(ah, jnp.concatenate([xh, xl], 1))
        res.append(rhs[h] - x0[h] - (hh[:, :2 * HD] + hh[:, 2 * HD:] + _dot(al, xh)))
    uw = [x0[h] + _dot(tb[h], res[h].astype(BF16)) for h in heads]

    s_old = [s_ref[h] for h in heads]
    wq = [_dot(jnp.concatenate([uw[h][:, HD:], q_dec[:, sl(h)]], 0).astype(BF16), s_old[h].astype(BF16))
          for h in heads]
    v_new = bfl([uw[h][:, :HD] - wq[h][:C] for h in heads])
    o = [wq[h][C:] + _dot(a_qk[h], v_new[h]) for h in heads]
    for h in heads:
        s_ref[h] = s_old[h] * e_last[:, sl(h)] + _dot_tn(k_dec[:, sl(h)].astype(BF16), v_new[h])

    z = z_ref[...].astype(F32)
    nw = nw_ref[...]
    outs = []
    for h in heads:
        o_h = o[h] * lax.rsqrt(jnp.mean(o[h] * o[h], -1, keepdims=True) + 1e-6)
        zh = z[:, sl(h)]
        outs.append(o_h * nw * (zh * _sig(zh)))
    o_ref[...] = jnp.concatenate(outs, 1).astype(BF16)


def _gdn(qkv, z, small, conv_buf, s0, conv_w, a_log, dt_bias, norm_w, l_valid, chunk):
    b, lp, w3 = qkv.shape
    C = chunk
    nc = lp // C
    H, HD, W = GDN_HEADS, GDN_HEAD_DIM, GDN_WIDTH
    abt = small[:, :, :2 * H].reshape(b, nc, C, 2 * H).transpose(0, 1, 3, 2)
    cb = jnp.pad(conv_buf, ((0, 0), (SUBLANES - (GDN_CONV - 1), 0), (0, 0)))
    prow = jnp.zeros((2, LANES), F32).at[0, :H].set(a_log).at[1, :H].set(dt_bias)
    alog_r = jnp.broadcast_to(a_log[:, None], (H, C))
    dtb_r = jnp.broadcast_to(dt_bias[:, None], (H, C))
    ix = jnp.arange(C)
    tri = (ix[:, None] >= ix[None, :]).astype(F32)
    triu = tri.T
    hid = jnp.arange(W) // HD
    hw = 2 * LANES
    bd = (hid[:hw, None] == hid[None, :hw]).astype(BF16)
    lane = jnp.arange(LANES)
    expg = (lane[:, None] == hid[None, :]).astype(BF16)
    expb = (lane[:, None] == hid[None, :] + H).astype(BF16)
    const = lambda shape: pl.BlockSpec(shape, lambda i, j: (0,) * len(shape))
    o, s_fin = pl.pallas_call(
        functools.partial(_gdn_kernel, chunk=C, l_valid=l_valid, l_pad=lp),
        grid=(b, nc),
        in_specs=[pl.BlockSpec((None, C, w3), lambda i, j: (i, j, 0)),
                  pl.BlockSpec((None, C, W), lambda i, j: (i, j, 0)),
                  pl.BlockSpec((None, C, LANES), lambda i, j: (i, j, 0)),
                  pl.BlockSpec((None, None, 2 * H, C), lambda i, j: (i, j, 0, 0)),
                  pl.BlockSpec((None, SUBLANES, w3), lambda i, j: (i, 0, 0)),
                  const((GDN_CONV, w3)), const((2, LANES)), const((H, C)), const((H, C)), const((1, HD)),
                  pl.BlockSpec((None, H, HD, HD), lambda i, j: (i, 0, 0, 0)),
                  const((C, C)), const((C, C)), const((hw, hw)), const((LANES, W)), const((LANES, W))],
        out_specs=[pl.BlockSpec((None, C, W), lambda i, j: (i, j, 0)),
                   pl.BlockSpec((None, H, HD, HD), lambda i, j: (i, 0, 0, 0))],
        out_shape=[jax.ShapeDtypeStruct((b, lp, W), BF16), jax.ShapeDtypeStruct((b, H, HD, HD), F32)],
        scratch_shapes=[pltpu.VMEM((SUBLANES + C, w3), F32)],
        compiler_params=_params("parallel", "arbitrary"),
        name="gdn",
    )(qkv, z, small, abt, cb, conv_w, prow, alog_r, dtb_r, norm_w.reshape(1, HD), s0,
      tri, triu, bd, expg, expb)
    return o, s_fin


def _pool_tile(x, w, pm):
    t = x.shape[1]
    wt = jnp.concatenate([w] * (t // LANES), 1) if t > LANES else w
    return jnp.dot(x * wt, pm, precision=HIGHEST, preferred_element_type=F32)


def _pool_kernel(x_ref, w_ref, pm_ref, o_ref):
    o_ref[...] = _pool_tile(x_ref[...], w_ref[...], pm_ref[...])


def _pool_matrix(t):
    nb = -(-(t // CMP_BLOCK) // LANES) * LANES
    return (jnp.arange(t)[:, None] // CMP_BLOCK == jnp.arange(nb)[None, :]).astype(F32)


def _pool(kvt, wrow):
    b, rows, t = kvt.shape
    pm = _pool_matrix(t)
    return pl.pallas_call(
        _pool_kernel,
        grid=(b,),
        in_specs=[pl.BlockSpec((None, rows, t), lambda i: (i, 0, 0)),
                  pl.BlockSpec(wrow.shape, lambda i: (0, 0)),
                  pl.BlockSpec(pm.shape, lambda i: (0, 0))],
        out_specs=pl.BlockSpec((None, rows, pm.shape[1]), lambda i: (i, 0, 0)),
        out_shape=jax.ShapeDtypeStruct((b, rows, pm.shape[1]), F32),
        compiler_params=_params("parallel"),
        name="pool",
    )(kvt, wrow, pm)


def _pool_paged_kernel(pt_ref, *refs, pages):
    del pt_ref
    w_ref, pm_ref, o_ref = refs[pages:]
    x = jnp.concatenate([refs[p][...] for p in range(pages)], 1)
    o_ref[...] = _pool_tile(x, w_ref[...], pm_ref[...])


def _pool_paged(cache_t, page_table, wrow, pages):
    _, rows, ps = cache_t.shape
    b, n_pages = page_table.shape
    pm = _pool_matrix(pages * ps)
    assert pm.shape[1] == pages * ps // CMP_BLOCK
    page_spec = lambda p: pl.BlockSpec((None, rows, ps), lambda i, j, pt: (pt[i, j * pages + p], 0, 0))
    return pl.pallas_call(
        functools.partial(_pool_paged_kernel, pages=pages),
        grid_spec=pltpu.PrefetchScalarGridSpec(
            num_scalar_prefetch=1,
            grid=(b, n_pages // pages),
            in_specs=[page_spec(p) for p in range(pages)]
                     + [pl.BlockSpec(wrow.shape, lambda i, j, pt: (0, 0)),
                        pl.BlockSpec(pm.shape, lambda i, j, pt: (0, 0))],
            out_specs=pl.BlockSpec((None, rows, pm.shape[1]), lambda i, j, pt: (i, 0, j))),
        out_shape=jax.ShapeDtypeStruct((b, rows, n_pages * ps // CMP_BLOCK), F32),
        compiler_params=_params("parallel", "arbitrary"),
        name="pool_paged",
    )(page_table, *([cache_t] * pages), wrow, pm)


def _cmp_probs(sc, pos, tq):
    n = sc.shape[1]
    lane = lax.broadcasted_iota(jnp.int32, (1, n), 1)
    maskc = ((lane + 1) * CMP_BLOCK - 1) <= pos
    ps = []
    imp = None
    for g in range(NSA_GROUP):
        s = jnp.where(maskc, sc[g * tq:(g + 1) * tq], NEG)
        m = jnp.max(s, -1, keepdims=True)
        e = jnp.where(maskc, jnp.exp(s - m), 0.0)
        den = jnp.sum(e, -1, keepdims=True)
        p = e / jnp.where(den > 0.0, den, 1.0)
        ps.append(p)
        imp = p if imp is None else imp + p
    return jnp.concatenate(ps, 0), imp


def _select_blocks(imp, pos, axis):
    n = imp.shape[axis]
    idx = lax.broadcasted_iota(jnp.int32, (n, 1) if axis == 0 else (1, n), axis)
    even = (idx & 1) == 0
    imp2 = imp + jnp.where(even, pltpu.roll(imp, n - 1, axis), pltpu.roll(imp, 1, axis))
    blk = idx >> 1
    valid = blk * SEL_BLOCK <= pos
    cur = pos >> 6
    forced = (blk == 0) | (blk == cur) | (blk == cur - 1)
    score = jnp.where(valid, jnp.where(forced, BIG, imp2), -1.0)
    work = jnp.where(even, score, -2.0)
    idxf = idx.astype(F32)
    sel = jnp.zeros(work.shape, F32)
    for _ in range(SEL_TOPK):
        m = jnp.max(work, axis, keepdims=True)
        first = jnp.min(jnp.where(work == m, idxf, 1e9), axis, keepdims=True)
        pick = idxf == first
        sel = jnp.where(pick, 1.0, sel)
        work = jnp.where(pick, -2.0, work)
    sel = jnp.where(score >= 0.0, sel, 0.0)
    return sel + pltpu.roll(sel, 1, axis)


def _online_update(carry, s, mf, vt):
    m, l, acc = carry
    s = jnp.where(mf > 0.5, s, NEG)
    m_new = jnp.maximum(m, jnp.max(s, -1, keepdims=True))
    alpha = jnp.exp(m - m_new)
    p = jnp.exp(s - m_new) * mf
    l = alpha * l + jnp.sum(p, -1, keepdims=True)
    acc = alpha * acc + _dot_nt(p.astype(BF16), vt)
    return m_new, l, acc


def _expand_matrix(n_blocks, n_keys):
    return (jnp.arange(n_keys)[None, :] // CMP_BLOCK == jnp.arange(n_blocks)[:, None]).astype(BF16)


def _nsa_prompt_kernel(q_ref, kvb_ref, kvs_ref, kvw_ref, sm_ref, e_ref, o_ref, *, n_rows):
    TQ, TK, G, HD, KV = NSA_TQ, NSA_TK, NSA_GROUP, NSA_HEAD_DIM, NSA_KV_HEADS
    i = pl.program_id(1)
    t0 = i * TQ
    pos = t0 + lax.broadcasted_iota(jnp.int32, (TQ, 1), 0)
    pos_row = t0 + lax.broadcasted_iota(jnp.int32, (1, TQ), 1)
    gs = _sig(sm_ref[...])
    gate_lane = lambda br, h, g: 2 * GDN_HEADS + br * NSA_Q_HEADS + h * G + g

    n_wt = WINDOW // TQ + 1
    w_off = [pl.multiple_of(jnp.maximum(i - (n_wt - 1) + j, 0) * TQ, TQ) for j in range(n_wt)]
    cidx = lax.broadcasted_iota(jnp.int32, (1, WINDOW + TQ), 1)
    diff = lax.broadcasted_iota(jnp.int32, (TQ, 1), 0) + WINDOW - cidx
    bw = jnp.where((diff >= 0) & (diff < WINDOW) & (t0 - WINDOW + cidx >= 0), 0.0, NEG)
    kidx = lax.broadcasted_iota(jnp.int32, (1, TK), 1)
    n_used = (t0 + TQ + TK - 1) // TK

    krow = lambda h: slice(h * HD, (h + 1) * HD)
    vrow = lambda h: slice((KV + h) * HD, (KV + h + 1) * HD)
    qg, oc4, ow, selt = [], [], [], []
    for h in range(KV):
        q = q_ref[:, h * G * HD:(h + 1) * G * HD]
        qg.append([q[:, g * HD:(g + 1) * HD] for g in range(G)])

        pc, imp = _cmp_probs(_dot(jnp.concatenate(qg[h], 0), kvb_ref[krow(h), :].astype(BF16)), pos, TQ)
        oc4.append(_dot_nt(pc.astype(BF16), kvb_ref[vrow(h), :].astype(BF16)))

        kw = jnp.concatenate([kvw_ref[krow(h), pl.ds(o, TQ)] for o in w_off], 1).astype(BF16)
        vw = jnp.concatenate([kvw_ref[vrow(h), pl.ds(o, TQ)] for o in w_off], 1).astype(BF16)
        vw = jnp.concatenate([vw, jnp.ones_like(vw)], 0)
        ow.append([])
        for g in range(G):
            s = _dot(qg[h][g], kw) + bw
            e = jnp.exp(s - jnp.max(s, -1, keepdims=True))
            r = _dot_nt(e.astype(BF16), vw)
            ow[h].append(r[:, :HD] / r[:, HD:HD + 1])

        selt.append(_select_blocks(imp.T[:n_rows], pos_row, 0).astype(BF16))

    def body(kt, carry):
        off = pl.multiple_of(kt * TK, TK)
        e_t = e_ref[:, pl.ds(off, TK)]
        causal = off + kidx <= pos
        out = []
        for h in range(KV):
            bias = jnp.where((_dot_tn(selt[h], e_t) > 0.5) & causal, 0.0, NEG)
            k_t = kvs_ref[krow(h), pl.ds(off, TK)].astype(BF16)
            v_t = kvs_ref[vrow(h), pl.ds(off, TK)].astype(BF16)
            v_t = jnp.concatenate([v_t, jnp.ones_like(v_t)], 0)
            for g in range(G):
                m, acc = carry[h * G + g]
                s = _dot(qg[h][g], k_t) + bias
                m_new = jnp.maximum(m, jnp.max(s, -1, keepdims=True))
                p = jnp.exp(s - m_new)
                out.append((m_new, jnp.exp(m - m_new) * acc + _dot_nt(p.astype(BF16), v_t)))
        return tuple(out)

    init = tuple((jnp.full((TQ, 1), NEG, F32), jnp.zeros((TQ, 2 * HD), F32)) for _ in range(KV * G))
    res = lax.fori_loop(0, n_used, body, init)

    outs = []
    for h in range(KV):
        for g in range(G):
            acc = res[h * G + g][1]
            gate = lambda br: gs[:, gate_lane(br, h, g):gate_lane(br, h, g) + 1]
            outs.append(gate(0) * oc4[h][g * TQ:(g + 1) * TQ] + gate(1) * (acc[:, :HD] / acc[:, HD:HD + 1])
                        + gate(2) * ow[h][g])
    o_ref[...] = jnp.concatenate(outs, 1).astype(BF16)


def _nsa_prompt(qn, kvb_t, kvs_t, kvw_t, small):
    b, l, _ = qn.shape
    TQ, TK = NSA_TQ, NSA_TK
    n_cmp = l // CMP_BLOCK
    assert kvb_t.shape[2] == LANES and n_cmp <= LANES and l % TK == 0 and WINDOW % TQ == 0
    n_rows = -(-n_cmp // SUBLANES) * SUBLANES
    emat = _expand_matrix(n_rows, l)
    slab = lambda t: pl.BlockSpec((None,) + t.shape[1:], lambda bi, i: (bi, 0, 0))
    return pl.pallas_call(
        functools.partial(_nsa_prompt_kernel, n_rows=n_rows),
        grid=(b, l // TQ),
        in_specs=[pl.BlockSpec((None, TQ, NSA_WIDTH), lambda bi, i: (bi, i, 0)),
                  slab(kvb_t), slab(kvs_t), slab(kvw_t),
                  pl.BlockSpec((None, TQ, LANES), lambda bi, i: (bi, i, 0)),
                  pl.BlockSpec((n_rows, l), lambda bi, i: (0, 0))],
        out_specs=pl.BlockSpec((None, TQ, NSA_WIDTH), lambda bi, i: (bi, i, 0)),
        out_shape=jax.ShapeDtypeStruct((b, l, NSA_WIDTH), BF16),
        compiler_params=_params("parallel", "arbitrary"),
        name="nsa_prompt",
    )(qn, kvb_t, kvs_t, kvw_t, small, emat)


LS = SUBLANES
ROWS_H = NSA_GROUP * LS


def _row_token(rows):
    return lax.broadcasted_iota(jnp.int32, (rows, 1), 0) & (LS - 1)


def _nsa_sample_a_kernel(q_ref, kvb_ref, cw_ref, new_ref, oc_ref, ow_ref, sel_ref, *, past, l_new):
    HD, KV = NSA_HEAD_DIM, NSA_KV_HEADS
    pos = past + lax.broadcasted_iota(jnp.int32, (LS, 1), 0)
    wb = cw_ref.shape[1]
    tok = _row_token(ROWS_H)
    c1 = lax.broadcasted_iota(jnp.int32, (1, wb), 1)
    d1 = tok + wb - c1
    m1 = jnp.where((d1 >= 0) & (d1 < WINDOW) & (past - wb + c1 >= 0), 1.0, 0.0)
    c2 = lax.broadcasted_iota(jnp.int32, (1, new_ref.shape[1]), 1)
    d2 = tok - c2
    m2 = jnp.where((d2 >= 0) & (d2 < WINDOW) & (c2 < l_new), 1.0, 0.0)
    imps = []
    for h in range(KV):
        krow = slice(h * HD, (h + 1) * HD)
        vrow = slice((KV + h) * HD, (KV + h + 1) * HD)
        qh = q_ref[h]
        p, imp = _cmp_probs(_dot(qh, kvb_ref[krow, :].astype(BF16)), pos, LS)
        imps.append(imp)
        oc_ref[h] = _dot_nt(p.astype(BF16), kvb_ref[vrow, :].astype(BF16))

        s1 = jnp.where(m1 > 0.5, _dot(qh, cw_ref[krow, :].astype(BF16)), NEG)
        s2 = jnp.where(m2 > 0.5, _dot(qh, new_ref[krow, :].astype(BF16)), NEG)
        m = jnp.maximum(jnp.max(s1, -1, keepdims=True), jnp.max(s2, -1, keepdims=True))
        e1 = jnp.exp(s1 - m) * m1
        e2 = jnp.exp(s2 - m) * m2
        den = jnp.sum(e1, -1, keepdims=True) + jnp.sum(e2, -1, keepdims=True)
        den = jnp.where(den > 0.0, den, 1.0)
        ow_ref[h] = (_dot_nt((e1 / den).astype(BF16), cw_ref[vrow, :].astype(BF16))
                     + _dot_nt((e2 / den).astype(BF16), new_ref[vrow, :].astype(BF16)))
    sel = _select_blocks(jnp.concatenate(imps, 0), jnp.concatenate([pos] * KV, 0), 1)
    for h in range(KV):
        sel_ref[h] = sel[h * LS:(h + 1) * LS]


def _nsa_sample_a(q_rows, kvb_t, cache_win_t, kvw_new_t, past, l_new):
    b, _, ncp = kvb_t.shape
    per_seq = lambda t: pl.BlockSpec((None,) + t.shape[1:], lambda i: (i,) + (0,) * (t.ndim - 1))
    out_rows = jax.ShapeDtypeStruct(q_rows.shape, F32)
    sel_shape = jax.ShapeDtypeStruct((b, NSA_KV_HEADS, LS, ncp), F32)
    return pl.pallas_call(
        functools.partial(_nsa_sample_a_kernel, past=past, l_new=l_new),
        grid=(b,),
        in_specs=[per_seq(q_rows), per_seq(kvb_t), per_seq(cache_win_t), per_seq(kvw_new_t)],
        out_specs=[per_seq(out_rows), per_seq(out_rows), per_seq(sel_shape)],
        out_shape=[out_rows, out_rows, sel_shape],
        compiler_params=_params("parallel"),
        name="nsa_sample_a",
    )(q_rows, kvb_t, cache_win_t, kvw_new_t)


def _nsa_sample_sel_kernel(pt_ref, *refs, pages, l_new):
    del pt_ref
    page_refs = refs[:pages]
    q_ref, sel_ref, tail_ref, new_ref, e_ref, oc_ref, ow_ref, gate_ref, o_ref, m_scr, l_scr, acc_scr = refs[pages:]
    HD, KV, G = NSA_HEAD_DIM, NSA_KV_HEADS, NSA_GROUP
    i = pl.program_id(1)
    last = i == pl.num_programs(1) - 1

    @pl.when(i == 0)
    def _():
        m_scr[...] = jnp.full_like(m_scr, NEG)
        l_scr[...] = jnp.zeros_like(l_scr)
        acc_scr[...] = jnp.zeros_like(acc_scr)

    kv = jnp.concatenate([r[...] for r in page_refs], 1)
    emat = e_ref[...]
    for h in range(KV):
        krow = slice(h * HD, (h + 1) * HD)
        vrow = slice((KV + h) * HD, (KV + h + 1) * HD)
        mh = _dot(sel_ref[h].astype(BF16), emat)
        carry = _online_update((m_scr[h], l_scr[h], acc_scr[h]),
                               _dot(q_ref[h], kv[krow].astype(BF16)),
                               jnp.concatenate([mh] * G, 0), kv[vrow].astype(BF16))
        m_scr[h], l_scr[h], acc_scr[h] = carry

    @pl.when(last)
    def _():
        tok = _row_token(ROWS_H)
        c2 = lax.broadcasted_iota(jnp.int32, (1, new_ref.shape[1]), 1)
        for h in range(KV):
            krow = slice(h * HD, (h + 1) * HD)
            vrow = slice((KV + h) * HD, (KV + h + 1) * HD)
            flag = jnp.concatenate([tail_ref[h][:, 0:1]] * G, 0)
            mt = jnp.where((c2 <= tok) & (c2 < l_new) & (flag > 0.5), 1.0, 0.0)
            _, l, acc = _online_update((m_scr[h], l_scr[h], acc_scr[h]),
                                       _dot(q_ref[h], new_ref[krow, :].astype(BF16)), mt,
                                       new_ref[vrow, :].astype(BF16))
            osel = acc / jnp.where(l > 0.0, l, 1.0)
            gs = _sig(gate_ref[h])
            o_ref[h] = gs[:, 0:1] * oc_ref[h] + gs[:, 1:2] * osel + gs[:, 2:3] * ow_ref[h]


def _nsa_sample_sel(q_rows, selmask, cache_sel_t, page_table, kvs_new_t, oc, ow, gate_rows, pages, l_new):
    b, n_pages = page_table.shape
    _, rows, ps = cache_sel_t.shape
    lanes_per_step = pages * ps // CMP_BLOCK
    assert lanes_per_step % LANES == 0 and n_pages % pages == 0
    n_steps = n_pages // pages
    emat = _expand_matrix(lanes_per_step, pages * ps)
    page_spec = lambda p: pl.BlockSpec((None, rows, ps), lambda bi, i, pt: (pt[bi, i * pages + p], 0, 0))
    per_seq = lambda t: pl.BlockSpec((None,) + t.shape[1:], lambda bi, i, pt: (bi,) + (0,) * (t.ndim - 1))
    tail_block = n_steps * lanes_per_step // LANES
    kvh = NSA_KV_HEADS
    return pl.pallas_call(
        functools.partial(_nsa_sample_sel_kernel, pages=pages, l_new=l_new),
        grid_spec=pltpu.PrefetchScalarGridSpec(
            num_scalar_prefetch=1,
            grid=(b, n_steps),
            in_specs=[page_spec(p) for p in range(pages)] + [
                per_seq(q_rows),
                pl.BlockSpec((None, kvh, LS, lanes_per_step), lambda bi, i, pt: (bi, 0, 0, i)),
                pl.BlockSpec((None, kvh, LS, LANES), lambda bi, i, pt: (bi, 0, 0, tail_block)),
                per_seq(kvs_new_t),
                pl.BlockSpec(emat.shape, lambda bi, i, pt: (0, 0)),
                per_seq(oc), per_seq(ow), per_seq(gate_rows)],
            out_specs=per_seq(oc),
            scratch_shapes=[pltpu.VMEM((kvh, ROWS_H, 1), F32), pltpu.VMEM((kvh, ROWS_H, 1), F32),
                            pltpu.VMEM((kvh, ROWS_H, NSA_HEAD_DIM), F32)]),
        out_shape=jax.ShapeDtypeStruct(oc.shape, F32),
        compiler_params=_params("parallel", "arbitrary"),
        name="nsa_sample_sel",
    )(page_table, *([cache_sel_t] * pages), q_rows, selmask, selmask, kvs_new_t, emat, oc, ow, gate_rows)


def _nsa_sample(qn, kvc_t, kvs_t, kvw_t, gate, cache_cmp_t, cache_sel_t, cache_win_t, page_table, wrow):
    b, l, _ = qn.shape
    KV, G, HD = NSA_KV_HEADS, NSA_GROUP, NSA_HEAD_DIM
    n_pages = page_table.shape[1]
    ps = cache_cmp_t.shape[2]
    past = n_pages * ps
    assert l <= LS and ps == PAGE_SIZE
    pages = min(PAGES_PER_STEP, n_pages)
    pad_new = lambda t: jnp.pad(t, ((0, 0), (0, 0), (0, ps - l)))
    kvb_t = jnp.concatenate([_pool_paged(cache_cmp_t, page_table, wrow, pages), _pool(pad_new(kvc_t), wrow)], 2)
    q_rows = jnp.pad(qn.reshape(b, l, KV, G, HD).transpose(0, 2, 3, 1, 4),
                     ((0, 0),) * 3 + ((0, LS - l), (0, 0))).reshape(b, KV, ROWS_H, HD)
    oc, ow, selmask = _nsa_sample_a(q_rows, kvb_t, cache_win_t, pad_new(kvw_t), past, l)
    gate_rows = jnp.pad(gate.reshape(b, l, 3, KV, G).transpose(0, 3, 4, 1, 2),
                        ((0, 0),) * 3 + ((0, LS - l), (0, 0))).reshape(b, KV, ROWS_H, 3)
    o = _nsa_sample_sel(q_rows, selmask, cache_sel_t, page_table, pad_new(kvs_t), oc, ow, gate_rows, pages, l)
    o = o.reshape(b, KV, G, LS, HD)[:, :, :, :l]
    return o.transpose(0, 3, 1, 2, 4).reshape(b, l, NSA_WIDTH).astype(BF16)


def _merge_kernel(x_ref, gt_ref, og_ref, on_ref, m_ref, wbg_ref, wbn_ref, wo_ref, lg_ref, lb_ref, o_ref,
                  *, alpha):
    d = x_ref.shape[1]
    m = m_ref[...].astype(F32)
    mix = m[:, :d] * _dot(og_ref[...], wbg_ref[...]) + m[:, d:] * _dot(on_ref[...], wbn_ref[...])
    y = _dot(mix.astype(BF16), wo_ref[...])
    o_ref[...] = _layer_norm(alpha * x_ref[...] + gt_ref[...] * y, lg_ref[...], lb_ref[...])


def _merge(x, mods, kind, rows_per_seq, o_gdn, o_nsa, msig, wbg, wbn, wo, lg, lb, alpha, tm):
    n, d = x.shape
    tps = max(rows_per_seq // tm, 1)
    row = lambda wd: pl.BlockSpec((tm, wd), lambda i: (i, 0))
    full = lambda a: pl.BlockSpec(a.shape, lambda i: (0, 0))
    return pl.pallas_call(
        functools.partial(_merge_kernel, alpha=alpha),
        grid=(n // tm,),
        in_specs=[row(d), _mod_spec(kind, tm, d, tps, 5), row(o_gdn.shape[1]), row(o_nsa.shape[1]),
                  row(2 * d), full(wbg), full(wbn), full(wo), full(lg), full(lb)],
        out_specs=row(d),
        out_shape=jax.ShapeDtypeStruct((n, d), F32),
        compiler_params=_params("parallel"),
        name="merge",
    )(x, mods, o_gdn, o_nsa, msig, wbg, wbn, wo, lg, lb)


def _row_tile(n, pref):
    t = min(pref, n)
    while n % t:
        t //= 2
    return t


def _ff_tile(dff):
    for parts in (4, 2, 1, 11, 22):
        if dff % parts == 0 and (dff // parts) % LANES == 0:
            return dff // parts
    return dff


def _to_rows(kv_t):
    b, _, t = kv_t.shape
    return kv_t.reshape(b, 2, NSA_KV_HEADS, NSA_HEAD_DIM, t).transpose(0, 4, 1, 2, 3)


def _to_feature_major(kv_rows):
    n, t = kv_rows.shape[:2]
    return kv_rows.transpose(0, 2, 3, 4, 1).reshape(n, KV2, t)


def _layer(x, mod, kind, past, lw, alpha):
    b, l, d = x.shape
    n = b * l
    (wg1, wu1, wd1, wg2, wu2, wd2, w_in_parts, wkvt, conv_w, a_log, dt_bias, norm_w, wrow,
     wbg, wbn, wo, ln_g, ln_b) = lw
    tm = _row_tile(l if kind == "seq" else n, 512)
    tm_in = _row_tile(l if kind == "seq" else n, 256)
    tf = _ff_tile(wg1.shape[1])
    tm_ff = _row_tile(l if kind == "seq" else n, 1024)
    lg = lambda i: ln_g[i].reshape(1, d)
    lb = lambda i: ln_b[i].reshape(1, d)

    x1 = _ffn(x.reshape(n, d), mod, kind, l, 0, wg1, wu1, wd1, lg(0), lb(0), alpha, tm_ff, tf)

    qkv, z, qn, msig, small, kvc_t, kvs_t, kvw_t = _inproj(x1, mod, kind, l, w_in_parts, wkvt, tm_in)
    seq = lambda t: t.reshape(b, l, t.shape[-1])
    qkv, z, qn, small = [seq(t) for t in (qkv, z, qn, small)]
    if kind == "tok":
        kvc_t, kvs_t, kvw_t = [t.reshape(KV2, b, l).transpose(1, 0, 2) for t in (kvc_t, kvs_t, kvw_t)]

    if past is None:
        conv_buf = jnp.zeros((b, GDN_CONV - 1, 3 * GDN_WIDTH), F32)
        s0 = jnp.zeros((b, GDN_HEADS, GDN_HEAD_DIM, GDN_HEAD_DIM), F32)
        o_nsa = _nsa_prompt(qn, _pool(kvc_t, wrow), kvs_t, kvw_t, small)
        win_t = kvw_t[:, :, l - min(WINDOW, l):]
    else:
        s0, conv_buf, cache_cmp, cache_sel, cache_win, page_table = past
        cache_win_t = _to_feature_major(cache_win)
        gate = small[:, :, 2 * GDN_HEADS:2 * GDN_HEADS + 3 * NSA_Q_HEADS]
        o_nsa = _nsa_sample(qn, kvc_t, kvs_t, kvw_t, gate, _to_feature_major(cache_cmp),
                            _to_feature_major(cache_sel), cache_win_t, page_table, wrow)
        win_t = jnp.concatenate([cache_win_t, kvw_t], 2)[:, :, l:]

    chunk = min(GDN_CHUNK, -(-l // SUBLANES) * SUBLANES)
    lp = -(-l // chunk) * chunk
    padl = lambda t: jnp.pad(t, ((0, 0), (0, lp - l), (0, 0)))
    o_gdn, s_new = _gdn(padl(qkv), padl(z), padl(small), conv_buf, s0, conv_w, a_log, dt_bias, norm_w, l, chunk)
    conv_new = jnp.concatenate([conv_buf, qkv], 1)[:, -(GDN_CONV - 1):]

    x2 = _merge(x1, mod, kind, l, o_gdn[:, :l].reshape(n, GDN_WIDTH), o_nsa.reshape(n, NSA_WIDTH), msig,
                wbg, wbn, wo, lg(1), lb(1), alpha, tm)
    x3 = _ffn(x2, mod, kind, l, 6, wg2, wu2, wd2, lg(2), lb(2), alpha, tm_ff, tf)
    return x3.reshape(b, l, d), (s_new, conv_new, _to_rows(kvc_t), _to_rows(kvs_t), _to_rows(win_t))


def _split_w_in(w_in, d):
    splits = (3 * GDN_WIDTH, GDN_WIDTH, GDN_HEADS, GDN_HEADS, NSA_WIDTH, KV2, KV2, KV2, 3 * NSA_Q_HEADS, 2 * d)
    offs = [0]
    for s in splits:
        offs.append(offs[-1] + s)
    qkv, z, a, bb, q, kc, ks, kw, gate, merge = [w_in[:, offs[i]:offs[i + 1]] for i in range(len(splits))]
    n_small = 2 * GDN_HEADS + 3 * NSA_Q_HEADS
    small = jnp.pad(jnp.concatenate([a, bb, gate], 1), ((0, 0), (0, LANES - n_small)))
    wkvt = jnp.concatenate([kc, ks, kw], 1).T
    return [t.astype(BF16) for t in (qkv, z, q, merge, small)], wkvt.astype(BF16)


def kernel(x_prompt, x_sample, c_prompt, c_sample, state_gdn, state_gdn_conv, cache_cmp_kv, cache_sel_kv, cache_win_kv, page_table, ln_g, ln_b, w_ada, b_ada, w_ff1_gu, w_ff1_dn, w_ff2_gu, w_ff2_dn, w_in, gdn_conv_w, gdn_a_log, gdn_dt_bias, gdn_norm_w, nsa_w_cmp, w_br_gdn, w_br_nsa, w_out):
    depth = w_in.shape[0]
    alpha = (2.0 * depth) ** 0.25
    bp, lp, d = x_prompt.shape
    bs, ls, _ = x_sample.shape
    y_p, y_s = x_prompt, x_sample
    p_st, s_st = [], []
    for l in range(depth):
        dff = w_ff1_dn.shape[1]
        bf = lambda t: t.astype(BF16)
        wrow = jnp.tile(jnp.broadcast_to(nsa_w_cmp[l].transpose(0, 2, 1)[:, :, None, :],
                                         (2, NSA_KV_HEADS, NSA_HEAD_DIM, CMP_BLOCK)).reshape(KV2, CMP_BLOCK),
                        (1, LANES // CMP_BLOCK))
        w_in_parts, wkvt = _split_w_in(w_in[l], d)
        lw = (bf(w_ff1_gu[l][:, :dff]), bf(w_ff1_gu[l][:, dff:]), bf(w_ff1_dn[l]),
              bf(w_ff2_gu[l][:, :dff]), bf(w_ff2_gu[l][:, dff:]), bf(w_ff2_dn[l]),
              w_in_parts, wkvt, gdn_conv_w[l], gdn_a_log[l], gdn_dt_bias[l], gdn_norm_w[l], wrow,
              bf(w_br_gdn[l]), bf(w_br_nsa[l]), bf(w_out[l]), ln_g[l], ln_b[l])
        c_all = jnp.concatenate([c_prompt, c_sample], 0)
        r = c_all.shape[0]
        rp = -(-r // SUBLANES) * SUBLANES
        mod = _ada(jnp.pad(c_all, ((0, rp - r), (0, 0))), w_ada[l], b_ada[l])
        mod_p = mod[:bp].reshape(bp * 9, 1, d)
        mod_s = jnp.repeat(mod[bp:bp + bs], ls, axis=0)
        y_p, st_p = _layer(y_p, mod_p, "seq", None, lw, alpha)
        past = (state_gdn[l], state_gdn_conv[l], cache_cmp_kv[l], cache_sel_kv[l], cache_win_kv[l], page_table)
        y_s, st_s = _layer(y_s, mod_s, "tok", past, lw, alpha)
        p_st.append(st_p)
        s_st.append(st_s)
    p_out = [jnp.stack(t) for t in zip(*p_st)]
    s_out = [jnp.stack(t) for t in zip(*s_st)]
    return (y_p, y_s, *p_out, *s_out)
```

```python
import functools

import jax
import jax.numpy as jnp
from jax import lax
from jax.experimental import pallas as pl
from jax.experimental.pallas import tpu as pltpu

F32 = jnp.float32
BF16 = jnp.bfloat16
HIGHEST = lax.Precision.HIGHEST

GDN_HEADS = 8
GDN_HEAD_DIM = 64
GDN_WIDTH = GDN_HEADS * GDN_HEAD_DIM
GDN_CONV = 4
GDN_CHUNK = 64
NSA_Q_HEADS = 8
NSA_KV_HEADS = 2
NSA_HEAD_DIM = 64
NSA_GROUP = NSA_Q_HEADS // NSA_KV_HEADS
NSA_WIDTH = NSA_Q_HEADS * NSA_HEAD_DIM
KV_WIDTH = NSA_KV_HEADS * NSA_HEAD_DIM
KV2 = 2 * KV_WIDTH
CMP_BLOCK = 32
SEL_BLOCK = 64
SEL_TOPK = 16
WINDOW = 512
PAGE_SIZE = 128
NEG = -1e30
BIG = 1e4
LN_EPS = 1e-5

SUBLANES = 8
LANES = 128
VMEM_LIMIT = 52 * 1024 * 1024

NSA_TQ = 128
NSA_TK = 256
PAGES_PER_STEP = 32


def _sig(x):
    return 1.0 / (1.0 + jnp.exp(-x))


def _softplus(x):
    return jnp.maximum(x, 0.0) + jnp.log(1.0 + jnp.exp(-jnp.abs(x)))


def _layer_norm(r, g, b):
    mu = jnp.mean(r, -1, keepdims=True)
    d = r - mu
    var = jnp.mean(d * d, -1, keepdims=True)
    return d * lax.rsqrt(var + LN_EPS) * g + b


def _dot(a, b):
    return jnp.dot(a, b, preferred_element_type=F32)


def _dot_nt(a, b):
    return lax.dot_general(a, b, (((1,), (1,)), ((), ())), preferred_element_type=F32)


def _dot_tn(a, b):
    return lax.dot_general(a, b, (((0,), (0,)), ((), ())), preferred_element_type=F32)


def _dot_split2(a, sel):
    hi = a.astype(BF16)
    lo = (a - hi.astype(F32)).astype(BF16)
    return _dot(hi, sel) + _dot(lo, sel)


def _params(*sem):
    return pltpu.CompilerParams(dimension_semantics=sem, vmem_limit_bytes=VMEM_LIMIT)


def _mod_spec(kind, tm, d, tiles_per_seq, k):
    if kind == "seq":
        return pl.BlockSpec((None, 1, d), lambda i, *_: ((i // tiles_per_seq) * 9 + k, 0, 0))
    return pl.BlockSpec((tm, d), lambda i, *_: (i, k))


def _ada_kernel(c_ref, w_ref, b_ref, o_ref):
    c = c_ref[...]
    h = (c * _sig(c)).astype(BF16)
    o_ref[...] = _dot(h, w_ref[...].astype(BF16)) + b_ref[...]


def _ada(c, w, b):
    r, d = c.shape
    n = w.shape[1]
    tn = d
    return pl.pallas_call(
        _ada_kernel,
        grid=(n // tn,),
        in_specs=[pl.BlockSpec((r, d), lambda j: (0, 0)),
                  pl.BlockSpec((d, tn), lambda j: (0, j)),
                  pl.BlockSpec((1, tn), lambda j: (0, j))],
        out_specs=pl.BlockSpec((r, tn), lambda j: (0, j)),
        out_shape=jax.ShapeDtypeStruct((r, n), F32),
        compiler_params=_params("arbitrary"),
        name="ada",
    )(c, w, b.reshape(1, n))


def _ffn_kernel(x_ref, sh_ref, sc_ref, gt_ref, wg_ref, wu_ref, wd_ref, lg_ref, lb_ref, o_ref,
                h_scr, acc_scr, *, alpha):
    j = pl.program_id(1)

    @pl.when(j == 0)
    def _():
        h_scr[...] = (x_ref[...] * (1.0 + sc_ref[...]) + sh_ref[...]).astype(BF16)
        acc_scr[...] = jnp.zeros_like(acc_scr)

    h = h_scr[...]
    g = _dot(h, wg_ref[...])
    u = _dot(h, wu_ref[...])
    a = (g * _sig(g) * u).astype(BF16)
    acc_scr[...] += _dot(a, wd_ref[...])

    @pl.when(j == pl.num_programs(1) - 1)
    def _():
        r = alpha * x_ref[...] + (0.5 * gt_ref[...]) * acc_scr[...]
        o_ref[...] = _layer_norm(r, lg_ref[...], lb_ref[...])


def _ffn(x, mods, kind, rows_per_seq, k0, wg, wu, wd, lg, lb, alpha, tm, tf):
    n, d = x.shape
    dff = wg.shape[1]
    tps = max(rows_per_seq // tm, 1)
    ms = lambda k: _mod_spec(kind, tm, d, tps, k)
    return pl.pallas_call(
        functools.partial(_ffn_kernel, alpha=alpha),
        grid=(n // tm, dff // tf),
        in_specs=[pl.BlockSpec((tm, d), lambda i, j: (i, 0)),
                  ms(k0), ms(k0 + 1), ms(k0 + 2),
                  pl.BlockSpec((d, tf), lambda i, j: (0, j)),
                  pl.BlockSpec((d, tf), lambda i, j: (0, j)),
                  pl.BlockSpec((tf, d), lambda i, j: (j, 0)),
                  pl.BlockSpec((1, d), lambda i, j: (0, 0)),
                  pl.BlockSpec((1, d), lambda i, j: (0, 0))],
        out_specs=pl.BlockSpec((tm, d), lambda i, j: (i, 0)),
        out_shape=jax.ShapeDtypeStruct((n, d), F32),
        scratch_shapes=[pltpu.VMEM((tm, d), BF16), pltpu.VMEM((tm, d), F32)],
        compiler_params=_params("parallel", "arbitrary"),
        name="ffn",
    )(x, mods, mods, mods, wg, wu, wd, lg, lb)


def _inproj_kernel(x_ref, sh_ref, sc_ref, wqkv, wz, wm, wsm, wk, wt,
                   oqkv, oz, om, osm, oks, okw, oqt, okct, okst, okwt):
    h = (x_ref[...] * (1.0 + sc_ref[...]) + sh_ref[...]).astype(BF16)
    oqkv[...] = _dot(h, wqkv[...])
    oz[...] = _dot(h, wz[...]).astype(BF16)
    om[...] = _sig(_dot(h, wm[...])).astype(BF16)
    osm[...] = _dot(h, wsm[...])
    kt = _dot(h, wk[...]).astype(BF16)
    oks[...] = kt[:, 0:KV_WIDTH]
    okw[...] = kt[:, KV_WIDTH:2 * KV_WIDTH]
    ft = _dot_nt(wt[...], h)
    oqt[...] = (ft[0:NSA_WIDTH] * (NSA_HEAD_DIM ** -0.5)).astype(BF16)
    okct[...] = ft[NSA_WIDTH:NSA_WIDTH + KV2]
    okst[...] = ft[NSA_WIDTH + KV2:NSA_WIDTH + 2 * KV2]
    okwt[...] = ft[NSA_WIDTH + 2 * KV2:NSA_WIDTH + 3 * KV2]


def _inproj(x, mods, kind, rows_per_seq, ws, wk, wt, tm):
    n, d = x.shape
    tps = max(rows_per_seq // tm, 1)
    widths = [w.shape[1] for w in ws] + [KV_WIDTH] * 2
    dtypes = [F32, BF16, BF16, F32, BF16, BF16]
    t_rows = [NSA_WIDTH, KV2, KV2, KV2]
    t_dtypes = [BF16, F32, F32, F32]
    if kind == "seq":
        t_specs = [pl.BlockSpec((None, r, tm), lambda i: (i // tps, 0, i % tps)) for r in t_rows]
        t_shapes = [jax.ShapeDtypeStruct((n // rows_per_seq, r, rows_per_seq), dt) for r, dt in zip(t_rows, t_dtypes)]
    else:
        t_specs = [pl.BlockSpec((r, tm), lambda i: (0, i)) for r in t_rows]
        t_shapes = [jax.ShapeDtypeStruct((r, n), dt) for r, dt in zip(t_rows, t_dtypes)]
    full = lambda w: pl.BlockSpec(w.shape, lambda i: (0, 0))
    return pl.pallas_call(
        _inproj_kernel,
        grid=(n // tm,),
        in_specs=[pl.BlockSpec((tm, d), lambda i: (i, 0)),
                  _mod_spec(kind, tm, d, tps, 3), _mod_spec(kind, tm, d, tps, 4)]
                 + [full(w) for w in ws] + [full(wk), full(wt)],
        out_specs=[pl.BlockSpec((tm, wd), lambda i: (i, 0)) for wd in widths] + t_specs,
        out_shape=[jax.ShapeDtypeStruct((n, wd), dt) for wd, dt in zip(widths, dtypes)] + t_shapes,
        compiler_params=_params("parallel"),
        name="inproj",
    )(x, mods, mods, *ws, wk, wt)


def _gdn_kernel(qkv_ref, z_ref, sm_ref, abt_ref, cb_ref, cw_ref, prow_ref, alog_ref, dtb_ref, nw_ref,
                s0_ref, tri_ref, triu_ref, bd_ref, expg_ref, expb_ref, o_ref, s_ref, xbuf,
                *, chunk, l_valid, l_pad):
    C = chunk
    HD = GDN_HEAD_DIM
    W = GDN_WIDTH
    c = pl.program_id(1)

    @pl.when(c == 0)
    def _():
        xbuf[0:SUBLANES, :] = cb_ref[...]
        s_ref[...] = s0_ref[...]

    x = qkv_ref[...]
    xbuf[SUBLANES:SUBLANES + C, :] = x
    cw = cw_ref[...]
    y = (xbuf[5:5 + C, :] * cw[0:1] + xbuf[6:6 + C, :] * cw[1:2]
         + xbuf[7:7 + C, :] * cw[2:3] + x * cw[3:4])
    xbuf[0:SUBLANES, :] = x[C - SUBLANES:C, :]
    act = y * _sig(y)
    q = act[:, 0:W]
    k = act[:, W:2 * W]
    v = act[:, 2 * W:3 * W]

    sq = jnp.concatenate([q * q, k * k], 0)
    bd = bd_ref[...]
    hw = bd.shape[0]
    ss = jnp.concatenate([_dot_split2(sq[:, j:j + hw], bd) for j in range(0, W, hw)], 1)
    qn = q * lax.rsqrt(ss[:C] + 1e-6) * (HD ** -0.5)
    kn = k * lax.rsqrt(ss[C:] + 1e-6)

    sm = sm_ref[...]
    pr = prow_ref[...]
    gcol = -jnp.exp(pr[0:1]) * _softplus(sm + pr[1:2])
    bcol = _sig(sm)
    ab = abt_ref[...]
    grow = -jnp.exp(alog_ref[...]) * _softplus(ab[0:GDN_HEADS] + dtb_ref[...])
    if l_pad != l_valid:
        vcol = (c * C + lax.broadcasted_iota(jnp.int32, (C, 1), 0) < l_valid).astype(F32)
        vrow = (c * C + lax.broadcasted_iota(jnp.int32, (1, C), 1) < l_valid).astype(F32)
        qn, kn, v = qn * vcol, kn * vcol, v * vcol
        gcol, bcol, grow = gcol * vcol, bcol * vcol, grow * vrow

    g_cum = jnp.dot(tri_ref[...], gcol, precision=HIGHEST, preferred_element_type=F32)
    gx = _dot_split2(g_cum, expg_ref[...])
    bx = _dot_split2(bcol, expb_ref[...])
    g_row = jnp.dot(grow, triu_ref[...], precision=HIGHEST, preferred_element_type=F32)

    eg = jnp.exp(gx)
    g_last = gx[C - 1:C, :]
    q_dec = qn * eg
    k_dec = kn * jnp.exp(g_last - gx)
    kb = kn * bx
    vb = v * bx
    kbg = kb * eg
    e_last = jnp.exp(g_last)

    ri = lax.broadcasted_iota(jnp.int32, (C, C), 0)
    ci = lax.broadcasted_iota(jnp.int32, (C, C), 1)
    incl = ri >= ci
    strict = ri > ci
    eye = (ri == ci).astype(F32)
    sh = min(C, SUBLANES).bit_length() - 1
    diag_blk = (ri >> sh) == (ci >> sh)
    merge_masks = []
    while (1 << sh) < C:
        merge_masks.append(((ri >> (sh + 1)) == (ci >> (sh + 1))) & ((ri >> sh) == (ci >> sh) + 1))
        sh += 1

    heads = range(GDN_HEADS)
    sl = lambda h: slice(h * HD, (h + 1) * HD)
    bfl = lambda xs: [x.astype(BF16) for x in xs]

    dec = [jnp.where(incl, jnp.exp(jnp.where(incl, gx[:, h * HD:h * HD + C] - g_row[h:h + 1, :], 0.0)), 0.0)
           for h in heads]
    r = [_dot_nt(jnp.concatenate([kb[:, sl(h)], qn[:, sl(h)]], 0).astype(BF16), kn[:, sl(h)].astype(BF16))
         for h in heads]
    a_kk = [jnp.where(strict, r[h][:C] * dec[h], 0.0) for h in heads]
    a_qk = bfl([r[h][C:] * dec[h] for h in heads])

    nd = [jnp.where(diag_blk, -a_kk[h], 0.0) for h in heads]
    ndb = bfl(nd)
    p2 = [_dot(ndb[h], ndb[h]) for h in heads]
    t = [eye + nd[h] for h in heads]
    r2 = [_dot(p2[h].astype(BF16), jnp.concatenate([t[h], p2[h]], 1).astype(BF16)) for h in heads]
    t = [t[h] + r2[h][:, :C] for h in heads]
    t = [t[h] + _dot(r2[h][:, C:].astype(BF16), t[h].astype(BF16)) for h in heads]
    for mm in merge_masks:
        tb = bfl(t)
        tl = bfl([_dot(tb[h], jnp.where(mm, a_kk[h], 0.0).astype(BF16)) for h in heads])
        t = [t[h] - _dot(tl[h], tb[h]) for h in heads]
    tb = bfl(t)

    rhs = [jnp.concatenate([vb[:, sl(h)], kbg[:, sl(h)]], 1) for h in heads]
    x0 = [_dot(tb[h], rhs[h].astype(BF16)) for h in heads]
    res = []
    for h in heads:
        ah = a_kk[h].astype(BF16)
        al = (a_kk[h] - ah.astype(F32)).astype(BF16)
        xh = x0[h].astype(BF16)
        xl = (x0[h] - xh.astype(F32)).astype(BF16)
        hh = _dot(ah, jnp.concatenate([xh, xl], 1))
        res.append(rhs[h] - x0[h] - (hh[:, :2 * HD] + hh[:, 2 * HD:] + _dot(al, xh)))
    uw = [x0[h] + _dot(tb[h], res[h].astype(BF16)) for h in heads]

    s_old = [s_ref[h] for h in heads]
    wq = [_dot(jnp.concatenate([uw[h][:, HD:], q_dec[:, sl(h)]], 0).astype(BF16), s_old[h].astype(BF16))
          for h in heads]
    v_new = bfl([uw[h][:, :HD] - wq[h][:C] for h in heads])
    o = [wq[h][C:] + _dot(a_qk[h], v_new[h]) for h in heads]
    for h in heads:
        s_ref[h] = s_old[h] * e_last[:, sl(h)] + _dot_tn(k_dec[:, sl(h)].astype(BF16), v_new[h])

    z = z_ref[...].astype(F32)
    nw = nw_ref[...]
    outs = []
    for h in heads:
        o_h = o[h] * lax.rsqrt(jnp.mean(o[h] * o[h], -1, keepdims=True) + 1e-6)
        zh = z[:, sl(h)]
        outs.append(o_h * nw * (zh * _sig(zh)))
    o_ref[...] = jnp.concatenate(outs, 1).astype(BF16)


def _gdn(qkv, z, small, conv_buf, s0, conv_w, a_log, dt_bias, norm_w, l_valid, chunk):
    b, lp, w3 = qkv.shape
    C = chunk
    nc = lp // C
    H, HD, W = GDN_HEADS, GDN_HEAD_DIM, GDN_WIDTH
    abt = small[:, :, :2 * H].reshape(b, nc, C, 2 * H).transpose(0, 1, 3, 2)
    cb = jnp.pad(conv_buf, ((0, 0), (SUBLANES - (GDN_CONV - 1), 0), (0, 0)))
    prow = jnp.zeros((2, LANES), F32).at[0, :H].set(a_log).at[1, :H].set(dt_bias)
    alog_r = jnp.broadcast_to(a_log[:, None], (H, C))
    dtb_r = jnp.broadcast_to(dt_bias[:, None], (H, C))
    ix = jnp.arange(C)
    tri = (ix[:, None] >= ix[None, :]).astype(F32)
    triu = tri.T
    hid = jnp.arange(W) // HD
    hw = 2 * LANES
    bd = (hid[:hw, None] == hid[None, :hw]).astype(BF16)
    lane = jnp.arange(LANES)
    expg = (lane[:, None] == hid[None, :]).astype(BF16)
    expb = (lane[:, None] == hid[None, :] + H).astype(BF16)
    const = lambda shape: pl.BlockSpec(shape, lambda i, j: (0,) * len(shape))
    o, s_fin = pl.pallas_call(
        functools.partial(_gdn_kernel, chunk=C, l_valid=l_valid, l_pad=lp),
        grid=(b, nc),
        in_specs=[pl.BlockSpec((None, C, w3), lambda i, j: (i, j, 0)),
                  pl.BlockSpec((None, C, W), lambda i, j: (i, j, 0)),
                  pl.BlockSpec((None, C, LANES), lambda i, j: (i, j, 0)),
                  pl.BlockSpec((None, None, 2 * H, C), lambda i, j: (i, j, 0, 0)),
                  pl.BlockSpec((None, SUBLANES, w3), lambda i, j: (i, 0, 0)),
                  const((GDN_CONV, w3)), const((2, LANES)), const((H, C)), const((H, C)), const((1, HD)),
                  pl.BlockSpec((None, H, HD, HD), lambda i, j: (i, 0, 0, 0)),
                  const((C, C)), const((C, C)), const((hw, hw)), const((LANES, W)), const((LANES, W))],
        out_specs=[pl.BlockSpec((None, C, W), lambda i, j: (i, j, 0)),
                   pl.BlockSpec((None, H, HD, HD), lambda i, j: (i, 0, 0, 0))],
        out_shape=[jax.ShapeDtypeStruct((b, lp, W), BF16), jax.ShapeDtypeStruct((b, H, HD, HD), F32)],
        scratch_shapes=[pltpu.VMEM((SUBLANES + C, w3), F32)],
        compiler_params=_params("parallel", "arbitrary"),
        name="gdn",
    )(qkv, z, small, abt, cb, conv_w, prow, alog_r, dtb_r, norm_w.reshape(1, HD), s0,
      tri, triu, bd, expg, expb)
    return o, s_fin


def _pool_tile(x, w, pm):
    t = x.shape[1]
    wt = jnp.concatenate([w] * (t // LANES), 1) if t > LANES else w
    return _dot_split2(x * wt, pm)


def _pool_kernel(x_ref, w_ref, pm_ref, o_ref):
    o_ref[...] = _pool_tile(x_ref[...], w_ref[...], pm_ref[...])


def _pool_matrix(t):
    nb = -(-(t // CMP_BLOCK) // LANES) * LANES
    return (jnp.arange(t)[:, None] // CMP_BLOCK == jnp.arange(nb)[None, :]).astype(BF16)


def _pool(kvt, wrow):
    b, rows, t = kvt.shape
    pm = _pool_matrix(t)
    return pl.pallas_call(
        _pool_kernel,
        grid=(b,),
        in_specs=[pl.BlockSpec((None, rows, t), lambda i: (i, 0, 0)),
                  pl.BlockSpec(wrow.shape, lambda i: (0, 0)),
                  pl.BlockSpec(pm.shape, lambda i: (0, 0))],
        out_specs=pl.BlockSpec((None, rows, pm.shape[1]), lambda i: (i, 0, 0)),
        out_shape=jax.ShapeDtypeStruct((b, rows, pm.shape[1]), F32),
        compiler_params=_params("parallel"),
        name="pool",
    )(kvt, wrow, pm)


def _pool_paged_kernel(pt_ref, *refs, pages):
    del pt_ref
    w_ref, pm_ref, o_ref = refs[pages:]
    x = jnp.concatenate([refs[p][...] for p in range(pages)], 1)
    o_ref[...] = _pool_tile(x, w_ref[...], pm_ref[...])


def _pool_paged(cache_t, page_table, wrow, pages):
    _, rows, ps = cache_t.shape
    b, n_pages = page_table.shape
    pm = _pool_matrix(pages * ps)
    assert pm.shape[1] == pages * ps // CMP_BLOCK
    page_spec = lambda p: pl.BlockSpec((None, rows, ps), lambda i, j, pt: (pt[i, j * pages + p], 0, 0))
    return pl.pallas_call(
        functools.partial(_pool_paged_kernel, pages=pages),
        grid_spec=pltpu.PrefetchScalarGridSpec(
            num_scalar_prefetch=1,
            grid=(b, n_pages // pages),
            in_specs=[page_spec(p) for p in range(pages)]
                     + [pl.BlockSpec(wrow.shape, lambda i, j, pt: (0, 0)),
                        pl.BlockSpec(pm.shape, lambda i, j, pt: (0, 0))],
            out_specs=pl.BlockSpec((None, rows, pm.shape[1]), lambda i, j, pt: (i, 0, j))),
        out_shape=jax.ShapeDtypeStruct((b, rows, n_pages * ps // CMP_BLOCK), F32),
        compiler_params=_params("parallel", "arbitrary"),
        name="pool_paged",
    )(page_table, *([cache_t] * pages), wrow, pm)


def _cmp_probs(sc, pos, tq):
    n = sc.shape[1]
    lane = lax.broadcasted_iota(jnp.int32, (1, n), 1)
    maskc = ((lane + 1) * CMP_BLOCK - 1) <= pos
    ps = []
    imp = None
    for g in range(NSA_GROUP):
        s = jnp.where(maskc, sc[g * tq:(g + 1) * tq], NEG)
        m = jnp.max(s, -1, keepdims=True)
        e = jnp.where(maskc, jnp.exp(s - m), 0.0)
        den = jnp.sum(e, -1, keepdims=True)
        p = e / jnp.where(den > 0.0, den, 1.0)
        ps.append(p)
        imp = p if imp is None else imp + p
    return jnp.concatenate(ps, 0), imp


def _select_blocks(imp, pos, axis):
    n = imp.shape[axis]
    idx = lax.broadcasted_iota(jnp.int32, (n, 1) if axis == 0 else (1, n), axis)
    even = (idx & 1) == 0
    imp2 = imp + jnp.where(even, pltpu.roll(imp, n - 1, axis), pltpu.roll(imp, 1, axis))
    blk = idx >> 1
    valid = blk * SEL_BLOCK <= pos
    cur = pos >> 6
    forced = (blk == 0) | (blk == cur) | (blk == cur - 1)
    score = jnp.where(valid, jnp.where(forced, BIG, imp2), -1.0)
    work = jnp.where(even, score, -2.0)
    idxf = idx.astype(F32)
    sel = jnp.zeros(work.shape, F32)
    for _ in range(SEL_TOPK):
        m = jnp.max(work, axis, keepdims=True)
        first = jnp.min(jnp.where(work == m, idxf, 1e9), axis, keepdims=True)
        pick = idxf == first
        sel = jnp.where(pick, 1.0, sel)
        work = jnp.where(pick, -2.0, work)
    sel = jnp.where(score >= 0.0, sel, 0.0)
    return sel + pltpu.roll(sel, 1, axis)


def _online_update(carry, s, mf, vt):
    m, l, acc = carry
    s = jnp.where(mf > 0.5, s, NEG)
    m_new = jnp.maximum(m, jnp.max(s, -1, keepdims=True))
    alpha = jnp.exp(m - m_new)
    p = jnp.exp(s - m_new) * mf
    l = alpha * l + jnp.sum(p, -1, keepdims=True)
    acc = alpha * acc + _dot_nt(p.astype(BF16), vt)
    return m_new, l, acc


def _expand_matrix(n_blocks, n_keys):
    return (jnp.arange(n_keys)[None, :] // CMP_BLOCK == jnp.arange(n_blocks)[:, None]).astype(BF16)


def _nsa_prompt_kernel(qt_ref, ks_ref, kw_ref, kvb_ref, kvs_ref, kvw_ref, sm_ref, et_ref, o_ref, *, n_rows):
    TQ, TK, G, HD, KV = NSA_TQ, NSA_TK, NSA_GROUP, NSA_HEAD_DIM, NSA_KV_HEADS
    i = pl.program_id(1)
    t0 = i * TQ
    pos = t0 + lax.broadcasted_iota(jnp.int32, (1, TQ), 1)
    gst = _sig(sm_ref[...]).T
    gate_row = lambda br, h, g: 2 * GDN_HEADS + br * NSA_Q_HEADS + h * G + g

    n_wt = WINDOW // TQ + 1
    w_off = [pl.multiple_of(jnp.maximum(i - (n_wt - 1) + j, 0) * TQ, TQ) for j in range(n_wt)]
    cidx = lax.broadcasted_iota(jnp.int32, (WINDOW + TQ, 1), 0)
    diff = lax.broadcasted_iota(jnp.int32, (1, TQ), 1) + WINDOW - cidx
    bw = jnp.where((diff >= 0) & (diff < WINDOW) & (t0 - WINDOW + cidx >= 0), 0.0, NEG)
    kidx = lax.broadcasted_iota(jnp.int32, (TK, 1), 0)
    n_used = (t0 + TQ + TK - 1) // TK

    krow = lambda h: slice(h * HD, (h + 1) * HD)
    vrow = lambda h: slice((KV + h) * HD, (KV + h + 1) * HD)
    zeros = jnp.zeros((HD, TQ), BF16)
    blk = lax.broadcasted_iota(jnp.int32, (kvb_ref.shape[1], 1), 0)
    maskc = ((blk + 1) * CMP_BLOCK - 1) <= pos
    kw_all = jnp.concatenate([kw_ref[pl.ds(o, TQ), :] for o in w_off], 0)
    qbd, oc, ow, selt = [], [], [], []
    for h in range(KV):
        qh = [qt_ref[(h * G + g) * HD:(h * G + g + 1) * HD, :] for g in range(G)]
        qbd.append([jnp.concatenate([zeros] * h + [q] + [zeros] * (KV - 1 - h), 0) for q in qh])

        kc = kvb_ref[krow(h), :].astype(BF16)
        vc = kvb_ref[vrow(h), :].astype(BF16)
        imp = None
        oc.append([])
        for g in range(G):
            s = jnp.where(maskc, _dot_tn(kc, qh[g]), NEG)
            e = jnp.where(maskc, jnp.exp(s - jnp.max(s, 0, keepdims=True)), 0.0)
            den = jnp.sum(e, 0, keepdims=True)
            p = e / jnp.where(den > 0.0, den, 1.0)
            imp = p if imp is None else imp + p
            oc[h].append(_dot(vc, p.astype(BF16)))
        selt.append(_select_blocks(imp[:n_rows], pos, 0).astype(BF16))

        vw = jnp.concatenate([kvw_ref[vrow(h), pl.ds(o, TQ)] for o in w_off], 1).astype(BF16)
        vw = jnp.concatenate([vw, jnp.ones_like(vw)], 0)
        ow.append([])
        for g in range(G):
            s = _dot(kw_all, qbd[h][g]) + bw
            e = jnp.exp(s - jnp.max(s, 0, keepdims=True))
            r = _dot(vw, e.astype(BF16))
            ow[h].append(r[:HD] / r[HD:HD + 1])

    def scores(kt):
        off = pl.multiple_of(kt * TK, TK)
        k_t = ks_ref[pl.ds(off, TK), :]
        e_t = et_ref[pl.ds(off, TK), :]
        causal = off + kidx <= pos
        out = []
        for h in range(KV):
            bias = jnp.where((_dot(e_t, selt[h]) > 0.5) & causal, 0.0, NEG)
            out += [_dot(k_t, qbd[h][g]) + bias for g in range(G)]
        return out

    def body(kt, carry):
        off = pl.multiple_of(kt * TK, TK)
        s_next = scores(jnp.minimum(kt + 1, n_used - 1))
        out = []
        for h in range(KV):
            v_t = kvs_ref[vrow(h), pl.ds(off, TK)].astype(BF16)
            v_t = jnp.concatenate([v_t, jnp.ones_like(v_t)], 0)
            for g in range(G):
                m, acc, s = carry[h * G + g]
                m_new = jnp.maximum(m, jnp.max(s, 0, keepdims=True))
                p = jnp.exp(s - m_new)
                out.append((m_new, jnp.exp(m - m_new) * acc + _dot(v_t, p.astype(BF16)), s_next[h * G + g]))
        return tuple(out)

    s0 = scores(0)
    init = tuple((jnp.full((1, TQ), NEG, F32), jnp.zeros((2 * HD, TQ), F32), s0[j]) for j in range(KV * G))
    res = lax.fori_loop(0, n_used, body, init)

    outs = []
    for h in range(KV):
        for g in range(G):
            acc = res[h * G + g][1]
            gate = lambda br: gst[gate_row(br, h, g):gate_row(br, h, g) + 1, :]
            outs.append(gate(0) * oc[h][g] + gate(1) * (acc[:HD] / acc[HD:HD + 1]) + gate(2) * ow[h][g])
    o_ref[...] = jnp.concatenate(outs, 0).astype(BF16)


def _nsa_prompt(q_t, ks, kw, kvb_t, kvs_t, kvw_t, small):
    b, _, l = q_t.shape
    TQ, TK = NSA_TQ, NSA_TK
    n_cmp = l // CMP_BLOCK
    assert kvb_t.shape[2] == LANES and n_cmp <= LANES and l % TK == 0 and WINDOW % TQ == 0
    n_rows = -(-n_cmp // SUBLANES) * SUBLANES
    emat_t = _expand_matrix(n_rows, l).T
    slab = lambda t: pl.BlockSpec((None,) + t.shape[1:], lambda bi, i: (bi, 0, 0))
    return pl.pallas_call(
        functools.partial(_nsa_prompt_kernel, n_rows=n_rows),
        grid=(b, l // TQ),
        in_specs=[pl.BlockSpec((None, NSA_WIDTH, TQ), lambda bi, i: (bi, 0, i)),
                  slab(ks), slab(kw), slab(kvb_t), slab(kvs_t), slab(kvw_t),
                  pl.BlockSpec((None, TQ, LANES), lambda bi, i: (bi, i, 0)),
                  pl.BlockSpec((l, n_rows), lambda bi, i: (0, 0))],
        out_specs=pl.BlockSpec((None, NSA_WIDTH, TQ), lambda bi, i: (bi, 0, i)),
        out_shape=jax.ShapeDtypeStruct((b, NSA_WIDTH, l), BF16),
        compiler_params=_params("parallel", "arbitrary"),
        name="nsa_prompt",
    )(q_t, ks, kw, kvb_t, kvs_t, kvw_t, small, emat_t)


LS = SUBLANES
ROWS_H = NSA_GROUP * LS


def _row_token(rows):
    return lax.broadcasted_iota(jnp.int32, (rows, 1), 0) & (LS - 1)


def _nsa_sample_a_kernel(q_ref, kvb_ref, cw_ref, new_ref, oc_ref, ow_ref, sel_ref, *, past, l_new):
    HD, KV = NSA_HEAD_DIM, NSA_KV_HEADS
    pos = past + lax.broadcasted_iota(jnp.int32, (LS, 1), 0)
    wb = cw_ref.shape[1]
    tok = _row_token(ROWS_H)
    c1 = lax.broadcasted_iota(jnp.int32, (1, wb), 1)
    d1 = tok + wb - c1
    m1 = jnp.where((d1 >= 0) & (d1 < WINDOW) & (past - wb + c1 >= 0), 1.0, 0.0)
    c2 = lax.broadcasted_iota(jnp.int32, (1, new_ref.shape[1]), 1)
    d2 = tok - c2
    m2 = jnp.where((d2 >= 0) & (d2 < WINDOW) & (c2 < l_new), 1.0, 0.0)
    imps = []
    for h in range(KV):
        krow = slice(h * HD, (h + 1) * HD)
        vrow = slice((KV + h) * HD, (KV + h + 1) * HD)
        qh = q_ref[h]
        p, imp = _cmp_probs(_dot(qh, kvb_ref[krow, :].astype(BF16)), pos, LS)
        imps.append(imp)
        oc_ref[h] = _dot_nt(p.astype(BF16), kvb_ref[vrow, :].astype(BF16))

        s1 = jnp.where(m1 > 0.5, _dot(qh, cw_ref[krow, :].astype(BF16)), NEG)
        s2 = jnp.where(m2 > 0.5, _dot(qh, new_ref[krow, :].astype(BF16)), NEG)
        m = jnp.maximum(jnp.max(s1, -1, keepdims=True), jnp.max(s2, -1, keepdims=True))
        e1 = jnp.exp(s1 - m) * m1
        e2 = jnp.exp(s2 - m) * m2
        den = jnp.sum(e1, -1, keepdims=True) + jnp.sum(e2, -1, keepdims=True)
        den = jnp.where(den > 0.0, den, 1.0)
        ow_ref[h] = (_dot_nt((e1 / den).astype(BF16), cw_ref[vrow, :].astype(BF16))
                     + _dot_nt((e2 / den).astype(BF16), new_ref[vrow, :].astype(BF16)))
    sel = _select_blocks(jnp.concatenate(imps, 0), jnp.concatenate([pos] * KV, 0), 1)
    for h in range(KV):
        sel_ref[h] = sel[h * LS:(h + 1) * LS]


def _nsa_sample_a(q_rows, kvb_t, cache_win_t, kvw_new_t, past, l_new):
    b, _, ncp = kvb_t.shape
    per_seq = lambda t: pl.BlockSpec((None,) + t.shape[1:], lambda i: (i,) + (0,) * (t.ndim - 1))
    out_rows = jax.ShapeDtypeStruct(q_rows.shape, F32)
    sel_shape = jax.ShapeDtypeStruct((b, NSA_KV_HEADS, LS, ncp), F32)
    return pl.pallas_call(
        functools.partial(_nsa_sample_a_kernel, past=past, l_new=l_new),
        grid=(b,),
        in_specs=[per_seq(q_rows), per_seq(kvb_t), per_seq(cache_win_t), per_seq(kvw_new_t)],
        out_specs=[per_seq(out_rows), per_seq(out_rows), per_seq(sel_shape)],
        out_shape=[out_rows, out_rows, sel_shape],
        compiler_params=_params("parallel"),
        name="nsa_sample_a",
    )(q_rows, kvb_t, cache_win_t, kvw_new_t)


def _nsa_sample_sel_kernel(pt_ref, *refs, pages, l_new):
    del pt_ref
    page_refs = refs[:pages]
    q_ref, sel_ref, tail_ref, new_ref, e_ref, oc_ref, ow_ref, gate_ref, o_ref, m_scr, l_scr, acc_scr = refs[pages:]
    HD, KV, G = NSA_HEAD_DIM, NSA_KV_HEADS, NSA_GROUP
    i = pl.program_id(1)
    last = i == pl.num_programs(1) - 1

    @pl.when(i == 0)
    def _():
        m_scr[...] = jnp.full_like(m_scr, NEG)
        l_scr[...] = jnp.zeros_like(l_scr)
        acc_scr[...] = jnp.zeros_like(acc_scr)

    kv = jnp.concatenate([r[...] for r in page_refs], 1)
    emat = e_ref[...]
    for h in range(KV):
        krow = slice(h * HD, (h + 1) * HD)
        vrow = slice((KV + h) * HD, (KV + h + 1) * HD)
        mh = _dot(sel_ref[h].astype(BF16), emat)
        carry = _online_update((m_scr[h], l_scr[h], acc_scr[h]),
                               _dot(q_ref[h], kv[krow].astype(BF16)),
                               jnp.concatenate([mh] * G, 0), kv[vrow].astype(BF16))
        m_scr[h], l_scr[h], acc_scr[h] = carry

    @pl.when(last)
    def _():
        tok = _row_token(ROWS_H)
        c2 = lax.broadcasted_iota(jnp.int32, (1, new_ref.shape[1]), 1)
        for h in range(KV):
            krow = slice(h * HD, (h + 1) * HD)
            vrow = slice((KV + h) * HD, (KV + h + 1) * HD)
            flag = jnp.concatenate([tail_ref[h][:, 0:1]] * G, 0)
            mt = jnp.where((c2 <= tok) & (c2 < l_new) & (flag > 0.5), 1.0, 0.0)
            _, l, acc = _online_update((m_scr[h], l_scr[h], acc_scr[h]),
                                       _dot(q_ref[h], new_ref[krow, :].astype(BF16)), mt,
                                       new_ref[vrow, :].astype(BF16))
            osel = acc / jnp.where(l > 0.0, l, 1.0)
            gs = _sig(gate_ref[h])
            o_ref[h] = gs[:, 0:1] * oc_ref[h] + gs[:, 1:2] * osel + gs[:, 2:3] * ow_ref[h]


def _nsa_sample_sel(q_rows, selmask, cache_sel_t, page_table, kvs_new_t, oc, ow, gate_rows, pages, l_new):
    b, n_pages = page_table.shape
    _, rows, ps = cache_sel_t.shape
    lanes_per_step = pages * ps // CMP_BLOCK
    assert lanes_per_step % LANES == 0 and n_pages % pages == 0
    n_steps = n_pages // pages
    emat = _expand_matrix(lanes_per_step, pages * ps)
    page_spec = lambda p: pl.BlockSpec((None, rows, ps), lambda bi, i, pt: (pt[bi, i * pages + p], 0, 0))
    per_seq = lambda t: pl.BlockSpec((None,) + t.shape[1:], lambda bi, i, pt: (bi,) + (0,) * (t.ndim - 1))
    tail_block = n_steps * lanes_per_step // LANES
    kvh = NSA_KV_HEADS
    return pl.pallas_call(
        functools.partial(_nsa_sample_sel_kernel, pages=pages, l_new=l_new),
        grid_spec=pltpu.PrefetchScalarGridSpec(
            num_scalar_prefetch=1,
            grid=(b, n_steps),
            in_specs=[page_spec(p) for p in range(pages)] + [
                per_seq(q_rows),
                pl.BlockSpec((None, kvh, LS, lanes_per_step), lambda bi, i, pt: (bi, 0, 0, i)),
                pl.BlockSpec((None, kvh, LS, LANES), lambda bi, i, pt: (bi, 0, 0, tail_block)),
                per_seq(kvs_new_t),
                pl.BlockSpec(emat.shape, lambda bi, i, pt: (0, 0)),
                per_seq(oc), per_seq(ow), per_seq(gate_rows)],
            out_specs=per_seq(oc),
            scratch_shapes=[pltpu.VMEM((kvh, ROWS_H, 1), F32), pltpu.VMEM((kvh, ROWS_H, 1), F32),
                            pltpu.VMEM((kvh, ROWS_H, NSA_HEAD_DIM), F32)]),
        out_shape=jax.ShapeDtypeStruct(oc.shape, F32),
        compiler_params=_params("parallel", "arbitrary"),
        name="nsa_sample_sel",
    )(page_table, *([cache_sel_t] * pages), q_rows, selmask, selmask, kvs_new_t, emat, oc, ow, gate_rows)


def _nsa_sample(q_t, kvc_t, kvs_t, kvw_t, gate, cache_cmp_t, cache_sel_t, cache_win_t, page_table, wrow):
    b, _, l = q_t.shape
    KV, G, HD = NSA_KV_HEADS, NSA_GROUP, NSA_HEAD_DIM
    n_pages = page_table.shape[1]
    ps = cache_cmp_t.shape[2]
    past = n_pages * ps
    assert l <= LS and ps == PAGE_SIZE
    pages = min(PAGES_PER_STEP, n_pages)
    pad_new = lambda t: jnp.pad(t, ((0, 0), (0, 0), (0, ps - l)))
    kvb_t = jnp.concatenate([_pool_paged(cache_cmp_t, page_table, wrow, pages), _pool(pad_new(kvc_t), wrow)], 2)
    q_rows = jnp.pad(q_t.reshape(b, KV, G, HD, l).transpose(0, 1, 2, 4, 3),
                     ((0, 0),) * 3 + ((0, LS - l), (0, 0))).reshape(b, KV, ROWS_H, HD)
    oc, ow, selmask = _nsa_sample_a(q_rows, kvb_t, cache_win_t, pad_new(kvw_t), past, l)
    gate_rows = jnp.pad(gate.reshape(b, l, 3, KV, G).transpose(0, 3, 4, 1, 2),
                        ((0, 0),) * 3 + ((0, LS - l), (0, 0))).reshape(b, KV, ROWS_H, 3)
    o = _nsa_sample_sel(q_rows, selmask, cache_sel_t, page_table, pad_new(kvs_t), oc, ow, gate_rows, pages, l)
    o = o.reshape(b, KV, G, LS, HD)[:, :, :, :l]
    return o.transpose(0, 3, 1, 2, 4).reshape(b, l, NSA_WIDTH).astype(BF16)


def _merge_kernel(x_ref, gt_ref, og_ref, on_ref, m_ref, wbg_ref, wbn_ref, wo_ref, lg_ref, lb_ref, o_ref,
                  *, alpha, nsa_feature_major):
    d = x_ref.shape[1]
    m = m_ref[...].astype(F32)
    nsa_dot = _dot_tn if nsa_feature_major else _dot
    mix = m[:, :d] * _dot(og_ref[...], wbg_ref[...]) + m[:, d:] * nsa_dot(on_ref[...], wbn_ref[...])
    y = _dot(mix.astype(BF16), wo_ref[...])
    o_ref[...] = _layer_norm(alpha * x_ref[...] + gt_ref[...] * y, lg_ref[...], lb_ref[...])


def _merge(x, mods, kind, rows_per_seq, o_gdn, o_nsa, msig, wbg, wbn, wo, lg, lb, alpha, tm):
    n, d = x.shape
    tps = max(rows_per_seq // tm, 1)
    row = lambda wd: pl.BlockSpec((tm, wd), lambda i: (i, 0))
    full = lambda a: pl.BlockSpec(a.shape, lambda i: (0, 0))
    feature_major = o_nsa.ndim == 3
    nsa_spec = (pl.BlockSpec((None, o_nsa.shape[1], tm), lambda i: (i // tps, 0, i % tps)) if feature_major
                else row(o_nsa.shape[1]))
    return pl.pallas_call(
        functools.partial(_merge_kernel, alpha=alpha, nsa_feature_major=feature_major),
        grid=(n // tm,),
        in_specs=[row(d), _mod_spec(kind, tm, d, tps, 5), row(o_gdn.shape[1]), nsa_spec,
                  row(2 * d), full(wbg), full(wbn), full(wo), full(lg), full(lb)],
        out_specs=row(d),
        out_shape=jax.ShapeDtypeStruct((n, d), F32),
        compiler_params=_params("parallel"),
        name="merge",
    )(x, mods, o_gdn, o_nsa, msig, wbg, wbn, wo, lg, lb)


def _row_tile(n, pref):
    t = min(pref, n)
    while n % t:
        t //= 2
    return t


def _ff_tile(dff):
    for parts in (4, 2, 1, 11, 22):
        if dff % parts == 0 and (dff // parts) % LANES == 0:
            return dff // parts
    return dff


def _to_rows(kv_t):
    b, _, t = kv_t.shape
    return kv_t.reshape(b, 2, NSA_KV_HEADS, NSA_HEAD_DIM, t).transpose(0, 4, 1, 2, 3)


def _to_feature_major(kv_rows):
    n, t = kv_rows.shape[:2]
    return kv_rows.transpose(0, 2, 3, 4, 1).reshape(n, KV2, t)


def _layer(x, mod, kind, past, lw, alpha):
    b, l, d = x.shape
    n = b * l
    (wg1, wu1, wd1, wg2, wu2, wd2, w_in_parts, wk, wt, conv_w, a_log, dt_bias, norm_w, wrow,
     wbg, wbn, wo, ln_g, ln_b) = lw
    tm = _row_tile(l if kind == "seq" else n, 512)
    tm_in = _row_tile(l if kind == "seq" else n, 256)
    tf = _ff_tile(wg1.shape[1])
    tm_ff = _row_tile(l if kind == "seq" else n, 1024)
    lg = lambda i: ln_g[i].reshape(1, d)
    lb = lambda i: ln_b[i].reshape(1, d)

    x1 = _ffn(x.reshape(n, d), mod, kind, l, 0, wg1, wu1, wd1, lg(0), lb(0), alpha, tm_ff, tf)

    qkv, z, msig, small, ks, kw, q_t, kvc_t, kvs_t, kvw_t = _inproj(x1, mod, kind, l, w_in_parts, wk, wt, tm_in)
    seq = lambda t: t.reshape(b, l, t.shape[-1])
    qkv, z, small, ks, kw = [seq(t) for t in (qkv, z, small, ks, kw)]
    if kind == "tok":
        q_t, kvc_t, kvs_t, kvw_t = [t.reshape(t.shape[0], b, l).transpose(1, 0, 2)
                                    for t in (q_t, kvc_t, kvs_t, kvw_t)]

    if past is None:
        conv_buf = jnp.zeros((b, GDN_CONV - 1, 3 * GDN_WIDTH), F32)
        s0 = jnp.zeros((b, GDN_HEADS, GDN_HEAD_DIM, GDN_HEAD_DIM), F32)
        o_nsa = _nsa_prompt(q_t, ks, kw, _pool(kvc_t, wrow), kvs_t, kvw_t, small)
        win_t = kvw_t[:, :, l - min(WINDOW, l):]
    else:
        s0, conv_buf, cache_cmp, cache_sel, cache_win, page_table = past
        cache_win_t = _to_feature_major(cache_win)
        gate = small[:, :, 2 * GDN_HEADS:2 * GDN_HEADS + 3 * NSA_Q_HEADS]
        o_nsa = _nsa_sample(q_t, kvc_t, kvs_t, kvw_t, gate, _to_feature_major(cache_cmp),
                            _to_feature_major(cache_sel), cache_win_t, page_table, wrow)
        o_nsa = o_nsa.reshape(n, NSA_WIDTH)
        win_t = jnp.concatenate([cache_win_t, kvw_t], 2)[:, :, l:]

    chunk = min(GDN_CHUNK, -(-l // SUBLANES) * SUBLANES)
    lp = -(-l // chunk) * chunk
    padl = lambda t: jnp.pad(t, ((0, 0), (0, lp - l), (0, 0)))
    o_gdn, s_new = _gdn(padl(qkv), padl(z), padl(small), conv_buf, s0, conv_w, a_log, dt_bias, norm_w, l, chunk)
    conv_new = jnp.concatenate([conv_buf, qkv], 1)[:, -(GDN_CONV - 1):]

    x2 = _merge(x1, mod, kind, l, o_gdn[:, :l].reshape(n, GDN_WIDTH), o_nsa, msig,
                wbg, wbn, wo, lg(1), lb(1), alpha, tm)
    x3 = _ffn(x2, mod, kind, l, 6, wg2, wu2, wd2, lg(2), lb(2), alpha, tm_ff, tf)
    return x3.reshape(b, l, d), (s_new, conv_new, _to_rows(kvc_t), _to_rows(kvs_t), _to_rows(win_t))


def _split_w_in(w_in, d):
    splits = (3 * GDN_WIDTH, GDN_WIDTH, GDN_HEADS, GDN_HEADS, NSA_WIDTH, KV2, KV2, KV2, 3 * NSA_Q_HEADS, 2 * d)
    offs = [0]
    for s in splits:
        offs.append(offs[-1] + s)
    qkv, z, a, bb, q, kc, ks, kw, gate, merge = [w_in[:, offs[i]:offs[i + 1]] for i in range(len(splits))]
    n_small = 2 * GDN_HEADS + 3 * NSA_Q_HEADS
    small = jnp.pad(jnp.concatenate([a, bb, gate], 1), ((0, 0), (0, LANES - n_small)))
    wk = jnp.concatenate([ks[:, :KV_WIDTH], kw[:, :KV_WIDTH]], 1)
    wt = jnp.concatenate([q, kc, ks, kw], 1).T
    return [t.astype(BF16) for t in (qkv, z, merge, small)], wk.astype(BF16), wt.astype(BF16)


def kernel(x_prompt, x_sample, c_prompt, c_sample, state_gdn, state_gdn_conv, cache_cmp_kv, cache_sel_kv, cache_win_kv, page_table, ln_g, ln_b, w_ada, b_ada, w_ff1_gu, w_ff1_dn, w_ff2_gu, w_ff2_dn, w_in, gdn_conv_w, gdn_a_log, gdn_dt_bias, gdn_norm_w, nsa_w_cmp, w_br_gdn, w_br_nsa, w_out):
    depth = w_in.shape[0]
    alpha = (2.0 * depth) ** 0.25
    bp, lp, d = x_prompt.shape
    bs, ls, _ = x_sample.shape
    y_p, y_s = x_prompt, x_sample
    p_st, s_st = [], []
    for l in range(depth):
        dff = w_ff1_dn.shape[1]
        bf = lambda t: t.astype(BF16)
        wrow = jnp.tile(jnp.broadcast_to(nsa_w_cmp[l].transpose(0, 2, 1)[:, :, None, :],
                                         (2, NSA_KV_HEADS, NSA_HEAD_DIM, CMP_BLOCK)).reshape(KV2, CMP_BLOCK),
                        (1, LANES // CMP_BLOCK))
        w_in_parts, wk, wt = _split_w_in(w_in[l], d)
        lw = (bf(w_ff1_gu[l][:, :dff]), bf(w_ff1_gu[l][:, dff:]), bf(w_ff1_dn[l]),
              bf(w_ff2_gu[l][:, :dff]), bf(w_ff2_gu[l][:, dff:]), bf(w_ff2_dn[l]),
              w_in_parts, wk, wt, gdn_conv_w[l], gdn_a_log[l], gdn_dt_bias[l], gdn_norm_w[l], wrow,
              bf(w_br_gdn[l]), bf(w_br_nsa[l]), bf(w_out[l]), ln_g[l], ln_b[l])
        c_all = jnp.concatenate([c_prompt, c_sample], 0)
        r = c_all.shape[0]
        rp = -(-r // SUBLANES) * SUBLANES
        mod = _ada(jnp.pad(c_all, ((0, rp - r), (0, 0))), w_ada[l], b_ada[l])
        mod_p = mod[:bp].reshape(bp * 9, 1, d)
        mod_s = jnp.repeat(mod[bp:bp + bs], ls, axis=0)
        y_p, st_p = _layer(y_p, mod_p, "seq", None, lw, alpha)
        past = (state_gdn[l], state_gdn_conv[l], cache_cmp_kv[l], cache_sel_kv[l], cache_win_kv[l], page_table)
        y_s, st_s = _layer(y_s, mod_s, "tok", past, lw, alpha)
        p_st.append(st_p)
        s_st.append(st_s)
    p_out = [jnp.stack(t) for t in zip(*p_st)]
    s_out = [jnp.stack(t) for t in zip(*s_st)]
    return (y_p, y_s, *p_out, *s_out)
```

```python
import functools

import jax
import jax.numpy as jnp
from jax import lax
from jax.experimental import pallas as pl
from jax.experimental.pallas import tpu as pltpu

F32 = jnp.float32
BF16 = jnp.bfloat16
HIGHEST = lax.Precision.HIGHEST

GDN_HEADS = 8
GDN_HEAD_DIM = 64
GDN_WIDTH = GDN_HEADS * GDN_HEAD_DIM
GDN_CONV = 4
GDN_CHUNK = 64
NSA_Q_HEADS = 8
NSA_KV_HEADS = 2
NSA_HEAD_DIM = 64
NSA_GROUP = NSA_Q_HEADS // NSA_KV_HEADS
NSA_WIDTH = NSA_Q_HEADS * NSA_HEAD_DIM
KV_WIDTH = NSA_KV_HEADS * NSA_HEAD_DIM
KV2 = 2 * KV_WIDTH
CMP_BLOCK = 32
SEL_BLOCK = 64
SEL_TOPK = 16
WINDOW = 512
PAGE_SIZE = 128
NEG = -1e30
BIG = 1e4
LN_EPS = 1e-5

SUBLANES = 8
LANES = 128
VMEM_LIMIT = 52 * 1024 * 1024

NSA_TQ = 128
NSA_TK = 256
PAGES_PER_STEP = 64
PAGES_PER_TILE = LANES * CMP_BLOCK // PAGE_SIZE
GDN_SEQS_PER_STEP = 2
SAMPLE_SEQS_PER_STEP = 4


def _sig(x):
    return 1.0 / (1.0 + jnp.exp(-x))


def _softplus(x):
    return jnp.maximum(x, 0.0) + jnp.log(1.0 + jnp.exp(-jnp.abs(x)))


def _layer_norm(r, g, b):
    mu = jnp.mean(r, -1, keepdims=True)
    d = r - mu
    var = jnp.mean(d * d, -1, keepdims=True)
    return d * lax.rsqrt(var + LN_EPS) * g + b


def _dot(a, b):
    return jnp.dot(a, b, preferred_element_type=F32)


def _dot_nt(a, b):
    return lax.dot_general(a, b, (((1,), (1,)), ((), ())), preferred_element_type=F32)


def _dot_tn(a, b):
    return lax.dot_general(a, b, (((0,), (0,)), ((), ())), preferred_element_type=F32)


def _dot_split2(a, sel):
    hi = a.astype(BF16)
    lo = (a - hi.astype(F32)).astype(BF16)
    return _dot(hi, sel) + _dot(lo, sel)


def _params(*sem):
    return pltpu.CompilerParams(dimension_semantics=sem, vmem_limit_bytes=VMEM_LIMIT)


def _mod_spec(kind, tm, d, tiles_per_seq, k):
    if kind == "seq":
        return pl.BlockSpec((None, 1, d), lambda i, *_: ((i // tiles_per_seq) * 9 + k, 0, 0))
    return pl.BlockSpec((tm, d), lambda i, *_: (i, k))


def _ada_kernel(c_ref, w_ref, b_ref, o_ref):
    c = c_ref[...]
    h = (c * _sig(c)).astype(BF16)
    o_ref[...] = _dot(h, w_ref[...].astype(BF16)) + b_ref[...]


def _ada(c, w, b):
    r, d = c.shape
    n = w.shape[1]
    tn = d
    return pl.pallas_call(
        _ada_kernel,
        grid=(n // tn,),
        in_specs=[pl.BlockSpec((r, d), lambda j: (0, 0)),
                  pl.BlockSpec((d, tn), lambda j: (0, j)),
                  pl.BlockSpec((1, tn), lambda j: (0, j))],
        out_specs=pl.BlockSpec((r, tn), lambda j: (0, j)),
        out_shape=jax.ShapeDtypeStruct((r, n), F32),
        compiler_params=_params("arbitrary"),
        name="ada",
    )(c, w, b.reshape(1, n))


def _ffn_kernel(x_ref, sh_ref, sc_ref, gt_ref, wg_ref, wu_ref, wd_ref, lg_ref, lb_ref, o_ref,
                h_scr, acc_scr, *, alpha):
    j = pl.program_id(1)

    @pl.when(j == 0)
    def _():
        h_scr[...] = (x_ref[...] * (1.0 + sc_ref[...]) + sh_ref[...]).astype(BF16)
        acc_scr[...] = jnp.zeros_like(acc_scr)

    h = h_scr[...]
    g = _dot(h, wg_ref[...])
    u = _dot(h, wu_ref[...])
    a = (g * _sig(g) * u).astype(BF16)
    acc_scr[...] += _dot(a, wd_ref[...])

    @pl.when(j == pl.num_programs(1) - 1)
    def _():
        r = alpha * x_ref[...] + (0.5 * gt_ref[...]) * acc_scr[...]
        o_ref[...] = _layer_norm(r, lg_ref[...], lb_ref[...])


def _ffn(x, mods, kind, rows_per_seq, k0, wg, wu, wd, lg, lb, alpha, tm, tf):
    n, d = x.shape
    dff = wg.shape[1]
    tps = max(rows_per_seq // tm, 1)
    ms = lambda k: _mod_spec(kind, tm, d, tps, k)
    return pl.pallas_call(
        functools.partial(_ffn_kernel, alpha=alpha),
        grid=(n // tm, dff // tf),
        in_specs=[pl.BlockSpec((tm, d), lambda i, j: (i, 0)),
                  ms(k0), ms(k0 + 1), ms(k0 + 2),
                  pl.BlockSpec((d, tf), lambda i, j: (0, j)),
                  pl.BlockSpec((d, tf), lambda i, j: (0, j)),
                  pl.BlockSpec((tf, d), lambda i, j: (j, 0)),
                  pl.BlockSpec((1, d), lambda i, j: (0, 0)),
                  pl.BlockSpec((1, d), lambda i, j: (0, 0))],
        out_specs=pl.BlockSpec((tm, d), lambda i, j: (i, 0)),
        out_shape=jax.ShapeDtypeStruct((n, d), F32),
        scratch_shapes=[pltpu.VMEM((tm, d), BF16), pltpu.VMEM((tm, d), F32)],
        compiler_params=_params("parallel", "arbitrary"),
        name="ffn",
    )(x, mods, mods, mods, wg, wu, wd, lg, lb)


def _inproj_kernel(x_ref, sh_ref, sc_ref, wqkv, wz, wm, wsm, wk, wt,
                   oqkv, oz, om, osm, oks, okw, oqt, okct, okst, okwt):
    h = (x_ref[...] * (1.0 + sc_ref[...]) + sh_ref[...]).astype(BF16)
    oqkv[...] = _dot(h, wqkv[...])
    oz[...] = _dot(h, wz[...]).astype(BF16)
    om[...] = _sig(_dot(h, wm[...])).astype(BF16)
    osm[...] = _dot(h, wsm[...])
    kt = _dot(h, wk[...]).astype(BF16)
    oks[...] = kt[:, 0:KV_WIDTH]
    okw[...] = kt[:, KV_WIDTH:2 * KV_WIDTH]
    ft = _dot_nt(wt[...], h)
    oqt[...] = (ft[0:NSA_WIDTH] * (NSA_HEAD_DIM ** -0.5)).astype(BF16)
    okct[...] = ft[NSA_WIDTH:NSA_WIDTH + KV2]
    okst[...] = ft[NSA_WIDTH + KV2:NSA_WIDTH + 2 * KV2]
    okwt[...] = ft[NSA_WIDTH + 2 * KV2:NSA_WIDTH + 3 * KV2]


def _inproj(x, mods, kind, rows_per_seq, ws, wk, wt, tm):
    n, d = x.shape
    tps = max(rows_per_seq // tm, 1)
    widths = [w.shape[1] for w in ws] + [KV_WIDTH] * 2
    dtypes = [F32, BF16, BF16, F32, BF16, BF16]
    t_rows = [NSA_WIDTH, KV2, KV2, KV2]
    t_dtypes = [BF16, F32, F32, F32]
    if kind == "seq":
        t_specs = [pl.BlockSpec((None, r, tm), lambda i: (i // tps, 0, i % tps)) for r in t_rows]
        t_shapes = [jax.ShapeDtypeStruct((n // rows_per_seq, r, rows_per_seq), dt) for r, dt in zip(t_rows, t_dtypes)]
    else:
        t_specs = [pl.BlockSpec((r, tm), lambda i: (0, i)) for r in t_rows]
        t_shapes = [jax.ShapeDtypeStruct((r, n), dt) for r, dt in zip(t_rows, t_dtypes)]
    full = lambda w: pl.BlockSpec(w.shape, lambda i: (0, 0))
    return pl.pallas_call(
        _inproj_kernel,
        grid=(n // tm,),
        in_specs=[pl.BlockSpec((tm, d), lambda i: (i, 0)),
                  _mod_spec(kind, tm, d, tps, 3), _mod_spec(kind, tm, d, tps, 4)]
                 + [full(w) for w in ws] + [full(wk), full(wt)],
        out_specs=[pl.BlockSpec((tm, wd), lambda i: (i, 0)) for wd in widths] + t_specs,
        out_shape=[jax.ShapeDtypeStruct((n, wd), dt) for wd, dt in zip(widths, dtypes)] + t_shapes,
        compiler_params=_params("parallel"),
        name="inproj",
    )(x, mods, mods, *ws, wk, wt)


def _gdn_kernel(qkv_ref, z_ref, sm_ref, abt_ref, cb_ref, cw_ref, prow_ref, alog_ref, dtb_ref, nw_ref,
                s0_ref, tri_ref, triu_ref, bd_ref, expg_ref, expb_ref, o_ref, s_ref, xbuf,
                *, chunk, l_valid, l_pad):
    C = chunk
    HD = GDN_HEAD_DIM
    W = GDN_WIDTH
    NB = qkv_ref.shape[0]
    c = pl.program_id(1)

    @pl.when(c == 0)
    def _():
        xbuf[:, 0:SUBLANES, :] = cb_ref[...]
        s_ref[...] = s0_ref[...]

    cw = cw_ref[...]
    bd = bd_ref[...]
    hw = bd.shape[0]
    pr = prow_ref[...]
    qn_l, kn_l, gx_l, g_row_l, q_dec_l, k_dec_l, kb_l, vb_l, kbg_l, e_last_l, z_l = ([] for _ in range(11))
    for n in range(NB):
        x = qkv_ref[n]
        xbuf[n, SUBLANES:SUBLANES + C, :] = x
        y = (xbuf[n, 5:5 + C, :] * cw[0:1] + xbuf[n, 6:6 + C, :] * cw[1:2]
             + xbuf[n, 7:7 + C, :] * cw[2:3] + x * cw[3:4])
        xbuf[n, 0:SUBLANES, :] = x[C - SUBLANES:C, :]
        act = y * _sig(y)
        q = act[:, 0:W]
        k = act[:, W:2 * W]
        v = act[:, 2 * W:3 * W]

        sq = jnp.concatenate([q * q, k * k], 0)
        ss = jnp.concatenate([_dot_split2(sq[:, j:j + hw], bd) for j in range(0, W, hw)], 1)
        qn = q * lax.rsqrt(ss[:C] + 1e-6) * (HD ** -0.5)
        kn = k * lax.rsqrt(ss[C:] + 1e-6)

        sm = sm_ref[n]
        gcol = -jnp.exp(pr[0:1]) * _softplus(sm + pr[1:2])
        bcol = _sig(sm)
        ab = abt_ref[n]
        grow = -jnp.exp(alog_ref[...]) * _softplus(ab[0:GDN_HEADS] + dtb_ref[...])
        if l_pad != l_valid:
            vcol = (c * C + lax.broadcasted_iota(jnp.int32, (C, 1), 0) < l_valid).astype(F32)
            vrow = (c * C + lax.broadcasted_iota(jnp.int32, (1, C), 1) < l_valid).astype(F32)
            qn, kn, v = qn * vcol, kn * vcol, v * vcol
            gcol, bcol, grow = gcol * vcol, bcol * vcol, grow * vrow

        g_cum = jnp.dot(tri_ref[...], gcol, precision=HIGHEST, preferred_element_type=F32)
        gx = _dot_split2(g_cum, expg_ref[...])
        bx = _dot_split2(bcol, expb_ref[...])
        g_row = jnp.dot(grow, triu_ref[...], precision=HIGHEST, preferred_element_type=F32)

        eg = jnp.exp(gx)
        g_last = gx[C - 1:C, :]
        kb = kn * bx
        for lst, val in ((qn_l, qn), (kn_l, kn), (gx_l, gx), (g_row_l, g_row), (q_dec_l, qn * eg),
                         (k_dec_l, kn * jnp.exp(g_last - gx)), (kb_l, kb), (vb_l, v * bx), (kbg_l, kb * eg),
                         (e_last_l, jnp.exp(g_last)), (z_l, z_ref[n].astype(F32))):
            lst.append(val)

    ri = lax.broadcasted_iota(jnp.int32, (C, C), 0)
    ci = lax.broadcasted_iota(jnp.int32, (C, C), 1)
    incl = ri >= ci
    strict = ri > ci
    eye = (ri == ci).astype(F32)
    sh = min(C, SUBLANES).bit_length() - 1
    diag_blk = (ri >> sh) == (ci >> sh)
    merge_masks = []
    while (1 << sh) < C:
        merge_masks.append(((ri >> (sh + 1)) == (ci >> (sh + 1))) & ((ri >> sh) == (ci >> sh) + 1))
        sh += 1

    H = GDN_HEADS
    heads = range(NB * H)
    col = lambda xs, j: xs[j // H][:, (j % H) * HD:(j % H + 1) * HD]
    bfl = lambda xs: [x.astype(BF16) for x in xs]

    dec = [jnp.where(incl, jnp.exp(jnp.where(
        incl, gx_l[j // H][:, (j % H) * HD:(j % H) * HD + C] - g_row_l[j // H][j % H:j % H + 1, :], 0.0)), 0.0)
        for j in heads]
    r = [_dot_nt(jnp.concatenate([col(kb_l, h), col(qn_l, h)], 0).astype(BF16), col(kn_l, h).astype(BF16))
         for h in heads]
    a_kk = [jnp.where(strict, r[h][:C] * dec[h], 0.0) for h in heads]
    a_qk = bfl([r[h][C:] * dec[h] for h in heads])

    nd = [jnp.where(diag_blk, -a_kk[h], 0.0) for h in heads]
    ndb = bfl(nd)
    p2 = [_dot(ndb[h], ndb[h]) for h in heads]
    t = [eye + nd[h] for h in heads]
    r2 = [_dot(p2[h].astype(BF16), jnp.concatenate([t[h], p2[h]], 1).astype(BF16)) for h in heads]
    t = [t[h] + r2[h][:, :C] for h in heads]
    t = [t[h] + _dot(r2[h][:, C:].astype(BF16), t[h].astype(BF16)) for h in heads]
    for mm in merge_masks:
        tb = bfl(t)
        tl = bfl([_dot(tb[h], jnp.where(mm, a_kk[h], 0.0).astype(BF16)) for h in heads])
        t = [t[h] - _dot(tl[h], tb[h]) for h in heads]
    tb = bfl(t)

    rhs = [jnp.concatenate([col(vb_l, h), col(kbg_l, h)], 1) for h in heads]
    x0 = [_dot(tb[h], rhs[h].astype(BF16)) for h in heads]
    res = []
    for h in heads:
        ah = a_kk[h].astype(BF16)
        al = (a_kk[h] - ah.astype(F32)).astype(BF16)
        xh = x0[h].astype(BF16)
        xl = (x0[h] - xh.astype(F32)).astype(BF16)
        hh = _dot(ah, jnp.concatenate([xh, xl], 1))
        res.append(rhs[h] - x0[h] - (hh[:, :2 * HD] + hh[:, 2 * HD:] + _dot(al, xh)))
    uw = [x0[h] + _dot(tb[h], res[h].astype(BF16)) for h in heads]

    s_old = [s_ref[h // H, h % H] for h in heads]
    wq = [_dot(jnp.concatenate([uw[h][:, HD:], col(q_dec_l, h)], 0).astype(BF16), s_old[h].astype(BF16))
          for h in heads]
    v_new = bfl([uw[h][:, :HD] - wq[h][:C] for h in heads])
    o = [wq[h][C:] + _dot(a_qk[h], v_new[h]) for h in heads]
    for h in heads:
        s_ref[h // H, h % H] = (s_old[h] * col(e_last_l, h)
                                + _dot_tn(col(k_dec_l, h).astype(BF16), v_new[h]))

    nw = nw_ref[...]
    outs = []
    for h in heads:
        o_h = o[h] * lax.rsqrt(jnp.mean(o[h] * o[h], -1, keepdims=True) + 1e-6)
        zh = col(z_l, h)
        outs.append(o_h * nw * (zh * _sig(zh)))
    for n in range(NB):
        o_ref[n] = jnp.concatenate(outs[n * H:(n + 1) * H], 1).astype(BF16)


def _gdn(qkv, z, small, conv_buf, s0, conv_w, a_log, dt_bias, norm_w, l_valid, chunk):
    b, lp, w3 = qkv.shape
    C = chunk
    nc = lp // C
    H, HD, W = GDN_HEADS, GDN_HEAD_DIM, GDN_WIDTH
    abt = small[:, :, :2 * H].reshape(b, nc, C, 2 * H).transpose(0, 1, 3, 2)
    cb = jnp.pad(conv_buf, ((0, 0), (SUBLANES - (GDN_CONV - 1), 0), (0, 0)))
    prow = jnp.zeros((2, LANES), F32).at[0, :H].set(a_log).at[1, :H].set(dt_bias)
    alog_r = jnp.broadcast_to(a_log[:, None], (H, C))
    dtb_r = jnp.broadcast_to(dt_bias[:, None], (H, C))
    ix = jnp.arange(C)
    tri = (ix[:, None] >= ix[None, :]).astype(F32)
    triu = tri.T
    hid = jnp.arange(W) // HD
    hw = 2 * LANES
    bd = (hid[:hw, None] == hid[None, :hw]).astype(BF16)
    lane = jnp.arange(LANES)
    expg = (lane[:, None] == hid[None, :]).astype(BF16)
    expb = (lane[:, None] == hid[None, :] + H).astype(BF16)
    const = lambda shape: pl.BlockSpec(shape, lambda i, j: (0,) * len(shape))
    nb = GDN_SEQS_PER_STEP if b % GDN_SEQS_PER_STEP == 0 else 1
    o, s_fin = pl.pallas_call(
        functools.partial(_gdn_kernel, chunk=C, l_valid=l_valid, l_pad=lp),
        grid=(b // nb, nc),
        in_specs=[pl.BlockSpec((nb, C, w3), lambda i, j: (i, j, 0)),
                  pl.BlockSpec((nb, C, W), lambda i, j: (i, j, 0)),
                  pl.BlockSpec((nb, C, LANES), lambda i, j: (i, j, 0)),
                  pl.BlockSpec((nb, None, 2 * H, C), lambda i, j: (i, j, 0, 0)),
                  pl.BlockSpec((nb, SUBLANES, w3), lambda i, j: (i, 0, 0)),
                  const((GDN_CONV, w3)), const((2, LANES)), const((H, C)), const((H, C)), const((1, HD)),
                  pl.BlockSpec((nb, H, HD, HD), lambda i, j: (i, 0, 0, 0)),
                  const((C, C)), const((C, C)), const((hw, hw)), const((LANES, W)), const((LANES, W))],
        out_specs=[pl.BlockSpec((nb, C, W), lambda i, j: (i, j, 0)),
                   pl.BlockSpec((nb, H, HD, HD), lambda i, j: (i, 0, 0, 0))],
        out_shape=[jax.ShapeDtypeStruct((b, lp, W), BF16), jax.ShapeDtypeStruct((b, H, HD, HD), F32)],
        scratch_shapes=[pltpu.VMEM((nb, SUBLANES + C, w3), F32)],
        compiler_params=_params("parallel", "arbitrary"),
        name="gdn",
    )(qkv, z, small, abt, cb, conv_w, prow, alog_r, dtb_r, norm_w.reshape(1, HD), s0,
      tri, triu, bd, expg, expb)
    return o, s_fin


def _pool_tile(x, w, pm):
    t = x.shape[1]
    wt = jnp.concatenate([w] * (t // LANES), 1) if t > LANES else w
    return _dot_split2(x * wt, pm)


def _pool_kernel(x_ref, w_ref, pm_ref, o_ref):
    o_ref[...] = _pool_tile(x_ref[...], w_ref[...], pm_ref[...])


def _pool_matrix(t):
    nb = -(-(t // CMP_BLOCK) // LANES) * LANES
    return (jnp.arange(t)[:, None] // CMP_BLOCK == jnp.arange(nb)[None, :]).astype(BF16)


def _pool(kvt, wrow):
    b, rows, t = kvt.shape
    pm = _pool_matrix(t)
    return pl.pallas_call(
        _pool_kernel,
        grid=(b,),
        in_specs=[pl.BlockSpec((None, rows, t), lambda i: (i, 0, 0)),
                  pl.BlockSpec(wrow.shape, lambda i: (0, 0)),
                  pl.BlockSpec(pm.shape, lambda i: (0, 0))],
        out_specs=pl.BlockSpec((None, rows, pm.shape[1]), lambda i: (i, 0, 0)),
        out_shape=jax.ShapeDtypeStruct((b, rows, pm.shape[1]), F32),
        compiler_params=_params("parallel"),
        name="pool",
    )(kvt, wrow, pm)


def _pool_paged_kernel(pt_ref, *refs, pages):
    del pt_ref
    w_ref, pm_ref, o_ref = refs[pages:]
    outs = []
    for c in range(0, pages, PAGES_PER_TILE):
        x = jnp.concatenate([refs[p][...] for p in range(c, c + PAGES_PER_TILE)], 1)
        outs.append(_pool_tile(x, w_ref[...], pm_ref[...]))
    o_ref[...] = jnp.concatenate(outs, 1)


def _pool_paged(cache_t, page_table, wrow, pages):
    _, rows, ps = cache_t.shape
    b, n_pages = page_table.shape
    assert ps == PAGE_SIZE and pages % PAGES_PER_TILE == 0 and n_pages % pages == 0
    pm = _pool_matrix(PAGES_PER_TILE * ps)
    out_lanes = pages * ps // CMP_BLOCK
    page_spec = lambda p: pl.BlockSpec((None, rows, ps), lambda i, j, pt: (pt[i, j * pages + p], 0, 0))
    return pl.pallas_call(
        functools.partial(_pool_paged_kernel, pages=pages),
        grid_spec=pltpu.PrefetchScalarGridSpec(
            num_scalar_prefetch=1,
            grid=(b, n_pages // pages),
            in_specs=[page_spec(p) for p in range(pages)]
                     + [pl.BlockSpec(wrow.shape, lambda i, j, pt: (0, 0)),
                        pl.BlockSpec(pm.shape, lambda i, j, pt: (0, 0))],
            out_specs=pl.BlockSpec((None, rows, out_lanes), lambda i, j, pt: (i, 0, j))),
        out_shape=jax.ShapeDtypeStruct((b, rows, n_pages * ps // CMP_BLOCK), F32),
        compiler_params=_params("parallel", "arbitrary"),
        name="pool_paged",
    )(page_table, *([cache_t] * pages), wrow, pm)


def _cmp_probs(sc, pos, tq):
    n = sc.shape[1]
    lane = lax.broadcasted_iota(jnp.int32, (1, n), 1)
    maskc = ((lane + 1) * CMP_BLOCK - 1) <= pos
    ps = []
    imp = None
    for g in range(NSA_GROUP):
        s = jnp.where(maskc, sc[g * tq:(g + 1) * tq], NEG)
        m = jnp.max(s, -1, keepdims=True)
        e = jnp.where(maskc, jnp.exp(s - m), 0.0)
        den = jnp.sum(e, -1, keepdims=True)
        p = e / jnp.where(den > 0.0, den, 1.0)
        ps.append(p)
        imp = p if imp is None else imp + p
    return jnp.concatenate(ps, 0), imp


def _select_blocks(imp, pos, axis):
    n = imp.shape[axis]
    idx = lax.broadcasted_iota(jnp.int32, (n, 1) if axis == 0 else (1, n), axis)
    even = (idx & 1) == 0
    imp2 = imp + jnp.where(even, pltpu.roll(imp, n - 1, axis), pltpu.roll(imp, 1, axis))
    blk = idx >> 1
    valid = blk * SEL_BLOCK <= pos
    cur = pos >> 6
    forced = (blk == 0) | (blk == cur) | (blk == cur - 1)
    score = jnp.where(valid, jnp.where(forced, BIG, imp2), -1.0)
    work = jnp.where(even, score, -2.0)
    idxf = idx.astype(F32)
    sel = jnp.zeros(work.shape, F32)
    for _ in range(SEL_TOPK):
        m = jnp.max(work, axis, keepdims=True)
        first = jnp.min(jnp.where(work == m, idxf, 1e9), axis, keepdims=True)
        pick = idxf == first
        sel = jnp.where(pick, 1.0, sel)
        work = jnp.where(pick, -2.0, work)
    sel = jnp.where(score >= 0.0, sel, 0.0)
    return sel + pltpu.roll(sel, 1, axis)


def _online_update(carry, s, mf, vt):
    m, l, acc = carry
    s = jnp.where(mf > 0.5, s, NEG)
    m_new = jnp.maximum(m, jnp.max(s, -1, keepdims=True))
    alpha = jnp.exp(m - m_new)
    p = jnp.exp(s - m_new) * mf
    l = alpha * l + jnp.sum(p, -1, keepdims=True)
    acc = alpha * acc + _dot_nt(p.astype(BF16), vt)
    return m_new, l, acc


def _expand_matrix(n_blocks, n_keys):
    return (jnp.arange(n_keys)[None, :] // CMP_BLOCK == jnp.arange(n_blocks)[:, None]).astype(BF16)


def _nsa_prompt_kernel(qt_ref, ks_ref, kw_ref, kvb_ref, kvs_ref, kvw_ref, sm_ref, et_ref, o_ref, *, n_rows):
    TQ, TK, G, HD, KV = NSA_TQ, NSA_TK, NSA_GROUP, NSA_HEAD_DIM, NSA_KV_HEADS
    i = pl.program_id(1)
    t0 = i * TQ
    pos = t0 + lax.broadcasted_iota(jnp.int32, (1, TQ), 1)
    gst = _sig(sm_ref[...]).T
    gate_row = lambda br, h, g: 2 * GDN_HEADS + br * NSA_Q_HEADS + h * G + g

    n_wt = WINDOW // TQ + 1
    w_off = [pl.multiple_of(jnp.maximum(i - (n_wt - 1) + j, 0) * TQ, TQ) for j in range(n_wt)]
    cidx = lax.broadcasted_iota(jnp.int32, (WINDOW + TQ, 1), 0)
    diff = lax.broadcasted_iota(jnp.int32, (1, TQ), 1) + WINDOW - cidx
    bw = jnp.where((diff >= 0) & (diff < WINDOW) & (t0 - WINDOW + cidx >= 0), 0.0, NEG)
    kidx = lax.broadcasted_iota(jnp.int32, (TK, 1), 0)
    n_used = (t0 + TQ + TK - 1) // TK

    krow = lambda h: slice(h * HD, (h + 1) * HD)
    vrow = lambda h: slice((KV + h) * HD, (KV + h + 1) * HD)
    zeros = jnp.zeros((HD, TQ), BF16)
    blk = lax.broadcasted_iota(jnp.int32, (kvb_ref.shape[1], 1), 0)
    maskc = ((blk + 1) * CMP_BLOCK - 1) <= pos
    kw_all = jnp.concatenate([kw_ref[pl.ds(o, TQ), :] for o in w_off], 0)
    qbd, oc, ow, selt = [], [], [], []
    for h in range(KV):
        qh = [qt_ref[(h * G + g) * HD:(h * G + g + 1) * HD, :] for g in range(G)]
        qbd.append([jnp.concatenate([zeros] * h + [q] + [zeros] * (KV - 1 - h), 0) for q in qh])

        kc = kvb_ref[krow(h), :].astype(BF16)
        vc = kvb_ref[vrow(h), :].astype(BF16)
        imp = None
        oc.append([])
        for g in range(G):
            s = jnp.where(maskc, _dot_tn(kc, qh[g]), NEG)
            e = jnp.where(maskc, jnp.exp(s - jnp.max(s, 0, keepdims=True)), 0.0)
            den = jnp.sum(e, 0, keepdims=True)
            p = e / jnp.where(den > 0.0, den, 1.0)
            imp = p if imp is None else imp + p
            oc[h].append(_dot(vc, p.astype(BF16)))
        selt.append(_select_blocks(imp[:n_rows], pos, 0).astype(BF16))

        vw = jnp.concatenate([kvw_ref[vrow(h), pl.ds(o, TQ)] for o in w_off], 1).astype(BF16)
        vw = jnp.concatenate([vw, jnp.ones_like(vw)], 0)
        ow.append([])
        for g in range(G):
            s = _dot(kw_all, qbd[h][g]) + bw
            e = jnp.exp(s - jnp.max(s, 0, keepdims=True))
            r = _dot(vw, e.astype(BF16))
            ow[h].append(r[:HD] / r[HD:HD + 1])

    def scores(kt):
        off = pl.multiple_of(kt * TK, TK)
        k_t = ks_ref[pl.ds(off, TK), :]
        e_t = et_ref[pl.ds(off, TK), :]
        causal = off + kidx <= pos
        out = []
        for h in range(KV):
            bias = jnp.where((_dot(e_t, selt[h]) > 0.5) & causal, 0.0, NEG)
            out += [_dot(k_t, qbd[h][g]) + bias for g in range(G)]
        return out

    def body(kt, carry):
        off = pl.multiple_of(kt * TK, TK)
        s_next = scores(jnp.minimum(kt + 1, n_used - 1))
        out = []
        for h in range(KV):
            v_t = kvs_ref[vrow(h), pl.ds(off, TK)].astype(BF16)
            v_t = jnp.concatenate([v_t, jnp.ones_like(v_t)], 0)
            for g in range(G):
                m, acc, s = carry[h * G + g]
                m_new = jnp.maximum(m, jnp.max(s, 0, keepdims=True))
                p = jnp.exp(s - m_new)
                out.append((m_new, jnp.exp(m - m_new) * acc + _dot(v_t, p.astype(BF16)), s_next[h * G + g]))
        return tuple(out)

    s0 = scores(0)
    init = tuple((jnp.full((1, TQ), NEG, F32), jnp.zeros((2 * HD, TQ), F32), s0[j]) for j in range(KV * G))
    res = lax.fori_loop(0, n_used, body, init)

    outs = []
    for h in range(KV):
        for g in range(G):
            acc = res[h * G + g][1]
            gate = lambda br: gst[gate_row(br, h, g):gate_row(br, h, g) + 1, :]
            outs.append(gate(0) * oc[h][g] + gate(1) * (acc[:HD] / acc[HD:HD + 1]) + gate(2) * ow[h][g])
    o_ref[...] = jnp.concatenate(outs, 0).astype(BF16)


def _nsa_prompt(q_t, ks, kw, kvb_t, kvs_t, kvw_t, small):
    b, _, l = q_t.shape
    TQ, TK = NSA_TQ, NSA_TK
    n_cmp = l // CMP_BLOCK
    assert kvb_t.shape[2] == LANES and n_cmp <= LANES and l % TK == 0 and WINDOW % TQ == 0
    n_rows = -(-n_cmp // SUBLANES) * SUBLANES
    emat_t = _expand_matrix(n_rows, l).T
    slab = lambda t: pl.BlockSpec((None,) + t.shape[1:], lambda bi, i: (bi, 0, 0))
    return pl.pallas_call(
        functools.partial(_nsa_prompt_kernel, n_rows=n_rows),
        grid=(b, l // TQ),
        in_specs=[pl.BlockSpec((None, NSA_WIDTH, TQ), lambda bi, i: (bi, 0, i)),
                  slab(ks), slab(kw), slab(kvb_t), slab(kvs_t), slab(kvw_t),
                  pl.BlockSpec((None, TQ, LANES), lambda bi, i: (bi, i, 0)),
                  pl.BlockSpec((l, n_rows), lambda bi, i: (0, 0))],
        out_specs=pl.BlockSpec((None, NSA_WIDTH, TQ), lambda bi, i: (bi, 0, i)),
        out_shape=jax.ShapeDtypeStruct((b, NSA_WIDTH, l), BF16),
        compiler_params=_params("parallel", "arbitrary"),
        name="nsa_prompt",
    )(q_t, ks, kw, kvb_t, kvs_t, kvw_t, small, emat_t)


LS = SUBLANES
ROWS_H = NSA_GROUP * LS


def _row_token(rows):
    return lax.broadcasted_iota(jnp.int32, (rows, 1), 0) & (LS - 1)


def _nsa_sample_a_kernel(q_ref, kvb_ref, cw_ref, new_ref, oc_ref, ow_ref, sel_ref, *, past, l_new):
    HD, KV = NSA_HEAD_DIM, NSA_KV_HEADS
    NB = q_ref.shape[0]
    pos = past + lax.broadcasted_iota(jnp.int32, (LS, 1), 0)
    wb = cw_ref.shape[2]
    tok = _row_token(ROWS_H)
    c1 = lax.broadcasted_iota(jnp.int32, (1, wb), 1)
    d1 = tok + wb - c1
    m1 = jnp.where((d1 >= 0) & (d1 < WINDOW) & (past - wb + c1 >= 0), 1.0, 0.0)
    c2 = lax.broadcasted_iota(jnp.int32, (1, new_ref.shape[2]), 1)
    d2 = tok - c2
    m2 = jnp.where((d2 >= 0) & (d2 < WINDOW) & (c2 < l_new), 1.0, 0.0)
    imps = []
    for n in range(NB):
        for h in range(KV):
            krow = slice(h * HD, (h + 1) * HD)
            vrow = slice((KV + h) * HD, (KV + h + 1) * HD)
            qh = q_ref[n, h]
            p, imp = _cmp_probs(_dot(qh, kvb_ref[n, krow, :].astype(BF16)), pos, LS)
            imps.append(imp)
            oc_ref[n, h] = _dot_nt(p.astype(BF16), kvb_ref[n, vrow, :].astype(BF16))

            s1 = jnp.where(m1 > 0.5, _dot(qh, cw_ref[n, krow, :].astype(BF16)), NEG)
            s2 = jnp.where(m2 > 0.5, _dot(qh, new_ref[n, krow, :].astype(BF16)), NEG)
            m = jnp.maximum(jnp.max(s1, -1, keepdims=True), jnp.max(s2, -1, keepdims=True))
            e1 = jnp.exp(s1 - m) * m1
            e2 = jnp.exp(s2 - m) * m2
            den = jnp.sum(e1, -1, keepdims=True) + jnp.sum(e2, -1, keepdims=True)
            den = jnp.where(den > 0.0, den, 1.0)
            ow_ref[n, h] = (_dot_nt((e1 / den).astype(BF16), cw_ref[n, vrow, :].astype(BF16))
                            + _dot_nt((e2 / den).astype(BF16), new_ref[n, vrow, :].astype(BF16)))
    sel = _select_blocks(jnp.concatenate(imps, 0), jnp.concatenate([pos] * (NB * KV), 0), 1)
    for n in range(NB):
        for h in range(KV):
            sel_ref[n, h] = sel[(n * KV + h) * LS:(n * KV + h + 1) * LS]


def _nsa_sample_a(q_rows, kvb_t, cache_win_t, kvw_new_t, past, l_new):
    b, _, ncp = kvb_t.shape
    nb = SAMPLE_SEQS_PER_STEP if b % SAMPLE_SEQS_PER_STEP == 0 else 1
    per_seq = lambda t: pl.BlockSpec((nb,) + t.shape[1:], lambda i: (i,) + (0,) * (t.ndim - 1))
    out_rows = jax.ShapeDtypeStruct(q_rows.shape, F32)
    sel_shape = jax.ShapeDtypeStruct((b, NSA_KV_HEADS, LS, ncp), F32)
    return pl.pallas_call(
        functools.partial(_nsa_sample_a_kernel, past=past, l_new=l_new),
        grid=(b // nb,),
        in_specs=[per_seq(q_rows), per_seq(kvb_t), per_seq(cache_win_t), per_seq(kvw_new_t)],
        out_specs=[per_seq(out_rows), per_seq(out_rows), per_seq(sel_shape)],
        out_shape=[out_rows, out_rows, sel_shape],
        compiler_params=_params("parallel"),
        name="nsa_sample_a",
    )(q_rows, kvb_t, cache_win_t, kvw_new_t)


def _nsa_sample_sel_kernel(pt_ref, *refs, pages, l_new):
    del pt_ref
    page_refs = refs[:pages]
    q_ref, sel_ref, tail_ref, new_ref, e_ref, oc_ref, ow_ref, gate_ref, o_ref, m_scr, l_scr, acc_scr = refs[pages:]
    HD, KV, G = NSA_HEAD_DIM, NSA_KV_HEADS, NSA_GROUP
    i = pl.program_id(1)
    last = i == pl.num_programs(1) - 1

    @pl.when(i == 0)
    def _():
        m_scr[...] = jnp.full_like(m_scr, NEG)
        l_scr[...] = jnp.zeros_like(l_scr)
        acc_scr[...] = jnp.zeros_like(acc_scr)

    kv = jnp.concatenate([r[...] for r in page_refs], 1)
    emat = e_ref[...]
    for h in range(KV):
        krow = slice(h * HD, (h + 1) * HD)
        vrow = slice((KV + h) * HD, (KV + h + 1) * HD)
        selh = sel_ref[h].astype(BF16)
        mh = jnp.concatenate([_dot(selh[:, c:c + LANES], emat) for c in range(0, selh.shape[1], LANES)], 1)
        carry = _online_update((m_scr[h], l_scr[h], acc_scr[h]),
                               _dot(q_ref[h], kv[krow].astype(BF16)),
                               jnp.concatenate([mh] * G, 0), kv[vrow].astype(BF16))
        m_scr[h], l_scr[h], acc_scr[h] = carry

    @pl.when(last)
    def _():
        tok = _row_token(ROWS_H)
        c2 = lax.broadcasted_iota(jnp.int32, (1, new_ref.shape[1]), 1)
        for h in range(KV):
            krow = slice(h * HD, (h + 1) * HD)
            vrow = slice((KV + h) * HD, (KV + h + 1) * HD)
            flag = jnp.concatenate([tail_ref[h][:, 0:1]] * G, 0)
            mt = jnp.where((c2 <= tok) & (c2 < l_new) & (flag > 0.5), 1.0, 0.0)
            _, l, acc = _online_update((m_scr[h], l_scr[h], acc_scr[h]),
                                       _dot(q_ref[h], new_ref[krow, :].astype(BF16)), mt,
                                       new_ref[vrow, :].astype(BF16))
            osel = acc / jnp.where(l > 0.0, l, 1.0)
            gs = _sig(gate_ref[h])
            o_ref[h] = gs[:, 0:1] * oc_ref[h] + gs[:, 1:2] * osel + gs[:, 2:3] * ow_ref[h]


def _nsa_sample_sel(q_rows, selmask, cache_sel_t, page_table, kvs_new_t, oc, ow, gate_rows, pages, l_new):
    b, n_pages = page_table.shape
    _, rows, ps = cache_sel_t.shape
    lanes_per_step = pages * ps // CMP_BLOCK
    assert lanes_per_step % LANES == 0 and n_pages % pages == 0
    n_steps = n_pages // pages
    emat = _expand_matrix(LANES, LANES * CMP_BLOCK)
    page_spec = lambda p: pl.BlockSpec((None, rows, ps), lambda bi, i, pt: (pt[bi, i * pages + p], 0, 0))
    per_seq = lambda t: pl.BlockSpec((None,) + t.shape[1:], lambda bi, i, pt: (bi,) + (0,) * (t.ndim - 1))
    tail_block = n_steps * lanes_per_step // LANES
    kvh = NSA_KV_HEADS
    return pl.pallas_call(
        functools.partial(_nsa_sample_sel_kernel, pages=pages, l_new=l_new),
        grid_spec=pltpu.PrefetchScalarGridSpec(
            num_scalar_prefetch=1,
            grid=(b, n_steps),
            in_specs=[page_spec(p) for p in range(pages)] + [
                per_seq(q_rows),
                pl.BlockSpec((None, kvh, LS, lanes_per_step), lambda bi, i, pt: (bi, 0, 0, i)),
                pl.BlockSpec((None, kvh, LS, LANES), lambda bi, i, pt: (bi, 0, 0, tail_block)),
                per_seq(kvs_new_t),
                pl.BlockSpec(emat.shape, lambda bi, i, pt: (0, 0)),
                per_seq(oc), per_seq(ow), per_seq(gate_rows)],
            out_specs=per_seq(oc),
            scratch_shapes=[pltpu.VMEM((kvh, ROWS_H, 1), F32), pltpu.VMEM((kvh, ROWS_H, 1), F32),
                            pltpu.VMEM((kvh, ROWS_H, NSA_HEAD_DIM), F32)]),
        out_shape=jax.ShapeDtypeStruct(oc.shape, F32),
        compiler_params=_params("parallel", "arbitrary"),
        name="nsa_sample_sel",
    )(page_table, *([cache_sel_t] * pages), q_rows, selmask, selmask, kvs_new_t, emat, oc, ow, gate_rows)


def _nsa_sample(q_t, kvc_t, kvs_t, kvw_t, gate, cache_cmp_t, cache_sel_t, cache_win_t, page_table, wrow):
    b, _, l = q_t.shape
    KV, G, HD = NSA_KV_HEADS, NSA_GROUP, NSA_HEAD_DIM
    n_pages = page_table.shape[1]
    ps = cache_cmp_t.shape[2]
    past = n_pages * ps
    assert l <= LS and ps == PAGE_SIZE
    pages = min(PAGES_PER_STEP, n_pages)
    pad_new = lambda t: jnp.pad(t, ((0, 0), (0, 0), (0, ps - l)))
    kvb_t = jnp.concatenate([_pool_paged(cache_cmp_t, page_table, wrow, pages), _pool(pad_new(kvc_t), wrow)], 2)
    q_rows = jnp.pad(q_t.reshape(b, KV, G, HD, l).transpose(0, 1, 2, 4, 3),
                     ((0, 0),) * 3 + ((0, LS - l), (0, 0))).reshape(b, KV, ROWS_H, HD)
    oc, ow, selmask = _nsa_sample_a(q_rows, kvb_t, cache_win_t, pad_new(kvw_t), past, l)
    gate_rows = jnp.pad(gate.reshape(b, l, 3, KV, G).transpose(0, 3, 4, 1, 2),
                        ((0, 0),) * 3 + ((0, LS - l), (0, 0))).reshape(b, KV, ROWS_H, 3)
    o = _nsa_sample_sel(q_rows, selmask, cache_sel_t, page_table, pad_new(kvs_t), oc, ow, gate_rows, pages, l)
    o = o.reshape(b, KV, G, LS, HD)[:, :, :, :l]
    return o.transpose(0, 3, 1, 2, 4).reshape(b, l, NSA_WIDTH).astype(BF16)


def _merge_kernel(x_ref, gt_ref, og_ref, on_ref, m_ref, wbg_ref, wbn_ref, wo_ref, lg_ref, lb_ref, o_ref,
                  *, alpha, nsa_feature_major):
    d = x_ref.shape[1]
    m = m_ref[...].astype(F32)
    nsa_dot = _dot_tn if nsa_feature_major else _dot
    mix = m[:, :d] * _dot(og_ref[...], wbg_ref[...]) + m[:, d:] * nsa_dot(on_ref[...], wbn_ref[...])
    y = _dot(mix.astype(BF16), wo_ref[...])
    o_ref[...] = _layer_norm(alpha * x_ref[...] + gt_ref[...] * y, lg_ref[...], lb_ref[...])


def _merge(x, mods, kind, rows_per_seq, o_gdn, o_nsa, msig, wbg, wbn, wo, lg, lb, alpha, tm):
    n, d = x.shape
    tps = max(rows_per_seq // tm, 1)
    row = lambda wd: pl.BlockSpec((tm, wd), lambda i: (i, 0))
    full = lambda a: pl.BlockSpec(a.shape, lambda i: (0, 0))
    feature_major = o_nsa.ndim == 3
    nsa_spec = (pl.BlockSpec((None, o_nsa.shape[1], tm), lambda i: (i // tps, 0, i % tps)) if feature_major
                else row(o_nsa.shape[1]))
    return pl.pallas_call(
        functools.partial(_merge_kernel, alpha=alpha, nsa_feature_major=feature_major),
        grid=(n // tm,),
        in_specs=[row(d), _mod_spec(kind, tm, d, tps, 5), row(o_gdn.shape[1]), nsa_spec,
                  row(2 * d), full(wbg), full(wbn), full(wo), full(lg), full(lb)],
        out_specs=row(d),
        out_shape=jax.ShapeDtypeStruct((n, d), F32),
        compiler_params=_params("parallel"),
        name="merge",
    )(x, mods, o_gdn, o_nsa, msig, wbg, wbn, wo, lg, lb)


def _row_tile(n, pref):
    t = min(pref, n)
    while n % t:
        t //= 2
    return t


def _ff_tile(dff):
    for parts in (4, 2, 1, 11, 22):
        if dff % parts == 0 and (dff // parts) % LANES == 0:
            return dff // parts
    return dff


def _to_rows(kv_t):
    b, _, t = kv_t.shape
    return kv_t.reshape(b, 2, NSA_KV_HEADS, NSA_HEAD_DIM, t).transpose(0, 4, 1, 2, 3)


def _to_feature_major(kv_rows):
    n, t = kv_rows.shape[:2]
    return kv_rows.transpose(0, 2, 3, 4, 1).reshape(n, KV2, t)


def _layer(x, mod, kind, past, lw, alpha):
    b, l, d = x.shape
    n = b * l
    (wg1, wu1, wd1, wg2, wu2, wd2, w_in_parts, wk, wt, conv_w, a_log, dt_bias, norm_w, wrow,
     wbg, wbn, wo, ln_g, ln_b) = lw
    tm = _row_tile(l if kind == "seq" else n, 512)
    tm_in = _row_tile(l if kind == "seq" else n, 256)
    tf = _ff_tile(wg1.shape[1])
    tm_ff = _row_tile(l if kind == "seq" else n, 1024)
    lg = lambda i: ln_g[i].reshape(1, d)
    lb = lambda i: ln_b[i].reshape(1, d)

    x1 = _ffn(x.reshape(n, d), mod, kind, l, 0, wg1, wu1, wd1, lg(0), lb(0), alpha, tm_ff, tf)

    qkv, z, msig, small, ks, kw, q_t, kvc_t, kvs_t, kvw_t = _inproj(x1, mod, kind, l, w_in_parts, wk, wt, tm_in)
    seq = lambda t: t.reshape(b, l, t.shape[-1])
    qkv, z, small, ks, kw = [seq(t) for t in (qkv, z, small, ks, kw)]
    if kind == "tok":
        q_t, kvc_t, kvs_t, kvw_t = [t.reshape(t.shape[0], b, l).transpose(1, 0, 2)
                                    for t in (q_t, kvc_t, kvs_t, kvw_t)]

    if past is None:
        conv_buf = jnp.zeros((b, GDN_CONV - 1, 3 * GDN_WIDTH), F32)
        s0 = jnp.zeros((b, GDN_HEADS, GDN_HEAD_DIM, GDN_HEAD_DIM), F32)
        o_nsa = _nsa_prompt(q_t, ks, kw, _pool(kvc_t, wrow), kvs_t, kvw_t, small)
        win_t = kvw_t[:, :, l - min(WINDOW, l):]
    else:
        s0, conv_buf, cache_cmp, cache_sel, cache_win, page_table = past
        cache_win_t = _to_feature_major(cache_win)
        gate = small[:, :, 2 * GDN_HEADS:2 * GDN_HEADS + 3 * NSA_Q_HEADS]
        o_nsa = _nsa_sample(q_t, kvc_t, kvs_t, kvw_t, gate, _to_feature_major(cache_cmp),
                            _to_feature_major(cache_sel), cache_win_t, page_table, wrow)
        o_nsa = o_nsa.reshape(n, NSA_WIDTH)
        win_t = jnp.concatenate([cache_win_t, kvw_t], 2)[:, :, l:]

    chunk = min(GDN_CHUNK, -(-l // SUBLANES) * SUBLANES)
    lp = -(-l // chunk) * chunk
    padl = lambda t: jnp.pad(t, ((0, 0), (0, lp - l), (0, 0)))
    o_gdn, s_new = _gdn(padl(qkv), padl(z), padl(small), conv_buf, s0, conv_w, a_log, dt_bias, norm_w, l, chunk)
    conv_new = jnp.concatenate([conv_buf, qkv], 1)[:, -(GDN_CONV - 1):]

    x2 = _merge(x1, mod, kind, l, o_gdn[:, :l].reshape(n, GDN_WIDTH), o_nsa, msig,
                wbg, wbn, wo, lg(1), lb(1), alpha, tm)
    x3 = _ffn(x2, mod, kind, l, 6, wg2, wu2, wd2, lg(2), lb(2), alpha, tm_ff, tf)
    return x3.reshape(b, l, d), (s_new, conv_new, _to_rows(kvc_t), _to_rows(kvs_t), _to_rows(win_t))


def _split_w_in(w_in, d):
    splits = (3 * GDN_WIDTH, GDN_WIDTH, GDN_HEADS, GDN_HEADS, NSA_WIDTH, KV2, KV2, KV2, 3 * NSA_Q_HEADS, 2 * d)
    offs = [0]
    for s in splits:
        offs.append(offs[-1] + s)
    qkv, z, a, bb, q, kc, ks, kw, gate, merge = [w_in[:, offs[i]:offs[i + 1]] for i in range(len(splits))]
    n_small = 2 * GDN_HEADS + 3 * NSA_Q_HEADS
    small = jnp.pad(jnp.concatenate([a, bb, gate], 1), ((0, 0), (0, LANES - n_small)))
    wk = jnp.concatenate([ks[:, :KV_WIDTH], kw[:, :KV_WIDTH]], 1)
    wt = jnp.concatenate([q, kc, ks, kw], 1).T
    return [t.astype(BF16) for t in (qkv, z, merge, small)], wk.astype(BF16), wt.astype(BF16)


def kernel(x_prompt, x_sample, c_prompt, c_sample, state_gdn, state_gdn_conv, cache_cmp_kv, cache_sel_kv, cache_win_kv, page_table, ln_g, ln_b, w_ada, b_ada, w_ff1_gu, w_ff1_dn, w_ff2_gu, w_ff2_dn, w_in, gdn_conv_w, gdn_a_log, gdn_dt_bias, gdn_norm_w, nsa_w_cmp, w_br_gdn, w_br_nsa, w_out):
    depth = w_in.shape[0]
    alpha = (2.0 * depth) ** 0.25
    bp, lp, d = x_prompt.shape
    bs, ls, _ = x_sample.shape
    y_p, y_s = x_prompt, x_sample
    p_st, s_st = [], []
    for l in range(depth):
        dff = w_ff1_dn.shape[1]
        bf = lambda t: t.astype(BF16)
        wrow = jnp.tile(jnp.broadcast_to(nsa_w_cmp[l].transpose(0, 2, 1)[:, :, None, :],
                                         (2, NSA_KV_HEADS, NSA_HEAD_DIM, CMP_BLOCK)).reshape(KV2, CMP_BLOCK),
                        (1, LANES // CMP_BLOCK))
        w_in_parts, wk, wt = _split_w_in(w_in[l], d)
        lw = (bf(w_ff1_gu[l][:, :dff]), bf(w_ff1_gu[l][:, dff:]), bf(w_ff1_dn[l]),
              bf(w_ff2_gu[l][:, :dff]), bf(w_ff2_gu[l][:, dff:]), bf(w_ff2_dn[l]),
              w_in_parts, wk, wt, gdn_conv_w[l], gdn_a_log[l], gdn_dt_bias[l], gdn_norm_w[l], wrow,
              bf(w_br_gdn[l]), bf(w_br_nsa[l]), bf(w_out[l]), ln_g[l], ln_b[l])
        c_all = jnp.concatenate([c_prompt, c_sample], 0)
        r = c_all.shape[0]
        rp = -(-r // SUBLANES) * SUBLANES
        mod = _ada(jnp.pad(c_all, ((0, rp - r), (0, 0))), w_ada[l], b_ada[l])
        mod_p = mod[:bp].reshape(bp * 9, 1, d)
        mod_s = jnp.repeat(mod[bp:bp + bs], ls, axis=0)
        y_p, st_p = _layer(y_p, mod_p, "seq", None, lw, alpha)
        past = (state_gdn[l], state_gdn_conv[l], cache_cmp_kv[l], cache_sel_kv[l], cache_win_kv[l], page_table)
        y_s, st_s = _layer(y_s, mod_s, "tok", past, lw, alpha)
        p_st.append(st_p)
        s_st.append(st_s)
    p_out = [jnp.stack(t) for t in zip(*p_st)]
    s_out = [jnp.stack(t) for t in zip(*s_st)]
    return (y_p, y_s, *p_out, *s_out)
```

```python
import functools

import jax
import jax.numpy as jnp
from jax import lax
from jax.experimental import pallas as pl
from jax.experimental.pallas import tpu as pltpu

F32 = jnp.float32
BF16 = jnp.bfloat16
HIGHEST = lax.Precision.HIGHEST

GDN_HEADS = 8
GDN_HEAD_DIM = 64
GDN_WIDTH = GDN_HEADS * GDN_HEAD_DIM
GDN_CONV = 4
GDN_CHUNK = 64
NSA_Q_HEADS = 8
NSA_KV_HEADS = 2
NSA_HEAD_DIM = 64
NSA_GROUP = NSA_Q_HEADS // NSA_KV_HEADS
NSA_WIDTH = NSA_Q_HEADS * NSA_HEAD_DIM
KV_WIDTH = NSA_KV_HEADS * NSA_HEAD_DIM
KV2 = 2 * KV_WIDTH
CMP_BLOCK = 32
SEL_BLOCK = 64
SEL_TOPK = 16
WINDOW = 512
PAGE_SIZE = 128
NEG = -1e30
BIG = 1e4
LN_EPS = 1e-5

SUBLANES = 8
LANES = 128
VMEM_LIMIT = 52 * 1024 * 1024

NSA_TQ = 128
NSA_TK = 256
PAGES_PER_STEP = 64
PAGES_PER_TILE = LANES * CMP_BLOCK // PAGE_SIZE
GDN_SEQS_PER_STEP = 2
SAMPLE_SEQS_PER_STEP = 4


def _sig(x):
    return 1.0 / (1.0 + jnp.exp(-x))


def _softplus(x):
    return jnp.maximum(x, 0.0) + jnp.log(1.0 + jnp.exp(-jnp.abs(x)))


def _layer_norm(r, g, b):
    mu = jnp.mean(r, -1, keepdims=True)
    d = r - mu
    var = jnp.mean(d * d, -1, keepdims=True)
    return d * lax.rsqrt(var + LN_EPS) * g + b


def _dot(a, b):
    return jnp.dot(a, b, preferred_element_type=F32)


def _dot_nt(a, b):
    return lax.dot_general(a, b, (((1,), (1,)), ((), ())), preferred_element_type=F32)


def _dot_tn(a, b):
    return lax.dot_general(a, b, (((0,), (0,)), ((), ())), preferred_element_type=F32)


def _dot_split2(a, sel):
    hi = a.astype(BF16)
    lo = (a - hi.astype(F32)).astype(BF16)
    return _dot(hi, sel) + _dot(lo, sel)


def _params(*sem):
    return pltpu.CompilerParams(dimension_semantics=sem, vmem_limit_bytes=VMEM_LIMIT)


def _mod_spec(kind, tm, d, tiles_per_seq, k):
    if kind == "seq":
        return pl.BlockSpec((None, 1, d), lambda i, *_: ((i // tiles_per_seq) * 9 + k, 0, 0))
    return pl.BlockSpec((tm, d), lambda i, *_: (i, k))


def _ada_kernel(c_ref, w_ref, b_ref, o_ref):
    c = c_ref[...]
    h = (c * _sig(c)).astype(BF16)
    o_ref[...] = _dot(h, w_ref[...].astype(BF16)) + b_ref[...]


def _ada(c, w, b):
    r, d = c.shape
    n = w.shape[1]
    tn = d
    return pl.pallas_call(
        _ada_kernel,
        grid=(n // tn,),
        in_specs=[pl.BlockSpec((r, d), lambda j: (0, 0)),
                  pl.BlockSpec((d, tn), lambda j: (0, j)),
                  pl.BlockSpec((1, tn), lambda j: (0, j))],
        out_specs=pl.BlockSpec((r, tn), lambda j: (0, j)),
        out_shape=jax.ShapeDtypeStruct((r, n), F32),
        compiler_params=_params("arbitrary"),
        name="ada",
    )(c, w, b.reshape(1, n))


def _ffn_kernel(x_ref, sh_ref, sc_ref, gt_ref, wg_ref, wu_ref, wd_ref, lg_ref, lb_ref, o_ref, *, alpha, tf):
    x = x_ref[...]
    h = (x * (1.0 + sc_ref[...]) + sh_ref[...]).astype(BF16)
    acc = None
    for c in range(0, wg_ref.shape[1], tf):
        g = _dot(h, wg_ref[:, c:c + tf])
        u = _dot(h, wu_ref[:, c:c + tf])
        a = (g * _sig(g) * u).astype(BF16)
        part = _dot(a, wd_ref[c:c + tf, :])
        acc = part if acc is None else acc + part
    r = alpha * x + (0.5 * gt_ref[...]) * acc
    o_ref[...] = _layer_norm(r, lg_ref[...], lb_ref[...])


def _ffn(x, mods, kind, rows_per_seq, k0, wg, wu, wd, lg, lb, alpha, tm, tf):
    n, d = x.shape
    tps = max(rows_per_seq // tm, 1)
    ms = lambda k: _mod_spec(kind, tm, d, tps, k)
    full = lambda a: pl.BlockSpec(a.shape, lambda i: (0, 0))
    return pl.pallas_call(
        functools.partial(_ffn_kernel, alpha=alpha, tf=tf),
        grid=(n // tm,),
        in_specs=[pl.BlockSpec((tm, d), lambda i: (i, 0)),
                  ms(k0), ms(k0 + 1), ms(k0 + 2), full(wg), full(wu), full(wd), full(lg), full(lb)],
        out_specs=pl.BlockSpec((tm, d), lambda i: (i, 0)),
        out_shape=jax.ShapeDtypeStruct((n, d), F32),
        compiler_params=_params("parallel"),
        name="ffn",
    )(x, mods, mods, mods, wg, wu, wd, lg, lb)


def _inproj_kernel(x_ref, sh_ref, sc_ref, wqkv, wz, wm, wsm, wk, wt,
                   oqkv, oz, om, osm, oks, okw, oqt, okct, okst, okwt):
    h = (x_ref[...] * (1.0 + sc_ref[...]) + sh_ref[...]).astype(BF16)
    oqkv[...] = _dot(h, wqkv[...])
    oz[...] = _dot(h, wz[...]).astype(BF16)
    om[...] = _sig(_dot(h, wm[...])).astype(BF16)
    osm[...] = _dot(h, wsm[...])
    kt = _dot(h, wk[...]).astype(BF16)
    oks[...] = kt[:, 0:KV_WIDTH]
    okw[...] = kt[:, KV_WIDTH:2 * KV_WIDTH]
    ft = _dot_nt(wt[...], h)
    oqt[...] = (ft[0:NSA_WIDTH] * (NSA_HEAD_DIM ** -0.5)).astype(BF16)
    okct[...] = ft[NSA_WIDTH:NSA_WIDTH + KV2]
    okst[...] = ft[NSA_WIDTH + KV2:NSA_WIDTH + 2 * KV2]
    okwt[...] = ft[NSA_WIDTH + 2 * KV2:NSA_WIDTH + 3 * KV2]


def _inproj(x, mods, kind, rows_per_seq, ws, wk, wt, tm):
    n, d = x.shape
    tps = max(rows_per_seq // tm, 1)
    widths = [w.shape[1] for w in ws] + [KV_WIDTH] * 2
    dtypes = [F32, BF16, BF16, F32, BF16, BF16]
    t_rows = [NSA_WIDTH, KV2, KV2, KV2]
    t_dtypes = [BF16, F32, F32, F32]
    if kind == "seq":
        t_specs = [pl.BlockSpec((None, r, tm), lambda i: (i // tps, 0, i % tps)) for r in t_rows]
        t_shapes = [jax.ShapeDtypeStruct((n // rows_per_seq, r, rows_per_seq), dt) for r, dt in zip(t_rows, t_dtypes)]
    else:
        t_specs = [pl.BlockSpec((r, tm), lambda i: (0, i)) for r in t_rows]
        t_shapes = [jax.ShapeDtypeStruct((r, n), dt) for r, dt in zip(t_rows, t_dtypes)]
    full = lambda w: pl.BlockSpec(w.shape, lambda i: (0, 0))
    return pl.pallas_call(
        _inproj_kernel,
        grid=(n // tm,),
        in_specs=[pl.BlockSpec((tm, d), lambda i: (i, 0)),
                  _mod_spec(kind, tm, d, tps, 3), _mod_spec(kind, tm, d, tps, 4)]
                 + [full(w) for w in ws] + [full(wk), full(wt)],
        out_specs=[pl.BlockSpec((tm, wd), lambda i: (i, 0)) for wd in widths] + t_specs,
        out_shape=[jax.ShapeDtypeStruct((n, wd), dt) for wd, dt in zip(widths, dtypes)] + t_shapes,
        compiler_params=_params("parallel"),
        name="inproj",
    )(x, mods, mods, *ws, wk, wt)


def _gdn_kernel(qkv_ref, z_ref, sm_ref, abt_ref, cb_ref, cw_ref, prow_ref, alog_ref, dtb_ref, nw_ref,
                s0_ref, tri_ref, triu_ref, bd_ref, expg_ref, expb_ref, o_ref, s_ref, xbuf,
                *, chunk, l_valid, l_pad):
    C = chunk
    HD = GDN_HEAD_DIM
    W = GDN_WIDTH
    NB = qkv_ref.shape[0]
    c = pl.program_id(1)

    @pl.when(c == 0)
    def _():
        xbuf[:, 0:SUBLANES, :] = cb_ref[...]
        s_ref[...] = s0_ref[...]

    cw = cw_ref[...]
    bd = bd_ref[...]
    hw = bd.shape[0]
    pr = prow_ref[...]
    qn_l, kn_l, gx_l, g_row_l, q_dec_l, k_dec_l, kb_l, vb_l, kbg_l, e_last_l, z_l = ([] for _ in range(11))
    for n in range(NB):
        x = qkv_ref[n]
        xbuf[n, SUBLANES:SUBLANES + C, :] = x
        y = (xbuf[n, 5:5 + C, :] * cw[0:1] + xbuf[n, 6:6 + C, :] * cw[1:2]
             + xbuf[n, 7:7 + C, :] * cw[2:3] + x * cw[3:4])
        xbuf[n, 0:SUBLANES, :] = x[C - SUBLANES:C, :]
        act = y * _sig(y)
        q = act[:, 0:W]
        k = act[:, W:2 * W]
        v = act[:, 2 * W:3 * W]

        sq = jnp.concatenate([q * q, k * k], 0)
        ss = jnp.concatenate([_dot_split2(sq[:, j:j + hw], bd) for j in range(0, W, hw)], 1)
        qn = q * lax.rsqrt(ss[:C] + 1e-6) * (HD ** -0.5)
        kn = k * lax.rsqrt(ss[C:] + 1e-6)

        sm = sm_ref[n]
        gcol = -jnp.exp(pr[0:1]) * _softplus(sm + pr[1:2])
        bcol = _sig(sm)
        ab = abt_ref[n]
        grow = -jnp.exp(alog_ref[...]) * _softplus(ab[0:GDN_HEADS] + dtb_ref[...])
        if l_pad != l_valid:
            vcol = (c * C + lax.broadcasted_iota(jnp.int32, (C, 1), 0) < l_valid).astype(F32)
            vrow = (c * C + lax.broadcasted_iota(jnp.int32, (1, C), 1) < l_valid).astype(F32)
            qn, kn, v = qn * vcol, kn * vcol, v * vcol
            gcol, bcol, grow = gcol * vcol, bcol * vcol, grow * vrow

        g_cum = jnp.dot(tri_ref[...], gcol, precision=HIGHEST, preferred_element_type=F32)
        gx = _dot_split2(g_cum, expg_ref[...])
        bx = _dot_split2(bcol, expb_ref[...])
        g_row = jnp.dot(grow, triu_ref[...], precision=HIGHEST, preferred_element_type=F32)

        eg = jnp.exp(gx)
        g_last = gx[C - 1:C, :]
        kb = kn * bx
        for lst, val in ((qn_l, qn), (kn_l, kn), (gx_l, gx), (g_row_l, g_row), (q_dec_l, qn * eg),
                         (k_dec_l, kn * jnp.exp(g_last - gx)), (kb_l, kb), (vb_l, v * bx), (kbg_l, kb * eg),
                         (e_last_l, jnp.exp(g_last)), (z_l, z_ref[n].astype(F32))):
            lst.append(val)

    ri = lax.broadcasted_iota(jnp.int32, (C, C), 0)
    ci = lax.broadcasted_iota(jnp.int32, (C, C), 1)
    incl = ri >= ci
    strict = ri > ci
    eye = (ri == ci).astype(F32)
    sh = min(C, SUBLANES).bit_length() - 1
    diag_blk = (ri >> sh) == (ci >> sh)
    merge_masks = []
    while (1 << sh) < C:
        merge_masks.append(((ri >> (sh + 1)) == (ci >> (sh + 1))) & ((ri >> sh) == (ci >> sh) + 1))
        sh += 1

    H = GDN_HEADS
    heads = range(NB * H)
    col = lambda xs, j: xs[j // H][:, (j % H) * HD:(j % H + 1) * HD]
    bfl = lambda xs: [x.astype(BF16) for x in xs]

    dec = [jnp.where(incl, jnp.exp(jnp.where(
        incl, gx_l[j // H][:, (j % H) * HD:(j % H) * HD + C] - g_row_l[j // H][j % H:j % H + 1, :], 0.0)), 0.0)
        for j in heads]
    r = [_dot_nt(jnp.concatenate([col(kb_l, h), col(qn_l, h)], 0).astype(BF16), col(kn_l, h).astype(BF16))
         for h in heads]
    a_kk = [jnp.where(strict, r[h][:C] * dec[h], 0.0) for h in heads]
    a_qk = bfl([r[h][C:] * dec[h] for h in heads])

    nd = [jnp.where(diag_blk, -a_kk[h], 0.0) for h in heads]
    ndb = bfl(nd)
    p2 = [_dot(ndb[h], ndb[h]) for h in heads]
    t = [eye + nd[h] for h in heads]
    r2 = [_dot(p2[h].astype(BF16), jnp.concatenate([t[h], p2[h]], 1).astype(BF16)) for h in heads]
    t = [t[h] + r2[h][:, :C] for h in heads]
    t = [t[h] + _dot(r2[h][:, C:].astype(BF16), t[h].astype(BF16)) for h in heads]
    for mm in merge_masks:
        tb = bfl(t)
        tl = bfl([_dot(tb[h], jnp.where(mm, a_kk[h], 0.0).astype(BF16)) for h in heads])
        t = [t[h] - _dot(tl[h], tb[h]) for h in heads]
    tb = bfl(t)

    rhs = [jnp.concatenate([col(vb_l, h), col(kbg_l, h)], 1) for h in heads]
    x0 = [_dot(tb[h], rhs[h].astype(BF16)) for h in heads]
    res = []
    for h in heads:
        ah = a_kk[h].astype(BF16)
        al = (a_kk[h] - ah.astype(F32)).astype(BF16)
        xh = x0[h].astype(BF16)
        xl = (x0[h] - xh.astype(F32)).astype(BF16)
        hh = _dot(ah, jnp.concatenate([xh, xl], 1))
        res.append(rhs[h] - x0[h] - (hh[:, :2 * HD] + hh[:, 2 * HD:] + _dot(al, xh)))
    uw = [x0[h] + _dot(tb[h], res[h].astype(BF16)) for h in heads]

    s_old = [s_ref[h // H, h % H] for h in heads]
    wq = [_dot(jnp.concatenate([uw[h][:, HD:], col(q_dec_l, h)], 0).astype(BF16), s_old[h].astype(BF16))
          for h in heads]
    v_new = bfl([uw[h][:, :HD] - wq[h][:C] for h in heads])
    o = [wq[h][C:] + _dot(a_qk[h], v_new[h]) for h in heads]
    for h in heads:
        s_ref[h // H, h % H] = (s_old[h] * col(e_last_l, h)
                                + _dot_tn(col(k_dec_l, h).astype(BF16), v_new[h]))

    nw = nw_ref[...]
    outs = []
    for h in heads:
        o_h = o[h] * lax.rsqrt(jnp.mean(o[h] * o[h], -1, keepdims=True) + 1e-6)
        zh = col(z_l, h)
        outs.append(o_h * nw * (zh * _sig(zh)))
    for n in range(NB):
        o_ref[n] = jnp.concatenate(outs[n * H:(n + 1) * H], 1).astype(BF16)


def _gdn(qkv, z, small, conv_buf, s0, conv_w, a_log, dt_bias, norm_w, l_valid, chunk):
    b, lp, w3 = qkv.shape
    C = chunk
    nc = lp // C
    H, HD, W = GDN_HEADS, GDN_HEAD_DIM, GDN_WIDTH
    abt = small[:, :, :2 * H].reshape(b, nc, C, 2 * H).transpose(0, 1, 3, 2)
    cb = jnp.pad(conv_buf, ((0, 0), (SUBLANES - (GDN_CONV - 1), 0), (0, 0)))
    prow = jnp.zeros((2, LANES), F32).at[0, :H].set(a_log).at[1, :H].set(dt_bias)
    alog_r = jnp.broadcast_to(a_log[:, None], (H, C))
    dtb_r = jnp.broadcast_to(dt_bias[:, None], (H, C))
    ix = jnp.arange(C)
    tri = (ix[:, None] >= ix[None, :]).astype(F32)
    triu = tri.T
    hid = jnp.arange(W) // HD
    hw = 2 * LANES
    bd = (hid[:hw, None] == hid[None, :hw]).astype(BF16)
    lane = jnp.arange(LANES)
    expg = (lane[:, None] == hid[None, :]).astype(BF16)
    expb = (lane[:, None] == hid[None, :] + H).astype(BF16)
    const = lambda shape: pl.BlockSpec(shape, lambda i, j: (0,) * len(shape))
    nb = GDN_SEQS_PER_STEP if b % GDN_SEQS_PER_STEP == 0 else 1
    o, s_fin = pl.pallas_call(
        functools.partial(_gdn_kernel, chunk=C, l_valid=l_valid, l_pad=lp),
        grid=(b // nb, nc),
        in_specs=[pl.BlockSpec((nb, C, w3), lambda i, j: (i, j, 0)),
                  pl.BlockSpec((nb, C, W), lambda i, j: (i, j, 0)),
                  pl.BlockSpec((nb, C, LANES), lambda i, j: (i, j, 0)),
                  pl.BlockSpec((nb, None, 2 * H, C), lambda i, j: (i, j, 0, 0)),
                  pl.BlockSpec((nb, SUBLANES, w3), lambda i, j: (i, 0, 0)),
                  const((GDN_CONV, w3)), const((2, LANES)), const((H, C)), const((H, C)), const((1, HD)),
                  pl.BlockSpec((nb, H, HD, HD), lambda i, j: (i, 0, 0, 0)),
                  const((C, C)), const((C, C)), const((hw, hw)), const((LANES, W)), const((LANES, W))],
        out_specs=[pl.BlockSpec((nb, C, W), lambda i, j: (i, j, 0)),
                   pl.BlockSpec((nb, H, HD, HD), lambda i, j: (i, 0, 0, 0))],
        out_shape=[jax.ShapeDtypeStruct((b, lp, W), BF16), jax.ShapeDtypeStruct((b, H, HD, HD), F32)],
        scratch_shapes=[pltpu.VMEM((nb, SUBLANES + C, w3), F32)],
        compiler_params=_params("parallel", "arbitrary"),
        name="gdn",
    )(qkv, z, small, abt, cb, conv_w, prow, alog_r, dtb_r, norm_w.reshape(1, HD), s0,
      tri, triu, bd, expg, expb)
    return o, s_fin


def _pool_tile(x, w, pm):
    t = x.shape[1]
    wt = jnp.concatenate([w] * (t // LANES), 1) if t > LANES else w
    return _dot_split2(x * wt, pm)


def _pool_kernel(x_ref, w_ref, pm_ref, o_ref):
    o_ref[...] = _pool_tile(x_ref[...], w_ref[...], pm_ref[...])


def _pool_matrix(t):
    nb = -(-(t // CMP_BLOCK) // LANES) * LANES
    return (jnp.arange(t)[:, None] // CMP_BLOCK == jnp.arange(nb)[None, :]).astype(BF16)


def _pool(kvt, wrow):
    b, rows, t = kvt.shape
    pm = _pool_matrix(t)
    return pl.pallas_call(
        _pool_kernel,
        grid=(b,),
        in_specs=[pl.BlockSpec((None, rows, t), lambda i: (i, 0, 0)),
                  pl.BlockSpec(wrow.shape, lambda i: (0, 0)),
                  pl.BlockSpec(pm.shape, lambda i: (0, 0))],
        out_specs=pl.BlockSpec((None, rows, pm.shape[1]), lambda i: (i, 0, 0)),
        out_shape=jax.ShapeDtypeStruct((b, rows, pm.shape[1]), F32),
        compiler_params=_params("parallel"),
        name="pool",
    )(kvt, wrow, pm)


def _pool_paged_kernel(pt_ref, *refs, pages):
    del pt_ref
    w_ref, pm_ref, o_ref = refs[pages:]
    outs = []
    for c in range(0, pages, PAGES_PER_TILE):
        x = jnp.concatenate([refs[p][...] for p in range(c, c + PAGES_PER_TILE)], 1)
        outs.append(_pool_tile(x, w_ref[...], pm_ref[...]))
    o_ref[...] = jnp.concatenate(outs, 1)


def _pool_paged(cache_t, page_table, wrow, pages):
    _, rows, ps = cache_t.shape
    b, n_pages = page_table.shape
    assert ps == PAGE_SIZE and pages % PAGES_PER_TILE == 0 and n_pages % pages == 0
    pm = _pool_matrix(PAGES_PER_TILE * ps)
    out_lanes = pages * ps // CMP_BLOCK
    page_spec = lambda p: pl.BlockSpec((None, rows, ps), lambda i, j, pt: (pt[i, j * pages + p], 0, 0))
    return pl.pallas_call(
        functools.partial(_pool_paged_kernel, pages=pages),
        grid_spec=pltpu.PrefetchScalarGridSpec(
            num_scalar_prefetch=1,
            grid=(b, n_pages // pages),
            in_specs=[page_spec(p) for p in range(pages)]
                     + [pl.BlockSpec(wrow.shape, lambda i, j, pt: (0, 0)),
                        pl.BlockSpec(pm.shape, lambda i, j, pt: (0, 0))],
            out_specs=pl.BlockSpec((None, rows, out_lanes), lambda i, j, pt: (i, 0, j))),
        out_shape=jax.ShapeDtypeStruct((b, rows, n_pages * ps // CMP_BLOCK), F32),
        compiler_params=_params("parallel", "arbitrary"),
        name="pool_paged",
    )(page_table, *([cache_t] * pages), wrow, pm)


def _cmp_probs(sc, pos, tq):
    n = sc.shape[1]
    lane = lax.broadcasted_iota(jnp.int32, (1, n), 1)
    maskc = ((lane + 1) * CMP_BLOCK - 1) <= pos
    ps = []
    imp = None
    for g in range(NSA_GROUP):
        s = jnp.where(maskc, sc[g * tq:(g + 1) * tq], NEG)
        m = jnp.max(s, -1, keepdims=True)
        e = jnp.where(maskc, jnp.exp(s - m), 0.0)
        den = jnp.sum(e, -1, keepdims=True)
        p = e / jnp.where(den > 0.0, den, 1.0)
        ps.append(p)
        imp = p if imp is None else imp + p
    return jnp.concatenate(ps, 0), imp


def _select_blocks(imp, pos, axis):
    n = imp.shape[axis]
    idx = lax.broadcasted_iota(jnp.int32, (n, 1) if axis == 0 else (1, n), axis)
    even = (idx & 1) == 0
    imp2 = imp + jnp.where(even, pltpu.roll(imp, n - 1, axis), pltpu.roll(imp, 1, axis))
    blk = idx >> 1
    valid = blk * SEL_BLOCK <= pos
    cur = pos >> 6
    forced = (blk == 0) | (blk == cur) | (blk == cur - 1)
    score = jnp.where(valid, jnp.where(forced, BIG, imp2), -1.0)
    work = jnp.where(even, score, -2.0)
    idxf = idx.astype(F32)
    sel = jnp.zeros(work.shape, F32)
    for _ in range(SEL_TOPK):
        m = jnp.max(work, axis, keepdims=True)
        first = jnp.min(jnp.where(work == m, idxf, 1e9), axis, keepdims=True)
        pick = idxf == first
        sel = jnp.where(pick, 1.0, sel)
        work = jnp.where(pick, -2.0, work)
    sel = jnp.where(score >= 0.0, sel, 0.0)
    return sel + pltpu.roll(sel, 1, axis)


def _online_update(carry, s, mf, vt):
    m, l, acc = carry
    s = jnp.where(mf > 0.5, s, NEG)
    m_new = jnp.maximum(m, jnp.max(s, -1, keepdims=True))
    alpha = jnp.exp(m - m_new)
    p = jnp.exp(s - m_new) * mf
    l = alpha * l + jnp.sum(p, -1, keepdims=True)
    acc = alpha * acc + _dot_nt(p.astype(BF16), vt)
    return m_new, l, acc


def _expand_matrix(n_blocks, n_keys):
    return (jnp.arange(n_keys)[None, :] // CMP_BLOCK == jnp.arange(n_blocks)[:, None]).astype(BF16)


def _nsa_prompt_kernel(qt_ref, ks_ref, kw_ref, kvb_ref, kvs_ref, kvw_ref, sm_ref, et_ref, o_ref, *, n_rows):
    TQ, TK, G, HD, KV = NSA_TQ, NSA_TK, NSA_GROUP, NSA_HEAD_DIM, NSA_KV_HEADS
    i = pl.program_id(1)
    t0 = i * TQ
    pos = t0 + lax.broadcasted_iota(jnp.int32, (1, TQ), 1)
    gst = _sig(sm_ref[...]).T
    gate_row = lambda br, h, g: 2 * GDN_HEADS + br * NSA_Q_HEADS + h * G + g

    n_wt = WINDOW // TQ + 1
    w_off = [pl.multiple_of(jnp.maximum(i - (n_wt - 1) + j, 0) * TQ, TQ) for j in range(n_wt)]
    cidx = lax.broadcasted_iota(jnp.int32, (WINDOW + TQ, 1), 0)
    diff = lax.broadcasted_iota(jnp.int32, (1, TQ), 1) + WINDOW - cidx
    bw = jnp.where((diff >= 0) & (diff < WINDOW) & (t0 - WINDOW + cidx >= 0), 0.0, NEG)
    kidx = lax.broadcasted_iota(jnp.int32, (TK, 1), 0)
    n_used = (t0 + TQ + TK - 1) // TK

    krow = lambda h: slice(h * HD, (h + 1) * HD)
    vrow = lambda h: slice((KV + h) * HD, (KV + h + 1) * HD)
    zeros = jnp.zeros((HD, TQ), BF16)
    blk = lax.broadcasted_iota(jnp.int32, (kvb_ref.shape[1], 1), 0)
    maskc = ((blk + 1) * CMP_BLOCK - 1) <= pos
    kw_all = jnp.concatenate([kw_ref[pl.ds(o, TQ), :] for o in w_off], 0)
    qbd, oc, ow, selt = [], [], [], []
    for h in range(KV):
        qh = [qt_ref[(h * G + g) * HD:(h * G + g + 1) * HD, :] for g in range(G)]
        qbd.append([jnp.concatenate([zeros] * h + [q] + [zeros] * (KV - 1 - h), 0) for q in qh])

        kc = kvb_ref[krow(h), :].astype(BF16)
        vc = kvb_ref[vrow(h), :].astype(BF16)
        imp = None
        oc.append([])
        for g in range(G):
            s = jnp.where(maskc, _dot_tn(kc, qh[g]), NEG)
            e = jnp.where(maskc, jnp.exp(s - jnp.max(s, 0, keepdims=True)), 0.0)
            den = jnp.sum(e, 0, keepdims=True)
            p = e / jnp.where(den > 0.0, den, 1.0)
            imp = p if imp is None else imp + p
            oc[h].append(_dot(vc, p.astype(BF16)))
        selt.append(_select_blocks(imp[:n_rows], pos, 0).astype(BF16))

        vw = jnp.concatenate([kvw_ref[vrow(h), pl.ds(o, TQ)] for o in w_off], 1).astype(BF16)
        vw = jnp.concatenate([vw, jnp.ones_like(vw)], 0)
        ow.append([])
        for g in range(G):
            s = _dot(kw_all, qbd[h][g]) + bw
            e = jnp.exp(s - jnp.max(s, 0, keepdims=True))
            r = _dot(vw, e.astype(BF16))
            ow[h].append(r[:HD] / r[HD:HD + 1])

    def scores(kt):
        off = pl.multiple_of(kt * TK, TK)
        k_t = ks_ref[pl.ds(off, TK), :]
        e_t = et_ref[pl.ds(off, TK), :]
        causal = off + kidx <= pos
        out = []
        for h in range(KV):
            bias = jnp.where((_dot(e_t, selt[h]) > 0.5) & causal, 0.0, NEG)
            out += [_dot(k_t, qbd[h][g]) + bias for g in range(G)]
        return out

    def body(kt, carry):
        off = pl.multiple_of(kt * TK, TK)
        s_next = scores(jnp.minimum(kt + 1, n_used - 1))
        out = []
        for h in range(KV):
            v_t = kvs_ref[vrow(h), pl.ds(off, TK)].astype(BF16)
            v_t = jnp.concatenate([v_t, jnp.ones_like(v_t)], 0)
            for g in range(G):
                m, acc, s = carry[h * G + g]
                m_new = jnp.maximum(m, jnp.max(s, 0, keepdims=True))
                p = jnp.exp(s - m_new)
                out.append((m_new, jnp.exp(m - m_new) * acc + _dot(v_t, p.astype(BF16)), s_next[h * G + g]))
        return tuple(out)

    s0 = scores(0)
    init = tuple((jnp.full((1, TQ), NEG, F32), jnp.zeros((2 * HD, TQ), F32), s0[j]) for j in range(KV * G))
    res = lax.fori_loop(0, n_used, body, init)

    outs = []
    for h in range(KV):
        for g in range(G):
            acc = res[h * G + g][1]
            gate = lambda br: gst[gate_row(br, h, g):gate_row(br, h, g) + 1, :]
            outs.append(gate(0) * oc[h][g] + gate(1) * (acc[:HD] / acc[HD:HD + 1]) + gate(2) * ow[h][g])
    o_ref[...] = jnp.concatenate(outs, 0).astype(BF16)


def _nsa_prompt(q_t, ks, kw, kvb_t, kvs_t, kvw_t, small):
    b, _, l = q_t.shape
    TQ, TK = NSA_TQ, NSA_TK
    n_cmp = l // CMP_BLOCK
    assert kvb_t.shape[2] == LANES and n_cmp <= LANES and l % TK == 0 and WINDOW % TQ == 0
    n_rows = -(-n_cmp // SUBLANES) * SUBLANES
    emat_t = _expand_matrix(n_rows, l).T
    slab = lambda t: pl.BlockSpec((None,) + t.shape[1:], lambda bi, i: (bi, 0, 0))
    return pl.pallas_call(
        functools.partial(_nsa_prompt_kernel, n_rows=n_rows),
        grid=(b, l // TQ),
        in_specs=[pl.BlockSpec((None, NSA_WIDTH, TQ), lambda bi, i: (bi, 0, i)),
                  slab(ks), slab(kw), slab(kvb_t), slab(kvs_t), slab(kvw_t),
                  pl.BlockSpec((None, TQ, LANES), lambda bi, i: (bi, i, 0)),
                  pl.BlockSpec((l, n_rows), lambda bi, i: (0, 0))],
        out_specs=pl.BlockSpec((None, NSA_WIDTH, TQ), lambda bi, i: (bi, 0, i)),
        out_shape=jax.ShapeDtypeStruct((b, NSA_WIDTH, l), BF16),
        compiler_params=_params("parallel", "arbitrary"),
        name="nsa_prompt",
    )(q_t, ks, kw, kvb_t, kvs_t, kvw_t, small, emat_t)


LS = SUBLANES
ROWS_H = NSA_GROUP * LS


def _row_token(rows):
    return lax.broadcasted_iota(jnp.int32, (rows, 1), 0) & (LS - 1)


def _nsa_sample_a_kernel(q_ref, kvb_ref, cw_ref, new_ref, oc_ref, ow_ref, sel_ref, *, past, l_new):
    HD, KV = NSA_HEAD_DIM, NSA_KV_HEADS
    NB = q_ref.shape[0]
    pos = past + lax.broadcasted_iota(jnp.int32, (LS, 1), 0)
    wb = cw_ref.shape[2]
    tok = _row_token(ROWS_H)
    c1 = lax.broadcasted_iota(jnp.int32, (1, wb), 1)
    d1 = tok + wb - c1
    m1 = jnp.where((d1 >= 0) & (d1 < WINDOW) & (past - wb + c1 >= 0), 1.0, 0.0)
    c2 = lax.broadcasted_iota(jnp.int32, (1, new_ref.shape[2]), 1)
    d2 = tok - c2
    m2 = jnp.where((d2 >= 0) & (d2 < WINDOW) & (c2 < l_new), 1.0, 0.0)
    imps = []
    for n in range(NB):
        for h in range(KV):
            krow = slice(h * HD, (h + 1) * HD)
            vrow = slice((KV + h) * HD, (KV + h + 1) * HD)
            qh = q_ref[n, h]
            p, imp = _cmp_probs(_dot(qh, kvb_ref[n, krow, :].astype(BF16)), pos, LS)
            imps.append(imp)
            oc_ref[n, h] = _dot_nt(p.astype(BF16), kvb_ref[n, vrow, :].astype(BF16))

            s1 = jnp.where(m1 > 0.5, _dot(qh, cw_ref[n, krow, :].astype(BF16)), NEG)
            s2 = jnp.where(m2 > 0.5, _dot(qh, new_ref[n, krow, :].astype(BF16)), NEG)
            m = jnp.maximum(jnp.max(s1, -1, keepdims=True), jnp.max(s2, -1, keepdims=True))
            e1 = jnp.exp(s1 - m) * m1
            e2 = jnp.exp(s2 - m) * m2
            den = jnp.sum(e1, -1, keepdims=True) + jnp.sum(e2, -1, keepdims=True)
            den = jnp.where(den > 0.0, den, 1.0)
            ow_ref[n, h] = (_dot_nt((e1 / den).astype(BF16), cw_ref[n, vrow, :].astype(BF16))
                            + _dot_nt((e2 / den).astype(BF16), new_ref[n, vrow, :].astype(BF16)))
    sel = _select_blocks(jnp.concatenate(imps, 0), jnp.concatenate([pos] * (NB * KV), 0), 1)
    for n in range(NB):
        for h in range(KV):
            sel_ref[n, h] = sel[(n * KV + h) * LS:(n * KV + h + 1) * LS]


def _nsa_sample_a(q_rows, kvb_t, cache_win_t, kvw_new_t, past, l_new):
    b, _, ncp = kvb_t.shape
    nb = SAMPLE_SEQS_PER_STEP if b % SAMPLE_SEQS_PER_STEP == 0 else 1
    per_seq = lambda t: pl.BlockSpec((nb,) + t.shape[1:], lambda i: (i,) + (0,) * (t.ndim - 1))
    out_rows = jax.ShapeDtypeStruct(q_rows.shape, F32)
    sel_shape = jax.ShapeDtypeStruct((b, NSA_KV_HEADS, LS, ncp), F32)
    return pl.pallas_call(
        functools.partial(_nsa_sample_a_kernel, past=past, l_new=l_new),
        grid=(b // nb,),
        in_specs=[per_seq(q_rows), per_seq(kvb_t), per_seq(cache_win_t), per_seq(kvw_new_t)],
        out_specs=[per_seq(out_rows), per_seq(out_rows), per_seq(sel_shape)],
        out_shape=[out_rows, out_rows, sel_shape],
        compiler_params=_params("parallel"),
        name="nsa_sample_a",
    )(q_rows, kvb_t, cache_win_t, kvw_new_t)


def _nsa_sample_sel_kernel(pt_ref, *refs, pages, l_new):
    del pt_ref
    page_refs = refs[:pages]
    q_ref, sel_ref, tail_ref, new_ref, e_ref, oc_ref, ow_ref, gate_ref, o_ref, m_scr, l_scr, acc_scr = refs[pages:]
    HD, KV, G = NSA_HEAD_DIM, NSA_KV_HEADS, NSA_GROUP
    i = pl.program_id(1)
    last = i == pl.num_programs(1) - 1

    @pl.when(i == 0)
    def _():
        m_scr[...] = jnp.full_like(m_scr, NEG)
        l_scr[...] = jnp.zeros_like(l_scr)
        acc_scr[...] = jnp.zeros_like(acc_scr)

    kv = jnp.concatenate([r[...] for r in page_refs], 1)
    emat = e_ref[...]
    for h in range(KV):
        krow = slice(h * HD, (h + 1) * HD)
        vrow = slice((KV + h) * HD, (KV + h + 1) * HD)
        selh = sel_ref[h].astype(BF16)
        mh = jnp.concatenate([_dot(selh[:, c:c + LANES], emat) for c in range(0, selh.shape[1], LANES)], 1)
        carry = _online_update((m_scr[h], l_scr[h], acc_scr[h]),
                               _dot(q_ref[h], kv[krow].astype(BF16)),
                               jnp.concatenate([mh] * G, 0), kv[vrow].astype(BF16))
        m_scr[h], l_scr[h], acc_scr[h] = carry

    @pl.when(last)
    def _():
        tok = _row_token(ROWS_H)
        c2 = lax.broadcasted_iota(jnp.int32, (1, new_ref.shape[1]), 1)
        for h in range(KV):
            krow = slice(h * HD, (h + 1) * HD)
            vrow = slice((KV + h) * HD, (KV + h + 1) * HD)
            flag = jnp.concatenate([tail_ref[h][:, 0:1]] * G, 0)
            mt = jnp.where((c2 <= tok) & (c2 < l_new) & (flag > 0.5), 1.0, 0.0)
            _, l, acc = _online_update((m_scr[h], l_scr[h], acc_scr[h]),
                                       _dot(q_ref[h], new_ref[krow, :].astype(BF16)), mt,
                                       new_ref[vrow, :].astype(BF16))
            osel = acc / jnp.where(l > 0.0, l, 1.0)
            gs = _sig(gate_ref[h])
            o_ref[h] = gs[:, 0:1] * oc_ref[h] + gs[:, 1:2] * osel + gs[:, 2:3] * ow_ref[h]


def _nsa_sample_sel(q_rows, selmask, cache_sel_t, page_table, kvs_new_t, oc, ow, gate_rows, pages, l_new):
    b, n_pages = page_table.shape
    _, rows, ps = cache_sel_t.shape
    lanes_per_step = pages * ps // CMP_BLOCK
    assert lanes_per_step % LANES == 0 and n_pages % pages == 0
    n_steps = n_pages // pages
    emat = _expand_matrix(LANES, LANES * CMP_BLOCK)
    page_spec = lambda p: pl.BlockSpec((None, rows, ps), lambda bi, i, pt: (pt[bi, i * pages + p], 0, 0))
    per_seq = lambda t: pl.BlockSpec((None,) + t.shape[1:], lambda bi, i, pt: (bi,) + (0,) * (t.ndim - 1))
    tail_block = n_steps * lanes_per_step // LANES
    kvh = NSA_KV_HEADS
    return pl.pallas_call(
        functools.partial(_nsa_sample_sel_kernel, pages=pages, l_new=l_new),
        grid_spec=pltpu.PrefetchScalarGridSpec(
            num_scalar_prefetch=1,
            grid=(b, n_steps),
            in_specs=[page_spec(p) for p in range(pages)] + [
                per_seq(q_rows),
                pl.BlockSpec((None, kvh, LS, lanes_per_step), lambda bi, i, pt: (bi, 0, 0, i)),
                pl.BlockSpec((None, kvh, LS, LANES), lambda bi, i, pt: (bi, 0, 0, tail_block)),
                per_seq(kvs_new_t),
                pl.BlockSpec(emat.shape, lambda bi, i, pt: (0, 0)),
                per_seq(oc), per_seq(ow), per_seq(gate_rows)],
            out_specs=per_seq(oc),
            scratch_shapes=[pltpu.VMEM((kvh, ROWS_H, 1), F32), pltpu.VMEM((kvh, ROWS_H, 1), F32),
                            pltpu.VMEM((kvh, ROWS_H, NSA_HEAD_DIM), F32)]),
        out_shape=jax.ShapeDtypeStruct(oc.shape, F32),
        compiler_params=_params("parallel", "arbitrary"),
        name="nsa_sample_sel",
    )(page_table, *([cache_sel_t] * pages), q_rows, selmask, selmask, kvs_new_t, emat, oc, ow, gate_rows)


def _nsa_sample(q_t, kvc_t, kvs_t, kvw_t, gate, cache_cmp_t, cache_sel_t, cache_win_t, page_table, wrow):
    b, _, l = q_t.shape
    KV, G, HD = NSA_KV_HEADS, NSA_GROUP, NSA_HEAD_DIM
    n_pages = page_table.shape[1]
    ps = cache_cmp_t.shape[2]
    past = n_pages * ps
    assert l <= LS and ps == PAGE_SIZE
    pages = min(PAGES_PER_STEP, n_pages)
    pad_new = lambda t: jnp.pad(t, ((0, 0), (0, 0), (0, ps - l)))
    kvb_t = jnp.concatenate([_pool_paged(cache_cmp_t, page_table, wrow, pages), _pool(pad_new(kvc_t), wrow)], 2)
    q_rows = jnp.pad(q_t.reshape(b, KV, G, HD, l).transpose(0, 1, 2, 4, 3),
                     ((0, 0),) * 3 + ((0, LS - l), (0, 0))).reshape(b, KV, ROWS_H, HD)
    oc, ow, selmask = _nsa_sample_a(q_rows, kvb_t, cache_win_t, pad_new(kvw_t), past, l)
    gate_rows = jnp.pad(gate.reshape(b, l, 3, KV, G).transpose(0, 3, 4, 1, 2),
                        ((0, 0),) * 3 + ((0, LS - l), (0, 0))).reshape(b, KV, ROWS_H, 3)
    o = _nsa_sample_sel(q_rows, selmask, cache_sel_t, page_table, pad_new(kvs_t), oc, ow, gate_rows, pages, l)
    o = o.reshape(b, KV, G, LS, HD)[:, :, :, :l]
    return o.transpose(0, 3, 1, 2, 4).reshape(b, l, NSA_WIDTH).astype(BF16)


def _merge_kernel(x_ref, gt_ref, og_ref, on_ref, m_ref, wbg_ref, wbn_ref, wo_ref, lg_ref, lb_ref, o_ref,
                  *, alpha, nsa_feature_major):
    d = x_ref.shape[1]
    m = m_ref[...].astype(F32)
    nsa_dot = _dot_tn if nsa_feature_major else _dot
    mix = m[:, :d] * _dot(og_ref[...], wbg_ref[...]) + m[:, d:] * nsa_dot(on_ref[...], wbn_ref[...])
    y = _dot(mix.astype(BF16), wo_ref[...])
    o_ref[...] = _layer_norm(alpha * x_ref[...] + gt_ref[...] * y, lg_ref[...], lb_ref[...])


def _merge(x, mods, kind, rows_per_seq, o_gdn, o_nsa, msig, wbg, wbn, wo, lg, lb, alpha, tm):
    n, d = x.shape
    tps = max(rows_per_seq // tm, 1)
    row = lambda wd: pl.BlockSpec((tm, wd), lambda i: (i, 0))
    full = lambda a: pl.BlockSpec(a.shape, lambda i: (0, 0))
    feature_major = o_nsa.ndim == 3
    nsa_spec = (pl.BlockSpec((None, o_nsa.shape[1], tm), lambda i: (i // tps, 0, i % tps)) if feature_major
                else row(o_nsa.shape[1]))
    return pl.pallas_call(
        functools.partial(_merge_kernel, alpha=alpha, nsa_feature_major=feature_major),
        grid=(n // tm,),
        in_specs=[row(d), _mod_spec(kind, tm, d, tps, 5), row(o_gdn.shape[1]), nsa_spec,
                  row(2 * d), full(wbg), full(wbn), full(wo), full(lg), full(lb)],
        out_specs=row(d),
        out_shape=jax.ShapeDtypeStruct((n, d), F32),
        compiler_params=_params("parallel"),
        name="merge",
    )(x, mods, o_gdn, o_nsa, msig, wbg, wbn, wo, lg, lb)


def _row_tile(n, pref):
    t = min(pref, n)
    while n % t:
        t //= 2
    return t


def _ff_tile(dff):
    for parts in (11, 4, 2, 1, 22):
        if dff % parts == 0 and (dff // parts) % LANES == 0:
            return dff // parts
    return dff


def _to_rows(kv_t):
    b, _, t = kv_t.shape
    return kv_t.reshape(b, 2, NSA_KV_HEADS, NSA_HEAD_DIM, t).transpose(0, 4, 1, 2, 3)


def _to_feature_major(kv_rows):
    n, t = kv_rows.shape[:2]
    return kv_rows.transpose(0, 2, 3, 4, 1).reshape(n, KV2, t)


def _layer(x, mod, kind, past, lw, alpha):
    b, l, d = x.shape
    n = b * l
    (wg1, wu1, wd1, wg2, wu2, wd2, w_in_parts, wk, wt, conv_w, a_log, dt_bias, norm_w, wrow,
     wbg, wbn, wo, ln_g, ln_b) = lw
    tm = _row_tile(l if kind == "seq" else n, 512)
    tm_in = _row_tile(l if kind == "seq" else n, 256)
    tf = _ff_tile(wg1.shape[1])
    tm_ff = _row_tile(l if kind == "seq" else n, 512)
    lg = lambda i: ln_g[i].reshape(1, d)
    lb = lambda i: ln_b[i].reshape(1, d)

    x1 = _ffn(x.reshape(n, d), mod, kind, l, 0, wg1, wu1, wd1, lg(0), lb(0), alpha, tm_ff, tf)

    qkv, z, msig, small, ks, kw, q_t, kvc_t, kvs_t, kvw_t = _inproj(x1, mod, kind, l, w_in_parts, wk, wt, tm_in)
    seq = lambda t: t.reshape(b, l, t.shape[-1])
    qkv, z, small, ks, kw = [seq(t) for t in (qkv, z, small, ks, kw)]
    if kind == "tok":
        q_t, kvc_t, kvs_t, kvw_t = [t.reshape(t.shape[0], b, l).transpose(1, 0, 2)
                                    for t in (q_t, kvc_t, kvs_t, kvw_t)]

    if past is None:
        conv_buf = jnp.zeros((b, GDN_CONV - 1, 3 * GDN_WIDTH), F32)
        s0 = jnp.zeros((b, GDN_HEADS, GDN_HEAD_DIM, GDN_HEAD_DIM), F32)
        o_nsa = _nsa_prompt(q_t, ks, kw, _pool(kvc_t, wrow), kvs_t, kvw_t, small)
        win_t = kvw_t[:, :, l - min(WINDOW, l):]
    else:
        s0, conv_buf, cache_cmp, cache_sel, cache_win, page_table = past
        cache_win_t = _to_feature_major(cache_win)
        gate = small[:, :, 2 * GDN_HEADS:2 * GDN_HEADS + 3 * NSA_Q_HEADS]
        o_nsa = _nsa_sample(q_t, kvc_t, kvs_t, kvw_t, gate, _to_feature_major(cache_cmp),
                            _to_feature_major(cache_sel), cache_win_t, page_table, wrow)
        o_nsa = o_nsa.reshape(n, NSA_WIDTH)
        win_t = jnp.concatenate([cache_win_t, kvw_t], 2)[:, :, l:]

    chunk = min(GDN_CHUNK, -(-l // SUBLANES) * SUBLANES)
    lp = -(-l // chunk) * chunk
    padl = lambda t: jnp.pad(t, ((0, 0), (0, lp - l), (0, 0)))
    o_gdn, s_new = _gdn(padl(qkv), padl(z), padl(small), conv_buf, s0, conv_w, a_log, dt_bias, norm_w, l, chunk)
    conv_new = jnp.concatenate([conv_buf, qkv], 1)[:, -(GDN_CONV - 1):]

    x2 = _merge(x1, mod, kind, l, o_gdn[:, :l].reshape(n, GDN_WIDTH), o_nsa, msig,
                wbg, wbn, wo, lg(1), lb(1), alpha, tm)
    x3 = _ffn(x2, mod, kind, l, 6, wg2, wu2, wd2, lg(2), lb(2), alpha, tm_ff, tf)
    return x3.reshape(b, l, d), (s_new, conv_new, _to_rows(kvc_t), _to_rows(kvs_t), _to_rows(win_t))


def _split_w_in(w_in, d):
    splits = (3 * GDN_WIDTH, GDN_WIDTH, GDN_HEADS, GDN_HEADS, NSA_WIDTH, KV2, KV2, KV2, 3 * NSA_Q_HEADS, 2 * d)
    offs = [0]
    for s in splits:
        offs.append(offs[-1] + s)
    qkv, z, a, bb, q, kc, ks, kw, gate, merge = [w_in[:, offs[i]:offs[i + 1]] for i in range(len(splits))]
    n_small = 2 * GDN_HEADS + 3 * NSA_Q_HEADS
    small = jnp.pad(jnp.concatenate([a, bb, gate], 1), ((0, 0), (0, LANES - n_small)))
    wk = jnp.concatenate([ks[:, :KV_WIDTH], kw[:, :KV_WIDTH]], 1)
    wt = jnp.concatenate([q, kc, ks, kw], 1).T
    return [t.astype(BF16) for t in (qkv, z, merge, small)], wk.astype(BF16), wt.astype(BF16)


def kernel(x_prompt, x_sample, c_prompt, c_sample, state_gdn, state_gdn_conv, cache_cmp_kv, cache_sel_kv, cache_win_kv, page_table, ln_g, ln_b, w_ada, b_ada, w_ff1_gu, w_ff1_dn, w_ff2_gu, w_ff2_dn, w_in, gdn_conv_w, gdn_a_log, gdn_dt_bias, gdn_norm_w, nsa_w_cmp, w_br_gdn, w_br_nsa, w_out):
    depth = w_in.shape[0]
    alpha = (2.0 * depth) ** 0.25
    bp, lp, d = x_prompt.shape
    bs, ls, _ = x_sample.shape
    y_p, y_s = x_prompt, x_sample
    p_st, s_st = [], []
    for l in range(depth):
        dff = w_ff1_dn.shape[1]
        bf = lambda t: t.astype(BF16)
        wrow = jnp.tile(jnp.broadcast_to(nsa_w_cmp[l].transpose(0, 2, 1)[:, :, None, :],
                                         (2, NSA_KV_HEADS, NSA_HEAD_DIM, CMP_BLOCK)).reshape(KV2, CMP_BLOCK),
                        (1, LANES // CMP_BLOCK))
        w_in_parts, wk, wt = _split_w_in(w_in[l], d)
        lw = (bf(w_ff1_gu[l][:, :dff]), bf(w_ff1_gu[l][:, dff:]), bf(w_ff1_dn[l]),
              bf(w_ff2_gu[l][:, :dff]), bf(w_ff2_gu[l][:, dff:]), bf(w_ff2_dn[l]),
              w_in_parts, wk, wt, gdn_conv_w[l], gdn_a_log[l], gdn_dt_bias[l], gdn_norm_w[l], wrow,
              bf(w_br_gdn[l]), bf(w_br_nsa[l]), bf(w_out[l]), ln_g[l], ln_b[l])
        c_all = jnp.concatenate([c_prompt, c_sample], 0)
        r = c_all.shape[0]
        rp = -(-r // SUBLANES) * SUBLANES
        mod = _ada(jnp.pad(c_all, ((0, rp - r), (0, 0))), w_ada[l], b_ada[l])
        mod_p = mod[:bp].reshape(bp * 9, 1, d)
        mod_s = jnp.repeat(mod[bp:bp + bs], ls, axis=0)
        y_p, st_p = _layer(y_p, mod_p, "seq", None, lw, alpha)
        past = (state_gdn[l], state_gdn_conv[l], cache_cmp_kv[l], cache_sel_kv[l], cache_win_kv[l], page_table)
        y_s, st_s = _layer(y_s, mod_s, "tok", past, lw, alpha)
        p_st.append(st_p)
        s_st.append(st_s)
    p_out = [jnp.stack(t) for t in zip(*p_st)]
    s_out = [jnp.stack(t) for t in zip(*s_st)]
    return (y_p, y_s, *p_out, *s_out)
```

```python
import functools

import jax
import jax.numpy as jnp
from jax import lax
from jax.experimental import pallas as pl
from jax.experimental.pallas import tpu as pltpu

F32 = jnp.float32
BF16 = jnp.bfloat16
HIGHEST = lax.Precision.HIGHEST

GDN_HEADS = 8
GDN_HEAD_DIM = 64
GDN_WIDTH = GDN_HEADS * GDN_HEAD_DIM
GDN_CONV = 4
GDN_CHUNK = 64
NSA_Q_HEADS = 8
NSA_KV_HEADS = 2
NSA_HEAD_DIM = 64
NSA_GROUP = NSA_Q_HEADS // NSA_KV_HEADS
NSA_WIDTH = NSA_Q_HEADS * NSA_HEAD_DIM
KV_WIDTH = NSA_KV_HEADS * NSA_HEAD_DIM
KV2 = 2 * KV_WIDTH
CMP_BLOCK = 32
SEL_BLOCK = 64
SEL_TOPK = 16
WINDOW = 512
PAGE_SIZE = 128
NEG = -1e30
BIG = 1e4
LN_EPS = 1e-5

SUBLANES = 8
LANES = 128
VMEM_LIMIT = 52 * 1024 * 1024

NSA_TQ = 128
NSA_TK = 256
PAGES_PER_STEP = 64
PAGES_PER_TILE = LANES * CMP_BLOCK // PAGE_SIZE
GDN_SEQS_PER_STEP = 4
SAMPLE_SEQS_PER_STEP = 8


def _sig(x):
    return 1.0 / (1.0 + jnp.exp(-x))


def _softplus(x):
    return jnp.maximum(x, 0.0) + jnp.log(1.0 + jnp.exp(-jnp.abs(x)))


def _layer_norm(r, g, b):
    mu = jnp.mean(r, -1, keepdims=True)
    d = r - mu
    var = jnp.mean(d * d, -1, keepdims=True)
    return d * lax.rsqrt(var + LN_EPS) * g + b


def _dot(a, b):
    return jnp.dot(a, b, preferred_element_type=F32)


def _dot_nt(a, b):
    return lax.dot_general(a, b, (((1,), (1,)), ((), ())), preferred_element_type=F32)


def _dot_tn(a, b):
    return lax.dot_general(a, b, (((0,), (0,)), ((), ())), preferred_element_type=F32)


def _dot_split2(a, sel):
    hi = a.astype(BF16)
    lo = (a - hi.astype(F32)).astype(BF16)
    return _dot(hi, sel) + _dot(lo, sel)


def _params(*sem):
    return pltpu.CompilerParams(dimension_semantics=sem, vmem_limit_bytes=VMEM_LIMIT)


def _mod_spec(kind, tm, d, tiles_per_seq, k):
    if kind == "seq":
        return pl.BlockSpec((None, 1, d), lambda i, *_: ((i // tiles_per_seq) * 9 + k, 0, 0))
    return pl.BlockSpec((tm, d), lambda i, *_: (i, k))


def _ada_kernel(c_ref, w_ref, b_ref, o_ref):
    c = c_ref[...]
    h = (c * _sig(c)).astype(BF16)
    o_ref[...] = _dot(h, w_ref[...].astype(BF16)) + b_ref[...]


def _ada(c, w, b):
    r, d = c.shape
    n = w.shape[1]
    tn = d
    return pl.pallas_call(
        _ada_kernel,
        grid=(n // tn,),
        in_specs=[pl.BlockSpec((r, d), lambda j: (0, 0)),
                  pl.BlockSpec((d, tn), lambda j: (0, j)),
                  pl.BlockSpec((1, tn), lambda j: (0, j))],
        out_specs=pl.BlockSpec((r, tn), lambda j: (0, j)),
        out_shape=jax.ShapeDtypeStruct((r, n), F32),
        compiler_params=_params("arbitrary"),
        name="ada",
    )(c, w, b.reshape(1, n))


def _ffn_kernel(x_ref, sh_ref, sc_ref, gt_ref, wg_ref, wu_ref, wd_ref, lg_ref, lb_ref, o_ref, *, alpha, tf):
    x = x_ref[...]
    h = (x * (1.0 + sc_ref[...]) + sh_ref[...]).astype(BF16)
    acc = None
    for c in range(0, wg_ref.shape[1], tf):
        g = _dot(h, wg_ref[:, c:c + tf])
        u = _dot(h, wu_ref[:, c:c + tf])
        a = (g * _sig(g) * u).astype(BF16)
        part = _dot(a, wd_ref[c:c + tf, :])
        acc = part if acc is None else acc + part
    r = alpha * x + (0.5 * gt_ref[...]) * acc
    o_ref[...] = _layer_norm(r, lg_ref[...], lb_ref[...])


def _ffn(x, mods, kind, rows_per_seq, k0, wg, wu, wd, lg, lb, alpha, tm, tf):
    n, d = x.shape
    tps = max(rows_per_seq // tm, 1)
    ms = lambda k: _mod_spec(kind, tm, d, tps, k)
    full = lambda a: pl.BlockSpec(a.shape, lambda i: (0, 0))
    return pl.pallas_call(
        functools.partial(_ffn_kernel, alpha=alpha, tf=tf),
        grid=(n // tm,),
        in_specs=[pl.BlockSpec((tm, d), lambda i: (i, 0)),
                  ms(k0), ms(k0 + 1), ms(k0 + 2), full(wg), full(wu), full(wd), full(lg), full(lb)],
        out_specs=pl.BlockSpec((tm, d), lambda i: (i, 0)),
        out_shape=jax.ShapeDtypeStruct((n, d), F32),
        compiler_params=_params("parallel"),
        name="ffn",
    )(x, mods, mods, mods, wg, wu, wd, lg, lb)


def _inproj_kernel(x_ref, sh_ref, sc_ref, wqkv, wz, wm, wsm, wk, wt,
                   oqkv, oz, om, osm, oks, okw, oqt, okct, okst, okwt):
    h = (x_ref[...] * (1.0 + sc_ref[...]) + sh_ref[...]).astype(BF16)
    oqkv[...] = _dot(h, wqkv[...])
    oz[...] = _dot(h, wz[...]).astype(BF16)
    om[...] = _sig(_dot(h, wm[...])).astype(BF16)
    osm[...] = _dot(h, wsm[...])
    kt = _dot(h, wk[...]).astype(BF16)
    oks[...] = kt[:, 0:KV_WIDTH]
    okw[...] = kt[:, KV_WIDTH:2 * KV_WIDTH]
    ft = _dot_nt(wt[...], h)
    oqt[...] = (ft[0:NSA_WIDTH] * (NSA_HEAD_DIM ** -0.5)).astype(BF16)
    okct[...] = ft[NSA_WIDTH:NSA_WIDTH + KV2]
    okst[...] = ft[NSA_WIDTH + KV2:NSA_WIDTH + 2 * KV2]
    okwt[...] = ft[NSA_WIDTH + 2 * KV2:NSA_WIDTH + 3 * KV2]


def _inproj(x, mods, kind, rows_per_seq, ws, wk, wt, tm):
    n, d = x.shape
    tps = max(rows_per_seq // tm, 1)
    widths = [w.shape[1] for w in ws] + [KV_WIDTH] * 2
    dtypes = [F32, BF16, BF16, F32, BF16, BF16]
    t_rows = [NSA_WIDTH, KV2, KV2, KV2]
    t_dtypes = [BF16, F32, F32, F32]
    if kind == "seq":
        t_specs = [pl.BlockSpec((None, r, tm), lambda i: (i // tps, 0, i % tps)) for r in t_rows]
        t_shapes = [jax.ShapeDtypeStruct((n // rows_per_seq, r, rows_per_seq), dt) for r, dt in zip(t_rows, t_dtypes)]
    else:
        t_specs = [pl.BlockSpec((r, tm), lambda i: (0, i)) for r in t_rows]
        t_shapes = [jax.ShapeDtypeStruct((r, n), dt) for r, dt in zip(t_rows, t_dtypes)]
    full = lambda w: pl.BlockSpec(w.shape, lambda i: (0, 0))
    return pl.pallas_call(
        _inproj_kernel,
        grid=(n // tm,),
        in_specs=[pl.BlockSpec((tm, d), lambda i: (i, 0)),
                  _mod_spec(kind, tm, d, tps, 3), _mod_spec(kind, tm, d, tps, 4)]
                 + [full(w) for w in ws] + [full(wk), full(wt)],
        out_specs=[pl.BlockSpec((tm, wd), lambda i: (i, 0)) for wd in widths] + t_specs,
        out_shape=[jax.ShapeDtypeStruct((n, wd), dt) for wd, dt in zip(widths, dtypes)] + t_shapes,
        compiler_params=_params("parallel"),
        name="inproj",
    )(x, mods, mods, *ws, wk, wt)


def _gdn_kernel(qkv_ref, z_ref, sm_ref, abt_ref, cb_ref, cw_ref, prow_ref, alog_ref, dtb_ref, nw_ref,
                s0_ref, tri_ref, triu_ref, bd_ref, expg_ref, expb_ref, o_ref, s_ref, xbuf,
                *, chunk, l_valid, l_pad):
    C = chunk
    HD = GDN_HEAD_DIM
    W = GDN_WIDTH
    NB = qkv_ref.shape[0]
    c = pl.program_id(1)

    @pl.when(c == 0)
    def _():
        xbuf[:, 0:SUBLANES, :] = cb_ref[...]
        s_ref[...] = s0_ref[...]

    cw = cw_ref[...]
    bd = bd_ref[...]
    hw = bd.shape[0]
    pr = prow_ref[...]
    qn_l, kn_l, gx_l, g_row_l, q_dec_l, k_dec_l, kb_l, vb_l, kbg_l, e_last_l, z_l = ([] for _ in range(11))
    for n in range(NB):
        x = qkv_ref[n]
        xbuf[n, SUBLANES:SUBLANES + C, :] = x
        y = (xbuf[n, 5:5 + C, :] * cw[0:1] + xbuf[n, 6:6 + C, :] * cw[1:2]
             + xbuf[n, 7:7 + C, :] * cw[2:3] + x * cw[3:4])
        xbuf[n, 0:SUBLANES, :] = x[C - SUBLANES:C, :]
        act = y * _sig(y)
        q = act[:, 0:W]
        k = act[:, W:2 * W]
        v = act[:, 2 * W:3 * W]

        sq = jnp.concatenate([q * q, k * k], 0)
        ss = jnp.concatenate([_dot_split2(sq[:, j:j + hw], bd) for j in range(0, W, hw)], 1)
        qn = q * lax.rsqrt(ss[:C] + 1e-6) * (HD ** -0.5)
        kn = k * lax.rsqrt(ss[C:] + 1e-6)

        sm = sm_ref[n]
        gcol = -jnp.exp(pr[0:1]) * _softplus(sm + pr[1:2])
        bcol = _sig(sm)
        ab = abt_ref[n]
        grow = -jnp.exp(alog_ref[...]) * _softplus(ab[0:GDN_HEADS] + dtb_ref[...])
        if l_pad != l_valid:
            vcol = (c * C + lax.broadcasted_iota(jnp.int32, (C, 1), 0) < l_valid).astype(F32)
            vrow = (c * C + lax.broadcasted_iota(jnp.int32, (1, C), 1) < l_valid).astype(F32)
            qn, kn, v = qn * vcol, kn * vcol, v * vcol
            gcol, bcol, grow = gcol * vcol, bcol * vcol, grow * vrow

        g_cum = jnp.dot(tri_ref[...], gcol, precision=HIGHEST, preferred_element_type=F32)
        gx = _dot_split2(g_cum, expg_ref[...])
        bx = _dot_split2(bcol, expb_ref[...])
        g_row = jnp.dot(grow, triu_ref[...], precision=HIGHEST, preferred_element_type=F32)

        eg = jnp.exp(gx)
        g_last = gx[C - 1:C, :]
        kb = kn * bx
        for lst, val in ((qn_l, qn), (kn_l, kn), (gx_l, gx), (g_row_l, g_row), (q_dec_l, qn * eg),
                         (k_dec_l, kn * jnp.exp(g_last - gx)), (kb_l, kb), (vb_l, v * bx), (kbg_l, kb * eg),
                         (e_last_l, jnp.exp(g_last)), (z_l, z_ref[n].astype(F32))):
            lst.append(val)

    ri = lax.broadcasted_iota(jnp.int32, (C, C), 0)
    ci = lax.broadcasted_iota(jnp.int32, (C, C), 1)
    incl = ri >= ci
    strict = ri > ci
    eye = (ri == ci).astype(F32)
    sh = min(C, SUBLANES).bit_length() - 1
    diag_blk = (ri >> sh) == (ci >> sh)
    merge_masks = []
    while (1 << sh) < C:
        merge_masks.append(((ri >> (sh + 1)) == (ci >> (sh + 1))) & ((ri >> sh) == (ci >> sh) + 1))
        sh += 1

    H = GDN_HEADS
    heads = range(NB * H)
    col = lambda xs, j: xs[j // H][:, (j % H) * HD:(j % H + 1) * HD]
    bfl = lambda xs: [x.astype(BF16) for x in xs]

    dec = [jnp.where(incl, jnp.exp(jnp.where(
        incl, gx_l[j // H][:, (j % H) * HD:(j % H) * HD + C] - g_row_l[j // H][j % H:j % H + 1, :], 0.0)), 0.0)
        for j in heads]
    r = [_dot_nt(jnp.concatenate([col(kb_l, h), col(qn_l, h)], 0).astype(BF16), col(kn_l, h).astype(BF16))
         for h in heads]
    a_kk = [jnp.where(strict, r[h][:C] * dec[h], 0.0) for h in heads]
    a_qk = bfl([r[h][C:] * dec[h] for h in heads])

    nd = [jnp.where(diag_blk, -a_kk[h], 0.0) for h in heads]
    ndb = bfl(nd)
    p2 = [_dot(ndb[h], ndb[h]) for h in heads]
    t = [eye + nd[h] for h in heads]
    r2 = [_dot(p2[h].astype(BF16), jnp.concatenate([t[h], p2[h]], 1).astype(BF16)) for h in heads]
    t = [t[h] + r2[h][:, :C] for h in heads]
    t = [t[h] + _dot(r2[h][:, C:].astype(BF16), t[h].astype(BF16)) for h in heads]
    for mm in merge_masks:
        tb = bfl(t)
        tl = bfl([_dot(tb[h], jnp.where(mm, a_kk[h], 0.0).astype(BF16)) for h in heads])
        t = [t[h] - _dot(tl[h], tb[h]) for h in heads]
    tb = bfl(t)

    rhs = [jnp.concatenate([col(vb_l, h), col(kbg_l, h)], 1) for h in heads]
    x0 = [_dot(tb[h], rhs[h].astype(BF16)) for h in heads]
    res = []
    for h in heads:
        ah = a_kk[h].astype(BF16)
        al = (a_kk[h] - ah.astype(F32)).astype(BF16)
        xh = x0[h].astype(BF16)
        xl = (x0[h] - xh.astype(F32)).astype(BF16)
        hh = _dot(ah, jnp.concatenate([xh, xl], 1))
        res.append(rhs[h] - x0[h] - (hh[:, :2 * HD] + hh[:, 2 * HD:] + _dot(al, xh)))
    uw = [x0[h] + _dot(tb[h], res[h].astype(BF16)) for h in heads]

    s_old = [s_ref[h // H, h % H] for h in heads]
    wq = [_dot(jnp.concatenate([uw[h][:, HD:], col(q_dec_l, h)], 0).astype(BF16), s_old[h].astype(BF16))
          for h in heads]
    v_new = bfl([uw[h][:, :HD] - wq[h][:C] for h in heads])
    o = [wq[h][C:] + _dot(a_qk[h], v_new[h]) for h in heads]
    for h in heads:
        s_ref[h // H, h % H] = (s_old[h] * col(e_last_l, h)
                                + _dot_tn(col(k_dec_l, h).astype(BF16), v_new[h]))

    nw = nw_ref[...]
    outs = []
    for h in heads:
        o_h = o[h] * lax.rsqrt(jnp.mean(o[h] * o[h], -1, keepdims=True) + 1e-6)
        zh = col(z_l, h)
        outs.append(o_h * nw * (zh * _sig(zh)))
    for n in range(NB):
        o_ref[n] = jnp.concatenate(outs[n * H:(n + 1) * H], 1).astype(BF16)


def _gdn(qkv, z, small, conv_buf, s0, conv_w, a_log, dt_bias, norm_w, l_valid, chunk):
    b, lp, w3 = qkv.shape
    C = chunk
    nc = lp // C
    H, HD, W = GDN_HEADS, GDN_HEAD_DIM, GDN_WIDTH
    abt = small[:, :, :2 * H].reshape(b, nc, C, 2 * H).transpose(0, 1, 3, 2)
    cb = jnp.pad(conv_buf, ((0, 0), (SUBLANES - (GDN_CONV - 1), 0), (0, 0)))
    prow = jnp.zeros((2, LANES), F32).at[0, :H].set(a_log).at[1, :H].set(dt_bias)
    alog_r = jnp.broadcast_to(a_log[:, None], (H, C))
    dtb_r = jnp.broadcast_to(dt_bias[:, None], (H, C))
    ix = jnp.arange(C)
    tri = (ix[:, None] >= ix[None, :]).astype(F32)
    triu = tri.T
    hid = jnp.arange(W) // HD
    hw = 2 * LANES
    bd = (hid[:hw, None] == hid[None, :hw]).astype(BF16)
    lane = jnp.arange(LANES)
    expg = (lane[:, None] == hid[None, :]).astype(BF16)
    expb = (lane[:, None] == hid[None, :] + H).astype(BF16)
    const = lambda shape: pl.BlockSpec(shape, lambda i, j: (0,) * len(shape))
    nb = GDN_SEQS_PER_STEP if b % GDN_SEQS_PER_STEP == 0 else 1
    o, s_fin = pl.pallas_call(
        functools.partial(_gdn_kernel, chunk=C, l_valid=l_valid, l_pad=lp),
        grid=(b // nb, nc),
        in_specs=[pl.BlockSpec((nb, C, w3), lambda i, j: (i, j, 0)),
                  pl.BlockSpec((nb, C, W), lambda i, j: (i, j, 0)),
                  pl.BlockSpec((nb, C, LANES), lambda i, j: (i, j, 0)),
                  pl.BlockSpec((nb, None, 2 * H, C), lambda i, j: (i, j, 0, 0)),
                  pl.BlockSpec((nb, SUBLANES, w3), lambda i, j: (i, 0, 0)),
                  const((GDN_CONV, w3)), const((2, LANES)), const((H, C)), const((H, C)), const((1, HD)),
                  pl.BlockSpec((nb, H, HD, HD), lambda i, j: (i, 0, 0, 0)),
                  const((C, C)), const((C, C)), const((hw, hw)), const((LANES, W)), const((LANES, W))],
        out_specs=[pl.BlockSpec((nb, C, W), lambda i, j: (i, j, 0)),
                   pl.BlockSpec((nb, H, HD, HD), lambda i, j: (i, 0, 0, 0))],
        out_shape=[jax.ShapeDtypeStruct((b, lp, W), BF16), jax.ShapeDtypeStruct((b, H, HD, HD), F32)],
        scratch_shapes=[pltpu.VMEM((nb, SUBLANES + C, w3), F32)],
        compiler_params=_params("parallel", "arbitrary"),
        name="gdn",
    )(qkv, z, small, abt, cb, conv_w, prow, alog_r, dtb_r, norm_w.reshape(1, HD), s0,
      tri, triu, bd, expg, expb)
    return o, s_fin


def _pool_tile(x, w, pm):
    t = x.shape[1]
    wt = jnp.concatenate([w] * (t // LANES), 1) if t > LANES else w
    return _dot_split2(x * wt, pm)


def _pool_kernel(x_ref, w_ref, pm_ref, o_ref):
    for n in range(x_ref.shape[0]):
        o_ref[n] = _pool_tile(x_ref[n], w_ref[...], pm_ref[...])


def _pool_matrix(t):
    nb = -(-(t // CMP_BLOCK) // LANES) * LANES
    return (jnp.arange(t)[:, None] // CMP_BLOCK == jnp.arange(nb)[None, :]).astype(BF16)


def _pool(kvt, wrow):
    b, rows, t = kvt.shape
    pm = _pool_matrix(t)
    nb = SUBLANES if (t <= LANES and b % SUBLANES == 0) else 1
    return pl.pallas_call(
        _pool_kernel,
        grid=(b // nb,),
        in_specs=[pl.BlockSpec((nb, rows, t), lambda i: (i, 0, 0)),
                  pl.BlockSpec(wrow.shape, lambda i: (0, 0)),
                  pl.BlockSpec(pm.shape, lambda i: (0, 0))],
        out_specs=pl.BlockSpec((nb, rows, pm.shape[1]), lambda i: (i, 0, 0)),
        out_shape=jax.ShapeDtypeStruct((b, rows, pm.shape[1]), F32),
        compiler_params=_params("parallel"),
        name="pool",
    )(kvt, wrow, pm)


def _pool_paged_kernel(pt_ref, *refs, pages):
    del pt_ref
    w_ref, pm_ref, o_ref = refs[pages:]
    outs = []
    for c in range(0, pages, PAGES_PER_TILE):
        x = jnp.concatenate([refs[p][...] for p in range(c, c + PAGES_PER_TILE)], 1)
        outs.append(_pool_tile(x, w_ref[...], pm_ref[...]))
    o_ref[...] = jnp.concatenate(outs, 1)


def _pool_paged(cache_t, page_table, wrow, pages):
    _, rows, ps = cache_t.shape
    b, n_pages = page_table.shape
    assert ps == PAGE_SIZE and pages % PAGES_PER_TILE == 0 and n_pages % pages == 0
    pm = _pool_matrix(PAGES_PER_TILE * ps)
    out_lanes = pages * ps // CMP_BLOCK
    page_spec = lambda p: pl.BlockSpec((None, rows, ps), lambda i, j, pt: (pt[i, j * pages + p], 0, 0))
    return pl.pallas_call(
        functools.partial(_pool_paged_kernel, pages=pages),
        grid_spec=pltpu.PrefetchScalarGridSpec(
            num_scalar_prefetch=1,
            grid=(b, n_pages // pages),
            in_specs=[page_spec(p) for p in range(pages)]
                     + [pl.BlockSpec(wrow.shape, lambda i, j, pt: (0, 0)),
                        pl.BlockSpec(pm.shape, lambda i, j, pt: (0, 0))],
            out_specs=pl.BlockSpec((None, rows, out_lanes), lambda i, j, pt: (i, 0, j))),
        out_shape=jax.ShapeDtypeStruct((b, rows, n_pages * ps // CMP_BLOCK), F32),
        compiler_params=_params("parallel", "arbitrary"),
        name="pool_paged",
    )(page_table, *([cache_t] * pages), wrow, pm)


def _cmp_probs(sc, pos, tq):
    n = sc.shape[1]
    lane = lax.broadcasted_iota(jnp.int32, (1, n), 1)
    maskc = ((lane + 1) * CMP_BLOCK - 1) <= pos
    ps = []
    imp = None
    for g in range(NSA_GROUP):
        s = jnp.where(maskc, sc[g * tq:(g + 1) * tq], NEG)
        m = jnp.max(s, -1, keepdims=True)
        e = jnp.where(maskc, jnp.exp(s - m), 0.0)
        den = jnp.sum(e, -1, keepdims=True)
        p = e / jnp.where(den > 0.0, den, 1.0)
        ps.append(p)
        imp = p if imp is None else imp + p
    return jnp.concatenate(ps, 0), imp


def _select_blocks(imp, pos, axis):
    n = imp.shape[axis]
    idx = lax.broadcasted_iota(jnp.int32, (n, 1) if axis == 0 else (1, n), axis)
    even = (idx & 1) == 0
    imp2 = imp + jnp.where(even, pltpu.roll(imp, n - 1, axis), pltpu.roll(imp, 1, axis))
    blk = idx >> 1
    valid = blk * SEL_BLOCK <= pos
    cur = pos >> 6
    forced = (blk == 0) | (blk == cur) | (blk == cur - 1)
    score = jnp.where(valid, jnp.where(forced, BIG, imp2), -1.0)
    work = jnp.where(even, score, -2.0)
    idxf = idx.astype(F32)
    sel = jnp.zeros(work.shape, F32)
    for _ in range(SEL_TOPK):
        m = jnp.max(work, axis, keepdims=True)
        first = jnp.min(jnp.where(work == m, idxf, 1e9), axis, keepdims=True)
        pick = idxf == first
        sel = jnp.where(pick, 1.0, sel)
        work = jnp.where(pick, -2.0, work)
    sel = jnp.where(score >= 0.0, sel, 0.0)
    return sel + pltpu.roll(sel, 1, axis)


def _online_update(carry, s, mf, vt):
    m, l, acc = carry
    s = jnp.where(mf > 0.5, s, NEG)
    m_new = jnp.maximum(m, jnp.max(s, -1, keepdims=True))
    alpha = jnp.exp(m - m_new)
    p = jnp.exp(s - m_new) * mf
    l = alpha * l + jnp.sum(p, -1, keepdims=True)
    acc = alpha * acc + _dot_nt(p.astype(BF16), vt)
    return m_new, l, acc


def _expand_matrix(n_blocks, n_keys):
    return (jnp.arange(n_keys)[None, :] // CMP_BLOCK == jnp.arange(n_blocks)[:, None]).astype(BF16)


def _nsa_prompt_kernel(qt_ref, ks_ref, kw_ref, kvb_ref, kvs_ref, kvw_ref, sm_ref, et_ref, o_ref, *, n_rows):
    TQ, TK, G, HD, KV = NSA_TQ, NSA_TK, NSA_GROUP, NSA_HEAD_DIM, NSA_KV_HEADS
    i = pl.program_id(1)
    t0 = i * TQ
    pos = t0 + lax.broadcasted_iota(jnp.int32, (1, TQ), 1)
    gst = _sig(sm_ref[...]).T
    gate_row = lambda br, h, g: 2 * GDN_HEADS + br * NSA_Q_HEADS + h * G + g

    n_wt = WINDOW // TQ + 1
    w_off = [pl.multiple_of(jnp.maximum(i - (n_wt - 1) + j, 0) * TQ, TQ) for j in range(n_wt)]
    cidx = lax.broadcasted_iota(jnp.int32, (WINDOW + TQ, 1), 0)
    diff = lax.broadcasted_iota(jnp.int32, (1, TQ), 1) + WINDOW - cidx
    bw = jnp.where((diff >= 0) & (diff < WINDOW) & (t0 - WINDOW + cidx >= 0), 0.0, NEG)
    kidx = lax.broadcasted_iota(jnp.int32, (TK, 1), 0)
    n_used = (t0 + TQ + TK - 1) // TK

    krow = lambda h: slice(h * HD, (h + 1) * HD)
    vrow = lambda h: slice((KV + h) * HD, (KV + h + 1) * HD)
    zeros = jnp.zeros((HD, TQ), BF16)
    blk = lax.broadcasted_iota(jnp.int32, (kvb_ref.shape[1], 1), 0)
    maskc = ((blk + 1) * CMP_BLOCK - 1) <= pos
    kw_all = jnp.concatenate([kw_ref[pl.ds(o, TQ), :] for o in w_off], 0)
    qbd, oc, ow, selt = [], [], [], []
    for h in range(KV):
        qh = [qt_ref[(h * G + g) * HD:(h * G + g + 1) * HD, :] for g in range(G)]
        qbd.append([jnp.concatenate([zeros] * h + [q] + [zeros] * (KV - 1 - h), 0) for q in qh])

        kc = kvb_ref[krow(h), :].astype(BF16)
        vc = kvb_ref[vrow(h), :].astype(BF16)
        imp = None
        oc.append([])
        for g in range(G):
            s = jnp.where(maskc, _dot_tn(kc, qh[g]), NEG)
            e = jnp.where(maskc, jnp.exp(s - jnp.max(s, 0, keepdims=True)), 0.0)
            den = jnp.sum(e, 0, keepdims=True)
            p = e / jnp.where(den > 0.0, den, 1.0)
            imp = p if imp is None else imp + p
            oc[h].append(_dot(vc, p.astype(BF16)))
        selt.append(_select_blocks(imp[:n_rows], pos, 0).astype(BF16))

        vw = jnp.concatenate([kvw_ref[vrow(h), pl.ds(o, TQ)] for o in w_off], 1).astype(BF16)
        vw = jnp.concatenate([vw, jnp.ones_like(vw)], 0)
        ow.append([])
        for g in range(G):
            s = _dot(kw_all, qbd[h][g]) + bw
            e = jnp.exp(s - jnp.max(s, 0, keepdims=True))
            r = _dot(vw, e.astype(BF16))
            ow[h].append(r[:HD] / r[HD:HD + 1])

    def scores(kt):
        off = pl.multiple_of(kt * TK, TK)
        k_t = ks_ref[pl.ds(off, TK), :]
        e_t = et_ref[pl.ds(off, TK), :]
        causal = off + kidx <= pos
        out = []
        for h in range(KV):
            bias = jnp.where((_dot(e_t, selt[h]) > 0.5) & causal, 0.0, NEG)
            out += [_dot(k_t, qbd[h][g]) + bias for g in range(G)]
        return out

    def body(kt, carry):
        off = pl.multiple_of(kt * TK, TK)
        s_next = scores(jnp.minimum(kt + 1, n_used - 1))
        out = []
        for h in range(KV):
            v_t = kvs_ref[vrow(h), pl.ds(off, TK)].astype(BF16)
            v_t = jnp.concatenate([v_t, jnp.ones_like(v_t)], 0)
            for g in range(G):
                m, acc, s = carry[h * G + g]
                m_new = jnp.maximum(m, jnp.max(s, 0, keepdims=True))
                p = jnp.exp(s - m_new)
                out.append((m_new, jnp.exp(m - m_new) * acc + _dot(v_t, p.astype(BF16)), s_next[h * G + g]))
        return tuple(out)

    s0 = scores(0)
    init = tuple((jnp.full((1, TQ), NEG, F32), jnp.zeros((2 * HD, TQ), F32), s0[j]) for j in range(KV * G))
    res = lax.fori_loop(0, n_used, body, init)

    outs = []
    for h in range(KV):
        for g in range(G):
            acc = res[h * G + g][1]
            gate = lambda br: gst[gate_row(br, h, g):gate_row(br, h, g) + 1, :]
            outs.append(gate(0) * oc[h][g] + gate(1) * (acc[:HD] / acc[HD:HD + 1]) + gate(2) * ow[h][g])
    o_ref[...] = jnp.concatenate(outs, 0).astype(BF16)


def _nsa_prompt(q_t, ks, kw, kvb_t, kvs_t, kvw_t, small):
    b, _, l = q_t.shape
    TQ, TK = NSA_TQ, NSA_TK
    n_cmp = l // CMP_BLOCK
    assert kvb_t.shape[2] == LANES and n_cmp <= LANES and l % TK == 0 and WINDOW % TQ == 0
    n_rows = -(-n_cmp // SUBLANES) * SUBLANES
    emat_t = _expand_matrix(n_rows, l).T
    slab = lambda t: pl.BlockSpec((None,) + t.shape[1:], lambda bi, i: (bi, 0, 0))
    return pl.pallas_call(
        functools.partial(_nsa_prompt_kernel, n_rows=n_rows),
        grid=(b, l // TQ),
        in_specs=[pl.BlockSpec((None, NSA_WIDTH, TQ), lambda bi, i: (bi, 0, i)),
                  slab(ks), slab(kw), slab(kvb_t), slab(kvs_t), slab(kvw_t),
                  pl.BlockSpec((None, TQ, LANES), lambda bi, i: (bi, i, 0)),
                  pl.BlockSpec((l, n_rows), lambda bi, i: (0, 0))],
        out_specs=pl.BlockSpec((None, NSA_WIDTH, TQ), lambda bi, i: (bi, 0, i)),
        out_shape=jax.ShapeDtypeStruct((b, NSA_WIDTH, l), BF16),
        compiler_params=_params("parallel", "arbitrary"),
        name="nsa_prompt",
    )(q_t, ks, kw, kvb_t, kvs_t, kvw_t, small, emat_t)


LS = SUBLANES
ROWS_H = NSA_GROUP * LS


def _row_token(rows):
    return lax.broadcasted_iota(jnp.int32, (rows, 1), 0) & (LS - 1)


def _nsa_sample_a_kernel(q_ref, kvb_ref, cw_ref, new_ref, oc_ref, ow_ref, sel_ref, *, past, l_new):
    HD, KV = NSA_HEAD_DIM, NSA_KV_HEADS
    NB = q_ref.shape[0]
    pos = past + lax.broadcasted_iota(jnp.int32, (LS, 1), 0)
    wb = cw_ref.shape[2]
    tok = _row_token(ROWS_H)
    c1 = lax.broadcasted_iota(jnp.int32, (1, wb), 1)
    d1 = tok + wb - c1
    m1 = jnp.where((d1 >= 0) & (d1 < WINDOW) & (past - wb + c1 >= 0), 1.0, 0.0)
    c2 = lax.broadcasted_iota(jnp.int32, (1, new_ref.shape[2]), 1)
    d2 = tok - c2
    m2 = jnp.where((d2 >= 0) & (d2 < WINDOW) & (c2 < l_new), 1.0, 0.0)
    imps = []
    for n in range(NB):
        for h in range(KV):
            krow = slice(h * HD, (h + 1) * HD)
            vrow = slice((KV + h) * HD, (KV + h + 1) * HD)
            qh = q_ref[n, h]
            p, imp = _cmp_probs(_dot(qh, kvb_ref[n, krow, :].astype(BF16)), pos, LS)
            imps.append(imp)
            oc_ref[n, h] = _dot_nt(p.astype(BF16), kvb_ref[n, vrow, :].astype(BF16))

            s1 = jnp.where(m1 > 0.5, _dot(qh, cw_ref[n, krow, :].astype(BF16)), NEG)
            s2 = jnp.where(m2 > 0.5, _dot(qh, new_ref[n, krow, :].astype(BF16)), NEG)
            m = jnp.maximum(jnp.max(s1, -1, keepdims=True), jnp.max(s2, -1, keepdims=True))
            e1 = jnp.exp(s1 - m) * m1
            e2 = jnp.exp(s2 - m) * m2
            den = jnp.sum(e1, -1, keepdims=True) + jnp.sum(e2, -1, keepdims=True)
            den = jnp.where(den > 0.0, den, 1.0)
            ow_ref[n, h] = (_dot_nt((e1 / den).astype(BF16), cw_ref[n, vrow, :].astype(BF16))
                            + _dot_nt((e2 / den).astype(BF16), new_ref[n, vrow, :].astype(BF16)))
    sel = _select_blocks(jnp.concatenate(imps, 0), jnp.concatenate([pos] * (NB * KV), 0), 1)
    for n in range(NB):
        for h in range(KV):
            sel_ref[n, h] = sel[(n * KV + h) * LS:(n * KV + h + 1) * LS]


def _nsa_sample_a(q_rows, kvb_t, cache_win_t, kvw_new_t, past, l_new):
    b, _, ncp = kvb_t.shape
    nb = SAMPLE_SEQS_PER_STEP if b % SAMPLE_SEQS_PER_STEP == 0 else 1
    per_seq = lambda t: pl.BlockSpec((nb,) + t.shape[1:], lambda i: (i,) + (0,) * (t.ndim - 1))
    out_rows = jax.ShapeDtypeStruct(q_rows.shape, F32)
    sel_shape = jax.ShapeDtypeStruct((b, NSA_KV_HEADS, LS, ncp), F32)
    return pl.pallas_call(
        functools.partial(_nsa_sample_a_kernel, past=past, l_new=l_new),
        grid=(b // nb,),
        in_specs=[per_seq(q_rows), per_seq(kvb_t), per_seq(cache_win_t), per_seq(kvw_new_t)],
        out_specs=[per_seq(out_rows), per_seq(out_rows), per_seq(sel_shape)],
        out_shape=[out_rows, out_rows, sel_shape],
        compiler_params=_params("parallel"),
        name="nsa_sample_a",
    )(q_rows, kvb_t, cache_win_t, kvw_new_t)


def _nsa_sample_sel_kernel(pt_ref, *refs, pages, l_new):
    del pt_ref
    page_refs = refs[:pages]
    q_ref, sel_ref, tail_ref, new_ref, e_ref, oc_ref, ow_ref, gate_ref, o_ref, m_scr, l_scr, acc_scr = refs[pages:]
    HD, KV, G = NSA_HEAD_DIM, NSA_KV_HEADS, NSA_GROUP
    i = pl.program_id(1)
    last = i == pl.num_programs(1) - 1

    @pl.when(i == 0)
    def _():
        m_scr[...] = jnp.full_like(m_scr, NEG)
        l_scr[...] = jnp.zeros_like(l_scr)
        acc_scr[...] = jnp.zeros_like(acc_scr)

    kv = jnp.concatenate([r[...] for r in page_refs], 1)
    emat = e_ref[...]
    for h in range(KV):
        krow = slice(h * HD, (h + 1) * HD)
        vrow = slice((KV + h) * HD, (KV + h + 1) * HD)
        selh = sel_ref[h].astype(BF16)
        mh = jnp.concatenate([_dot(selh[:, c:c + LANES], emat) for c in range(0, selh.shape[1], LANES)], 1)
        carry = _online_update((m_scr[h], l_scr[h], acc_scr[h]),
                               _dot(q_ref[h], kv[krow].astype(BF16)),
                               jnp.concatenate([mh] * G, 0), kv[vrow].astype(BF16))
        m_scr[h], l_scr[h], acc_scr[h] = carry

    @pl.when(last)
    def _():
        tok = _row_token(ROWS_H)
        c2 = lax.broadcasted_iota(jnp.int32, (1, new_ref.shape[1]), 1)
        for h in range(KV):
            krow = slice(h * HD, (h + 1) * HD)
            vrow = slice((KV + h) * HD, (KV + h + 1) * HD)
            flag = jnp.concatenate([tail_ref[h][:, 0:1]] * G, 0)
            mt = jnp.where((c2 <= tok) & (c2 < l_new) & (flag > 0.5), 1.0, 0.0)
            _, l, acc = _online_update((m_scr[h], l_scr[h], acc_scr[h]),
                                       _dot(q_ref[h], new_ref[krow, :].astype(BF16)), mt,
                                       new_ref[vrow, :].astype(BF16))
            osel = acc / jnp.where(l > 0.0, l, 1.0)
            gs = _sig(gate_ref[h])
            o_ref[h] = gs[:, 0:1] * oc_ref[h] + gs[:, 1:2] * osel + gs[:, 2:3] * ow_ref[h]


def _nsa_sample_sel(q_rows, selmask, cache_sel_t, page_table, kvs_new_t, oc, ow, gate_rows, pages, l_new):
    b, n_pages = page_table.shape
    _, rows, ps = cache_sel_t.shape
    lanes_per_step = pages * ps // CMP_BLOCK
    assert lanes_per_step % LANES == 0 and n_pages % pages == 0
    n_steps = n_pages // pages
    emat = _expand_matrix(LANES, LANES * CMP_BLOCK)
    page_spec = lambda p: pl.BlockSpec((None, rows, ps), lambda bi, i, pt: (pt[bi, i * pages + p], 0, 0))
    per_seq = lambda t: pl.BlockSpec((None,) + t.shape[1:], lambda bi, i, pt: (bi,) + (0,) * (t.ndim - 1))
    tail_block = n_steps * lanes_per_step // LANES
    kvh = NSA_KV_HEADS
    return pl.pallas_call(
        functools.partial(_nsa_sample_sel_kernel, pages=pages, l_new=l_new),
        grid_spec=pltpu.PrefetchScalarGridSpec(
            num_scalar_prefetch=1,
            grid=(b, n_steps),
            in_specs=[page_spec(p) for p in range(pages)] + [
                per_seq(q_rows),
                pl.BlockSpec((None, kvh, LS, lanes_per_step), lambda bi, i, pt: (bi, 0, 0, i)),
                pl.BlockSpec((None, kvh, LS, LANES), lambda bi, i, pt: (bi, 0, 0, tail_block)),
                per_seq(kvs_new_t),
                pl.BlockSpec(emat.shape, lambda bi, i, pt: (0, 0)),
                per_seq(oc), per_seq(ow), per_seq(gate_rows)],
            out_specs=per_seq(oc),
            scratch_shapes=[pltpu.VMEM((kvh, ROWS_H, 1), F32), pltpu.VMEM((kvh, ROWS_H, 1), F32),
                            pltpu.VMEM((kvh, ROWS_H, NSA_HEAD_DIM), F32)]),
        out_shape=jax.ShapeDtypeStruct(oc.shape, F32),
        compiler_params=_params("parallel", "arbitrary"),
        name="nsa_sample_sel",
    )(page_table, *([cache_sel_t] * pages), q_rows, selmask, selmask, kvs_new_t, emat, oc, ow, gate_rows)


def _nsa_sample(q_t, kvc_t, kvs_t, kvw_t, gate, cache_cmp_t, cache_sel_t, cache_win_t, page_table, wrow):
    b, _, l = q_t.shape
    KV, G, HD = NSA_KV_HEADS, NSA_GROUP, NSA_HEAD_DIM
    n_pages = page_table.shape[1]
    ps = cache_cmp_t.shape[2]
    past = n_pages * ps
    assert l <= LS and ps == PAGE_SIZE
    pages = min(PAGES_PER_STEP, n_pages)
    pad_new = lambda t: jnp.pad(t, ((0, 0), (0, 0), (0, ps - l)))
    kvb_t = jnp.concatenate([_pool_paged(cache_cmp_t, page_table, wrow, pages), _pool(pad_new(kvc_t), wrow)], 2)
    q_rows = jnp.pad(q_t.reshape(b, KV, G, HD, l).transpose(0, 1, 2, 4, 3),
                     ((0, 0),) * 3 + ((0, LS - l), (0, 0))).reshape(b, KV, ROWS_H, HD)
    oc, ow, selmask = _nsa_sample_a(q_rows, kvb_t, cache_win_t, pad_new(kvw_t), past, l)
    gate_rows = jnp.pad(gate.reshape(b, l, 3, KV, G).transpose(0, 3, 4, 1, 2),
                        ((0, 0),) * 3 + ((0, LS - l), (0, 0))).reshape(b, KV, ROWS_H, 3)
    o = _nsa_sample_sel(q_rows, selmask, cache_sel_t, page_table, pad_new(kvs_t), oc, ow, gate_rows, pages, l)
    o = o.reshape(b, KV, G, LS, HD)[:, :, :, :l]
    return o.transpose(0, 3, 1, 2, 4).reshape(b, l, NSA_WIDTH).astype(BF16)


def _merge_kernel(x_ref, gt_ref, og_ref, on_ref, m_ref, wbg_ref, wbn_ref, wo_ref, lg_ref, lb_ref, o_ref,
                  *, alpha, nsa_feature_major):
    d = x_ref.shape[1]
    m = m_ref[...].astype(F32)
    nsa_dot = _dot_tn if nsa_feature_major else _dot
    mix = m[:, :d] * _dot(og_ref[...], wbg_ref[...]) + m[:, d:] * nsa_dot(on_ref[...], wbn_ref[...])
    y = _dot(mix.astype(BF16), wo_ref[...])
    o_ref[...] = _layer_norm(alpha * x_ref[...] + gt_ref[...] * y, lg_ref[...], lb_ref[...])


def _merge(x, mods, kind, rows_per_seq, o_gdn, o_nsa, msig, wbg, wbn, wo, lg, lb, alpha, tm):
    n, d = x.shape
    tps = max(rows_per_seq // tm, 1)
    row = lambda wd: pl.BlockSpec((tm, wd), lambda i: (i, 0))
    full = lambda a: pl.BlockSpec(a.shape, lambda i: (0, 0))
    feature_major = o_nsa.ndim == 3
    nsa_spec = (pl.BlockSpec((None, o_nsa.shape[1], tm), lambda i: (i // tps, 0, i % tps)) if feature_major
                else row(o_nsa.shape[1]))
    return pl.pallas_call(
        functools.partial(_merge_kernel, alpha=alpha, nsa_feature_major=feature_major),
        grid=(n // tm,),
        in_specs=[row(d), _mod_spec(kind, tm, d, tps, 5), row(o_gdn.shape[1]), nsa_spec,
                  row(2 * d), full(wbg), full(wbn), full(wo), full(lg), full(lb)],
        out_specs=row(d),
        out_shape=jax.ShapeDtypeStruct((n, d), F32),
        compiler_params=_params("parallel"),
        name="merge",
    )(x, mods, o_gdn, o_nsa, msig, wbg, wbn, wo, lg, lb)


def _row_tile(n, pref):
    t = min(pref, n)
    while n % t:
        t //= 2
    return t


def _ff_tile(dff):
    for parts in (11, 4, 2, 1, 22):
        if dff % parts == 0 and (dff // parts) % LANES == 0:
            return dff // parts
    return dff


def _to_rows(kv_t):
    b, _, t = kv_t.shape
    return kv_t.reshape(b, 2, NSA_KV_HEADS, NSA_HEAD_DIM, t).transpose(0, 4, 1, 2, 3)


def _to_feature_major(kv_rows):
    n, t = kv_rows.shape[:2]
    return kv_rows.transpose(0, 2, 3, 4, 1).reshape(n, KV2, t)


def _layer(x, mod, kind, past, lw, alpha):
    b, l, d = x.shape
    n = b * l
    (wg1, wu1, wd1, wg2, wu2, wd2, w_in_parts, wk, wt, conv_w, a_log, dt_bias, norm_w, wrow,
     wbg, wbn, wo, ln_g, ln_b) = lw
    tm = _row_tile(l if kind == "seq" else n, 512)
    tm_in = _row_tile(l if kind == "seq" else n, 256)
    tf = _ff_tile(wg1.shape[1])
    tm_ff = _row_tile(l if kind == "seq" else n, 512)
    lg = lambda i: ln_g[i].reshape(1, d)
    lb = lambda i: ln_b[i].reshape(1, d)

    x1 = _ffn(x.reshape(n, d), mod, kind, l, 0, wg1, wu1, wd1, lg(0), lb(0), alpha, tm_ff, tf)

    qkv, z, msig, small, ks, kw, q_t, kvc_t, kvs_t, kvw_t = _inproj(x1, mod, kind, l, w_in_parts, wk, wt, tm_in)
    seq = lambda t: t.reshape(b, l, t.shape[-1])
    qkv, z, small, ks, kw = [seq(t) for t in (qkv, z, small, ks, kw)]
    if kind == "tok":
        q_t, kvc_t, kvs_t, kvw_t = [t.reshape(t.shape[0], b, l).transpose(1, 0, 2)
                                    for t in (q_t, kvc_t, kvs_t, kvw_t)]

    if past is None:
        conv_buf = jnp.zeros((b, GDN_CONV - 1, 3 * GDN_WIDTH), F32)
        s0 = jnp.zeros((b, GDN_HEADS, GDN_HEAD_DIM, GDN_HEAD_DIM), F32)
        o_nsa = _nsa_prompt(q_t, ks, kw, _pool(kvc_t, wrow), kvs_t, kvw_t, small)
        win_t = kvw_t[:, :, l - min(WINDOW, l):]
    else:
        s0, conv_buf, cache_cmp, cache_sel, cache_win, page_table = past
        cache_win_t = _to_feature_major(cache_win)
        gate = small[:, :, 2 * GDN_HEADS:2 * GDN_HEADS + 3 * NSA_Q_HEADS]
        o_nsa = _nsa_sample(q_t, kvc_t, kvs_t, kvw_t, gate, _to_feature_major(cache_cmp),
                            _to_feature_major(cache_sel), cache_win_t, page_table, wrow)
        o_nsa = o_nsa.reshape(n, NSA_WIDTH)
        win_t = jnp.concatenate([cache_win_t, kvw_t], 2)[:, :, l:]

    chunk = min(GDN_CHUNK, -(-l // SUBLANES) * SUBLANES)
    lp = -(-l // chunk) * chunk
    padl = lambda t: jnp.pad(t, ((0, 0), (0, lp - l), (0, 0)))
    o_gdn, s_new = _gdn(padl(qkv), padl(z), padl(small), conv_buf, s0, conv_w, a_log, dt_bias, norm_w, l, chunk)
    conv_new = jnp.concatenate([conv_buf, qkv], 1)[:, -(GDN_CONV - 1):]

    x2 = _merge(x1, mod, kind, l, o_gdn[:, :l].reshape(n, GDN_WIDTH), o_nsa, msig,
                wbg, wbn, wo, lg(1), lb(1), alpha, tm)
    x3 = _ffn(x2, mod, kind, l, 6, wg2, wu2, wd2, lg(2), lb(2), alpha, tm_ff, tf)
    return x3.reshape(b, l, d), (s_new, conv_new, _to_rows(kvc_t), _to_rows(kvs_t), _to_rows(win_t))


def _split_w_in(w_in, d):
    splits = (3 * GDN_WIDTH, GDN_WIDTH, GDN_HEADS, GDN_HEADS, NSA_WIDTH, KV2, KV2, KV2, 3 * NSA_Q_HEADS, 2 * d)
    offs = [0]
    for s in splits:
        offs.append(offs[-1] + s)
    qkv, z, a, bb, q, kc, ks, kw, gate, merge = [w_in[:, offs[i]:offs[i + 1]] for i in range(len(splits))]
    n_small = 2 * GDN_HEADS + 3 * NSA_Q_HEADS
    small = jnp.pad(jnp.concatenate([a, bb, gate], 1), ((0, 0), (0, LANES - n_small)))
    wk = jnp.concatenate([ks[:, :KV_WIDTH], kw[:, :KV_WIDTH]], 1)
    wt = jnp.concatenate([q, kc, ks, kw], 1).T
    return [t.astype(BF16) for t in (qkv, z, merge, small)], wk.astype(BF16), wt.astype(BF16)


def kernel(x_prompt, x_sample, c_prompt, c_sample, state_gdn, state_gdn_conv, cache_cmp_kv, cache_sel_kv, cache_win_kv, page_table, ln_g, ln_b, w_ada, b_ada, w_ff1_gu, w_ff1_dn, w_ff2_gu, w_ff2_dn, w_in, gdn_conv_w, gdn_a_log, gdn_dt_bias, gdn_norm_w, nsa_w_cmp, w_br_gdn, w_br_nsa, w_out):
    depth = w_in.shape[0]
    alpha = (2.0 * depth) ** 0.25
    bp, lp, d = x_prompt.shape
    bs, ls, _ = x_sample.shape
    y_p, y_s = x_prompt, x_sample
    p_st, s_st = [], []
    for l in range(depth):
        dff = w_ff1_dn.shape[1]
        bf = lambda t: t.astype(BF16)
        wrow = jnp.tile(jnp.broadcast_to(nsa_w_cmp[l].transpose(0, 2, 1)[:, :, None, :],
                                         (2, NSA_KV_HEADS, NSA_HEAD_DIM, CMP_BLOCK)).reshape(KV2, CMP_BLOCK),
                        (1, LANES // CMP_BLOCK))
        w_in_parts, wk, wt = _split_w_in(w_in[l], d)
        lw = (bf(w_ff1_gu[l][:, :dff]), bf(w_ff1_gu[l][:, dff:]), bf(w_ff1_dn[l]),
              bf(w_ff2_gu[l][:, :dff]), bf(w_ff2_gu[l][:, dff:]), bf(w_ff2_dn[l]),
              w_in_parts, wk, wt, gdn_conv_w[l], gdn_a_log[l], gdn_dt_bias[l], gdn_norm_w[l], wrow,
              bf(w_br_gdn[l]), bf(w_br_nsa[l]), bf(w_out[l]), ln_g[l], ln_b[l])
        c_all = jnp.concatenate([c_prompt, c_sample], 0)
        r = c_all.shape[0]
        rp = -(-r // SUBLANES) * SUBLANES
        mod = _ada(jnp.pad(c_all, ((0, rp - r), (0, 0))), w_ada[l], b_ada[l])
        mod_p = mod[:bp].reshape(bp * 9, 1, d)
        mod_s = jnp.repeat(mod[bp:bp + bs], ls, axis=0)
        y_p, st_p = _layer(y_p, mod_p, "seq", None, lw, alpha)
        past = (state_gdn[l], state_gdn_conv[l], cache_cmp_kv[l], cache_sel_kv[l], cache_win_kv[l], page_table)
        y_s, st_s = _layer(y_s, mod_s, "tok", past, lw, alpha)
        p_st.append(st_p)
        s_st.append(st_s)
    p_out = [jnp.stack(t) for t in zip(*p_st)]
    s_out = [jnp.stack(t) for t in zip(*s_st)]
    return (y_p, y_s, *p_out, *s_out)
```

```python
import functools

import jax
import jax.numpy as jnp
from jax import lax
from jax.experimental import pallas as pl
from jax.experimental.pallas import tpu as pltpu

F32 = jnp.float32
BF16 = jnp.bfloat16
HIGHEST = lax.Precision.HIGHEST

GDN_HEADS = 8
GDN_HEAD_DIM = 64
GDN_WIDTH = GDN_HEADS * GDN_HEAD_DIM
GDN_CONV = 4
GDN_CHUNK = 64
NSA_Q_HEADS = 8
NSA_KV_HEADS = 2
NSA_HEAD_DIM = 64
NSA_GROUP = NSA_Q_HEADS // NSA_KV_HEADS
NSA_WIDTH = NSA_Q_HEADS * NSA_HEAD_DIM
KV_WIDTH = NSA_KV_HEADS * NSA_HEAD_DIM
KV2 = 2 * KV_WIDTH
CMP_BLOCK = 32
SEL_BLOCK = 64
SEL_TOPK = 16
WINDOW = 512
PAGE_SIZE = 128
NEG = -1e30
BIG = 1e4
LN_EPS = 1e-5

SUBLANES = 8
LANES = 128
VMEM_LIMIT = 52 * 1024 * 1024

NSA_TQ = 128
NSA_TK = 256
PAGES_PER_STEP = 64
PAGES_PER_TILE = LANES * CMP_BLOCK // PAGE_SIZE
GDN_SEQS_PER_STEP = 4
SAMPLE_SEQS_PER_STEP = 8


def _sig(x):
    return 1.0 / (1.0 + jnp.exp(-x))


def _softplus(x):
    return jnp.maximum(x, 0.0) + jnp.log(1.0 + jnp.exp(-jnp.abs(x)))


def _layer_norm(r, g, b):
    mu = jnp.mean(r, -1, keepdims=True)
    d = r - mu
    var = jnp.mean(d * d, -1, keepdims=True)
    return d * lax.rsqrt(var + LN_EPS) * g + b


def _dot(a, b):
    return jnp.dot(a, b, preferred_element_type=F32)


def _dot_nt(a, b):
    return lax.dot_general(a, b, (((1,), (1,)), ((), ())), preferred_element_type=F32)


def _dot_tn(a, b):
    return lax.dot_general(a, b, (((0,), (0,)), ((), ())), preferred_element_type=F32)


def _dot_split2(a, sel):
    hi = a.astype(BF16)
    lo = (a - hi.astype(F32)).astype(BF16)
    return _dot(hi, sel) + _dot(lo, sel)


def _params(*sem):
    return pltpu.CompilerParams(dimension_semantics=sem, vmem_limit_bytes=VMEM_LIMIT)


def _mod_spec(kind, tm, d, tiles_per_seq, k):
    if kind == "seq":
        return pl.BlockSpec((None, 1, d), lambda i, *_: ((i // tiles_per_seq) * 9 + k, 0, 0))
    return pl.BlockSpec((tm, d), lambda i, *_: (i, k))


def _ada_kernel(c_ref, w_ref, b_ref, o_ref):
    c = c_ref[...]
    h = (c * _sig(c)).astype(BF16)
    o_ref[...] = _dot(h, w_ref[...].astype(BF16)) + b_ref[...]


def _ada(c, w, b):
    r, d = c.shape
    n = w.shape[1]
    tn = d
    return pl.pallas_call(
        _ada_kernel,
        grid=(n // tn,),
        in_specs=[pl.BlockSpec((r, d), lambda j: (0, 0)),
                  pl.BlockSpec((d, tn), lambda j: (0, j)),
                  pl.BlockSpec((1, tn), lambda j: (0, j))],
        out_specs=pl.BlockSpec((r, tn), lambda j: (0, j)),
        out_shape=jax.ShapeDtypeStruct((r, n), F32),
        compiler_params=_params("arbitrary"),
        name="ada",
    )(c, w, b.reshape(1, n))


def _ffn_kernel(x_ref, sh_ref, sc_ref, gt_ref, wg_ref, wu_ref, wd_ref, lg_ref, lb_ref, o_ref, *, alpha, tf):
    x = x_ref[...]
    h = (x * (1.0 + sc_ref[...]) + sh_ref[...]).astype(BF16)
    acc = None
    for c in range(0, wg_ref.shape[1], tf):
        g = _dot(h, wg_ref[:, c:c + tf])
        u = _dot(h, wu_ref[:, c:c + tf])
        a = (g * _sig(g) * u).astype(BF16)
        part = _dot(a, wd_ref[c:c + tf, :])
        acc = part if acc is None else acc + part
    r = alpha * x + (0.5 * gt_ref[...]) * acc
    o_ref[...] = _layer_norm(r, lg_ref[...], lb_ref[...])


def _ffn(x, mods, kind, rows_per_seq, k0, wg, wu, wd, lg, lb, alpha, tm, tf):
    n, d = x.shape
    tps = max(rows_per_seq // tm, 1)
    ms = lambda k: _mod_spec(kind, tm, d, tps, k)
    full = lambda a: pl.BlockSpec(a.shape, lambda i: (0, 0))
    return pl.pallas_call(
        functools.partial(_ffn_kernel, alpha=alpha, tf=tf),
        grid=(n // tm,),
        in_specs=[pl.BlockSpec((tm, d), lambda i: (i, 0)),
                  ms(k0), ms(k0 + 1), ms(k0 + 2), full(wg), full(wu), full(wd), full(lg), full(lb)],
        out_specs=pl.BlockSpec((tm, d), lambda i: (i, 0)),
        out_shape=jax.ShapeDtypeStruct((n, d), F32),
        compiler_params=_params("parallel"),
        name="ffn",
    )(x, mods, mods, mods, wg, wu, wd, lg, lb)


def _inproj_kernel(x_ref, sh_ref, sc_ref, wqkv, wz, wm, wsm, wk, wt,
                   oqkv, oz, om, osm, oks, okw, oqt, okct, okst, okwt):
    h = (x_ref[...] * (1.0 + sc_ref[...]) + sh_ref[...]).astype(BF16)
    oqkv[...] = _dot(h, wqkv[...])
    oz[...] = _dot(h, wz[...]).astype(BF16)
    om[...] = _sig(_dot(h, wm[...])).astype(BF16)
    osm[...] = _dot(h, wsm[...])
    kt = _dot(h, wk[...]).astype(BF16)
    oks[...] = kt[:, 0:KV_WIDTH]
    okw[...] = kt[:, KV_WIDTH:2 * KV_WIDTH]
    ft = _dot_nt(wt[...], h)
    oqt[...] = (ft[0:NSA_WIDTH] * (NSA_HEAD_DIM ** -0.5)).astype(BF16)
    okct[...] = ft[NSA_WIDTH:NSA_WIDTH + KV2]
    okst[...] = ft[NSA_WIDTH + KV2:NSA_WIDTH + 2 * KV2]
    okwt[...] = ft[NSA_WIDTH + 2 * KV2:NSA_WIDTH + 3 * KV2]


def _inproj(x, mods, kind, rows_per_seq, ws, wk, wt, tm):
    n, d = x.shape
    tps = max(rows_per_seq // tm, 1)
    widths = [w.shape[1] for w in ws] + [KV_WIDTH] * 2
    dtypes = [F32, BF16, BF16, F32, BF16, BF16]
    t_rows = [NSA_WIDTH, KV2, KV2, KV2]
    t_dtypes = [BF16, F32, F32, F32]
    if kind == "seq":
        t_specs = [pl.BlockSpec((None, r, tm), lambda i: (i // tps, 0, i % tps)) for r in t_rows]
        t_shapes = [jax.ShapeDtypeStruct((n // rows_per_seq, r, rows_per_seq), dt) for r, dt in zip(t_rows, t_dtypes)]
    else:
        t_specs = [pl.BlockSpec((r, tm), lambda i: (0, i)) for r in t_rows]
        t_shapes = [jax.ShapeDtypeStruct((r, n), dt) for r, dt in zip(t_rows, t_dtypes)]
    full = lambda w: pl.BlockSpec(w.shape, lambda i: (0, 0))
    return pl.pallas_call(
        _inproj_kernel,
        grid=(n // tm,),
        in_specs=[pl.BlockSpec((tm, d), lambda i: (i, 0)),
                  _mod_spec(kind, tm, d, tps, 3), _mod_spec(kind, tm, d, tps, 4)]
                 + [full(w) for w in ws] + [full(wk), full(wt)],
        out_specs=[pl.BlockSpec((tm, wd), lambda i: (i, 0)) for wd in widths] + t_specs,
        out_shape=[jax.ShapeDtypeStruct((n, wd), dt) for wd, dt in zip(widths, dtypes)] + t_shapes,
        compiler_params=_params("parallel"),
        name="inproj",
    )(x, mods, mods, *ws, wk, wt)


def _gdn_kernel(qkv_ref, z_ref, sm_ref, abt_ref, cb_ref, cw_ref, prow_ref, alog_ref, dtb_ref, nw_ref,
                s0_ref, tri_ref, triu_ref, bd_ref, expg_ref, expb_ref, o_ref, s_ref, xbuf,
                *, chunk, l_valid, l_pad):
    C = chunk
    HD = GDN_HEAD_DIM
    W = GDN_WIDTH
    NB = qkv_ref.shape[0]
    c = pl.program_id(1)

    @pl.when(c == 0)
    def _():
        xbuf[:, 0:SUBLANES, :] = cb_ref[...]
        s_ref[...] = s0_ref[...]

    cw = cw_ref[...]
    bd = bd_ref[...]
    hw = bd.shape[0]
    pr = prow_ref[...]
    qn_l, kn_l, gx_l, g_row_l, q_dec_l, k_dec_l, kb_l, vb_l, kbg_l, e_last_l, z_l = ([] for _ in range(11))
    for n in range(NB):
        x = qkv_ref[n]
        xbuf[n, SUBLANES:SUBLANES + C, :] = x
        y = (xbuf[n, 5:5 + C, :] * cw[0:1] + xbuf[n, 6:6 + C, :] * cw[1:2]
             + xbuf[n, 7:7 + C, :] * cw[2:3] + x * cw[3:4])
        xbuf[n, 0:SUBLANES, :] = x[C - SUBLANES:C, :]
        act = y * _sig(y)
        q = act[:, 0:W]
        k = act[:, W:2 * W]
        v = act[:, 2 * W:3 * W]

        sq = jnp.concatenate([q * q, k * k], 0)
        ss = jnp.concatenate([_dot_split2(sq[:, j:j + hw], bd) for j in range(0, W, hw)], 1)
        qn = q * lax.rsqrt(ss[:C] + 1e-6) * (HD ** -0.5)
        kn = k * lax.rsqrt(ss[C:] + 1e-6)

        sm = sm_ref[n]
        gcol = -jnp.exp(pr[0:1]) * _softplus(sm + pr[1:2])
        bcol = _sig(sm)
        ab = abt_ref[n]
        grow = -jnp.exp(alog_ref[...]) * _softplus(ab[0:GDN_HEADS] + dtb_ref[...])
        if l_pad != l_valid:
            vcol = (c * C + lax.broadcasted_iota(jnp.int32, (C, 1), 0) < l_valid).astype(F32)
            vrow = (c * C + lax.broadcasted_iota(jnp.int32, (1, C), 1) < l_valid).astype(F32)
            qn, kn, v = qn * vcol, kn * vcol, v * vcol
            gcol, bcol, grow = gcol * vcol, bcol * vcol, grow * vrow

        g_cum = jnp.dot(tri_ref[...], gcol, precision=HIGHEST, preferred_element_type=F32)
        gx = _dot_split2(g_cum, expg_ref[...])
        bx = _dot_split2(bcol, expb_ref[...])
        g_row = jnp.dot(grow, triu_ref[...], precision=HIGHEST, preferred_element_type=F32)

        eg = jnp.exp(gx)
        g_last = gx[C - 1:C, :]
        kb = kn * bx
        for lst, val in ((qn_l, qn), (kn_l, kn), (gx_l, gx), (g_row_l, g_row), (q_dec_l, qn * eg),
                         (k_dec_l, kn * jnp.exp(g_last - gx)), (kb_l, kb), (vb_l, v * bx), (kbg_l, kb * eg),
                         (e_last_l, jnp.exp(g_last)), (z_l, z_ref[n].astype(F32))):
            lst.append(val)

    ri = lax.broadcasted_iota(jnp.int32, (C, C), 0)
    ci = lax.broadcasted_iota(jnp.int32, (C, C), 1)
    incl = ri >= ci
    strict = ri > ci
    eye = (ri == ci).astype(F32)
    sh = min(C, SUBLANES).bit_length() - 1
    diag_blk = (ri >> sh) == (ci >> sh)
    merge_masks = []
    while (1 << sh) < C:
        merge_masks.append(((ri >> (sh + 1)) == (ci >> (sh + 1))) & ((ri >> sh) == (ci >> sh) + 1))
        sh += 1

    H = GDN_HEADS
    heads = range(NB * H)
    col = lambda xs, j: xs[j // H][:, (j % H) * HD:(j % H + 1) * HD]
    bfl = lambda xs: [x.astype(BF16) for x in xs]

    dec = [jnp.where(incl, jnp.exp(jnp.where(
        incl, gx_l[j // H][:, (j % H) * HD:(j % H) * HD + C] - g_row_l[j // H][j % H:j % H + 1, :], 0.0)), 0.0)
        for j in heads]
    r = [_dot_nt(jnp.concatenate([col(kb_l, h), col(qn_l, h)], 0).astype(BF16), col(kn_l, h).astype(BF16))
         for h in heads]
    a_kk = [jnp.where(strict, r[h][:C] * dec[h], 0.0) for h in heads]
    a_qk = bfl([r[h][C:] * dec[h] for h in heads])

    nd = [jnp.where(diag_blk, -a_kk[h], 0.0) for h in heads]
    ndb = bfl(nd)
    p2 = [_dot(ndb[h], ndb[h]) for h in heads]
    t = [eye + nd[h] for h in heads]
    r2 = [_dot(p2[h].astype(BF16), jnp.concatenate([t[h], p2[h]], 1).astype(BF16)) for h in heads]
    t = [t[h] + r2[h][:, :C] for h in heads]
    t = [t[h] + _dot(r2[h][:, C:].astype(BF16), t[h].astype(BF16)) for h in heads]
    for mm in merge_masks:
        tb = bfl(t)
        tl = bfl([_dot(tb[h], jnp.where(mm, a_kk[h], 0.0).astype(BF16)) for h in heads])
        t = [t[h] - _dot(tl[h], tb[h]) for h in heads]
    tb = bfl(t)

    rhs = [jnp.concatenate([col(vb_l, h), col(kbg_l, h)], 1) for h in heads]
    x0 = [_dot(tb[h], rhs[h].astype(BF16)) for h in heads]
    res = []
    for h in heads:
        ah = a_kk[h].astype(BF16)
        al = (a_kk[h] - ah.astype(F32)).astype(BF16)
        xh = x0[h].astype(BF16)
        xl = (x0[h] - xh.astype(F32)).astype(BF16)
        hh = _dot(ah, jnp.concatenate([xh, xl], 1))
        res.append(rhs[h] - x0[h] - (hh[:, :2 * HD] + hh[:, 2 * HD:] + _dot(al, xh)))
    uw = [x0[h] + _dot(tb[h], res[h].astype(BF16)) for h in heads]

    s_old = [s_ref[h // H, h % H] for h in heads]
    wq = [_dot(jnp.concatenate([uw[h][:, HD:], col(q_dec_l, h)], 0).astype(BF16), s_old[h].astype(BF16))
          for h in heads]
    v_new = bfl([uw[h][:, :HD] - wq[h][:C] for h in heads])
    o = [wq[h][C:] + _dot(a_qk[h], v_new[h]) for h in heads]
    for h in heads:
        s_ref[h // H, h % H] = (s_old[h] * col(e_last_l, h)
                                + _dot_tn(col(k_dec_l, h).astype(BF16), v_new[h]))

    nw = nw_ref[...]
    outs = []
    for h in heads:
        o_h = o[h] * lax.rsqrt(jnp.mean(o[h] * o[h], -1, keepdims=True) + 1e-6)
        zh = col(z_l, h)
        outs.append(o_h * nw * (zh * _sig(zh)))
    for n in range(NB):
        o_ref[n] = jnp.concatenate(outs[n * H:(n + 1) * H], 1).astype(BF16)


def _gdn(qkv, z, small, conv_buf, s0, conv_w, a_log, dt_bias, norm_w, l_valid, chunk):
    b, lp, w3 = qkv.shape
    C = chunk
    nc = lp // C
    H, HD, W = GDN_HEADS, GDN_HEAD_DIM, GDN_WIDTH
    abt = small[:, :, :2 * H].reshape(b, nc, C, 2 * H).transpose(0, 1, 3, 2)
    cb = jnp.pad(conv_buf, ((0, 0), (SUBLANES - (GDN_CONV - 1), 0), (0, 0)))
    prow = jnp.zeros((2, LANES), F32).at[0, :H].set(a_log).at[1, :H].set(dt_bias)
    alog_r = jnp.broadcast_to(a_log[:, None], (H, C))
    dtb_r = jnp.broadcast_to(dt_bias[:, None], (H, C))
    ix = jnp.arange(C)
    tri = (ix[:, None] >= ix[None, :]).astype(F32)
    triu = tri.T
    hid = jnp.arange(W) // HD
    hw = 2 * LANES
    bd = (hid[:hw, None] == hid[None, :hw]).astype(BF16)
    lane = jnp.arange(LANES)
    expg = (lane[:, None] == hid[None, :]).astype(BF16)
    expb = (lane[:, None] == hid[None, :] + H).astype(BF16)
    const = lambda shape: pl.BlockSpec(shape, lambda i, j: (0,) * len(shape))
    nb = GDN_SEQS_PER_STEP if b % GDN_SEQS_PER_STEP == 0 else 1
    o, s_fin = pl.pallas_call(
        functools.partial(_gdn_kernel, chunk=C, l_valid=l_valid, l_pad=lp),
        grid=(b // nb, nc),
        in_specs=[pl.BlockSpec((nb, C, w3), lambda i, j: (i, j, 0)),
                  pl.BlockSpec((nb, C, W), lambda i, j: (i, j, 0)),
                  pl.BlockSpec((nb, C, LANES), lambda i, j: (i, j, 0)),
                  pl.BlockSpec((nb, None, 2 * H, C), lambda i, j: (i, j, 0, 0)),
                  pl.BlockSpec((nb, SUBLANES, w3), lambda i, j: (i, 0, 0)),
                  const((GDN_CONV, w3)), const((2, LANES)), const((H, C)), const((H, C)), const((1, HD)),
                  pl.BlockSpec((nb, H, HD, HD), lambda i, j: (i, 0, 0, 0)),
                  const((C, C)), const((C, C)), const((hw, hw)), const((LANES, W)), const((LANES, W))],
        out_specs=[pl.BlockSpec((nb, C, W), lambda i, j: (i, j, 0)),
                   pl.BlockSpec((nb, H, HD, HD), lambda i, j: (i, 0, 0, 0))],
        out_shape=[jax.ShapeDtypeStruct((b, lp, W), BF16), jax.ShapeDtypeStruct((b, H, HD, HD), F32)],
        scratch_shapes=[pltpu.VMEM((nb, SUBLANES + C, w3), F32)],
        compiler_params=_params("parallel", "arbitrary"),
        name="gdn",
    )(qkv, z, small, abt, cb, conv_w, prow, alog_r, dtb_r, norm_w.reshape(1, HD), s0,
      tri, triu, bd, expg, expb)
    return o, s_fin


def _pool_tile(x, w, pm):
    t = x.shape[1]
    wt = jnp.concatenate([w] * (t // LANES), 1) if t > LANES else w
    return _dot_split2(x * wt, pm)


def _pool_kernel(x_ref, w_ref, pm_ref, o_ref):
    for n in range(x_ref.shape[0]):
        o_ref[n] = _pool_tile(x_ref[n], w_ref[...], pm_ref[...])


def _pool_matrix(t):
    nb = -(-(t // CMP_BLOCK) // LANES) * LANES
    return (jnp.arange(t)[:, None] // CMP_BLOCK == jnp.arange(nb)[None, :]).astype(BF16)


def _pool(kvt, wrow):
    b, rows, t = kvt.shape
    pm = _pool_matrix(t)
    nb = SUBLANES if (t <= LANES and b % SUBLANES == 0) else 1
    return pl.pallas_call(
        _pool_kernel,
        grid=(b // nb,),
        in_specs=[pl.BlockSpec((nb, rows, t), lambda i: (i, 0, 0)),
                  pl.BlockSpec(wrow.shape, lambda i: (0, 0)),
                  pl.BlockSpec(pm.shape, lambda i: (0, 0))],
        out_specs=pl.BlockSpec((nb, rows, pm.shape[1]), lambda i: (i, 0, 0)),
        out_shape=jax.ShapeDtypeStruct((b, rows, pm.shape[1]), F32),
        compiler_params=_params("parallel"),
        name="pool",
    )(kvt, wrow, pm)


def _pool_paged_kernel(pt_ref, *refs, pages):
    del pt_ref
    w_ref, pm_ref, o_ref = refs[pages:]
    outs = []
    for c in range(0, pages, PAGES_PER_TILE):
        x = jnp.concatenate([refs[p][...] for p in range(c, c + PAGES_PER_TILE)], 1)
        outs.append(_pool_tile(x, w_ref[...], pm_ref[...]))
    o_ref[...] = jnp.concatenate(outs, 1)


def _pool_paged(cache_t, page_table, wrow, pages):
    _, rows, ps = cache_t.shape
    b, n_pages = page_table.shape
    assert ps == PAGE_SIZE and pages % PAGES_PER_TILE == 0 and n_pages % pages == 0
    pm = _pool_matrix(PAGES_PER_TILE * ps)
    out_lanes = pages * ps // CMP_BLOCK
    page_spec = lambda p: pl.BlockSpec((None, rows, ps), lambda i, j, pt: (pt[i, j * pages + p], 0, 0))
    return pl.pallas_call(
        functools.partial(_pool_paged_kernel, pages=pages),
        grid_spec=pltpu.PrefetchScalarGridSpec(
            num_scalar_prefetch=1,
            grid=(b, n_pages // pages),
            in_specs=[page_spec(p) for p in range(pages)]
                     + [pl.BlockSpec(wrow.shape, lambda i, j, pt: (0, 0)),
                        pl.BlockSpec(pm.shape, lambda i, j, pt: (0, 0))],
            out_specs=pl.BlockSpec((None, rows, out_lanes), lambda i, j, pt: (i, 0, j))),
        out_shape=jax.ShapeDtypeStruct((b, rows, n_pages * ps // CMP_BLOCK), F32),
        compiler_params=_params("parallel", "arbitrary"),
        name="pool_paged",
    )(page_table, *([cache_t] * pages), wrow, pm)


def _cmp_probs(sc, pos, tq):
    n = sc.shape[1]
    lane = lax.broadcasted_iota(jnp.int32, (1, n), 1)
    maskc = ((lane + 1) * CMP_BLOCK - 1) <= pos
    ps = []
    imp = None
    for g in range(NSA_GROUP):
        s = jnp.where(maskc, sc[g * tq:(g + 1) * tq], NEG)
        m = jnp.max(s, -1, keepdims=True)
        e = jnp.where(maskc, jnp.exp(s - m), 0.0)
        den = jnp.sum(e, -1, keepdims=True)
        p = e / jnp.where(den > 0.0, den, 1.0)
        ps.append(p)
        imp = p if imp is None else imp + p
    return jnp.concatenate(ps, 0), imp


def _select_blocks(imp, pos, axis):
    n = imp.shape[axis]
    idx = lax.broadcasted_iota(jnp.int32, (n, 1) if axis == 0 else (1, n), axis)
    even = (idx & 1) == 0
    imp2 = imp + jnp.where(even, pltpu.roll(imp, n - 1, axis), pltpu.roll(imp, 1, axis))
    blk = idx >> 1
    valid = blk * SEL_BLOCK <= pos
    cur = pos >> 6
    forced = (blk == 0) | (blk == cur) | (blk == cur - 1)
    score = jnp.where(valid, jnp.where(forced, BIG, imp2), -1.0)
    work = jnp.where(even, score, -2.0)
    idxf = idx.astype(F32)
    sel = jnp.zeros(work.shape, F32)
    for _ in range(SEL_TOPK):
        m = jnp.max(work, axis, keepdims=True)
        first = jnp.min(jnp.where(work == m, idxf, 1e9), axis, keepdims=True)
        pick = idxf == first
        sel = jnp.where(pick, 1.0, sel)
        work = jnp.where(pick, -2.0, work)
    sel = jnp.where(score >= 0.0, sel, 0.0)
    return sel + pltpu.roll(sel, 1, axis)


def _online_update(carry, s, mf, vt):
    m, l, acc = carry
    s = jnp.where(mf > 0.5, s, NEG)
    m_new = jnp.maximum(m, jnp.max(s, -1, keepdims=True))
    alpha = jnp.exp(m - m_new)
    p = jnp.exp(s - m_new) * mf
    l = alpha * l + jnp.sum(p, -1, keepdims=True)
    acc = alpha * acc + _dot_nt(p.astype(BF16), vt)
    return m_new, l, acc


def _expand_matrix(n_blocks, n_keys):
    return (jnp.arange(n_keys)[None, :] // CMP_BLOCK == jnp.arange(n_blocks)[:, None]).astype(BF16)


def _nsa_prompt_kernel(qt_ref, ks_ref, kw_ref, kvb_ref, kvs_ref, kvw_ref, sm_ref, et_ref, o_ref, *, n_rows):
    TQ, TK, G, HD, KV = NSA_TQ, NSA_TK, NSA_GROUP, NSA_HEAD_DIM, NSA_KV_HEADS
    i = pl.program_id(1)
    t0 = i * TQ
    pos = t0 + lax.broadcasted_iota(jnp.int32, (1, TQ), 1)
    gst = _sig(sm_ref[...]).T
    gate_row = lambda br, h, g: 2 * GDN_HEADS + br * NSA_Q_HEADS + h * G + g

    n_wt = WINDOW // TQ + 1
    w_off = [pl.multiple_of(jnp.maximum(i - (n_wt - 1) + j, 0) * TQ, TQ) for j in range(n_wt)]
    cidx = lax.broadcasted_iota(jnp.int32, (WINDOW + TQ, 1), 0)
    diff = lax.broadcasted_iota(jnp.int32, (1, TQ), 1) + WINDOW - cidx
    bw = jnp.where((diff >= 0) & (diff < WINDOW) & (t0 - WINDOW + cidx >= 0), 0.0, NEG)
    kidx = lax.broadcasted_iota(jnp.int32, (TK, 1), 0)
    n_used = (t0 + TQ + TK - 1) // TK

    krow = lambda h: slice(h * HD, (h + 1) * HD)
    vrow = lambda h: slice((KV + h) * HD, (KV + h + 1) * HD)
    zeros = jnp.zeros((HD, TQ), BF16)
    blk = lax.broadcasted_iota(jnp.int32, (kvb_ref.shape[1], 1), 0)
    maskc = ((blk + 1) * CMP_BLOCK - 1) <= pos
    kw_all = jnp.concatenate([kw_ref[pl.ds(o, TQ), :] for o in w_off], 0)
    qbd, oc, ow, selt = [], [], [], []
    for h in range(KV):
        qh = [qt_ref[(h * G + g) * HD:(h * G + g + 1) * HD, :] for g in range(G)]
        qbd.append([jnp.concatenate([zeros] * h + [q] + [zeros] * (KV - 1 - h), 0) for q in qh])

        kc = kvb_ref[krow(h), :].astype(BF16)
        vc = kvb_ref[vrow(h), :].astype(BF16)
        imp = None
        oc.append([])
        for g in range(G):
            s = jnp.where(maskc, _dot_tn(kc, qh[g]), NEG)
            e = jnp.where(maskc, jnp.exp(s - jnp.max(s, 0, keepdims=True)), 0.0)
            den = jnp.sum(e, 0, keepdims=True)
            p = e / jnp.where(den > 0.0, den, 1.0)
            imp = p if imp is None else imp + p
            oc[h].append(_dot(vc, p.astype(BF16)))
        selt.append(_select_blocks(imp[:n_rows], pos, 0).astype(BF16))

        vw = jnp.concatenate([kvw_ref[vrow(h), pl.ds(o, TQ)] for o in w_off], 1).astype(BF16)
        vw = jnp.concatenate([vw, jnp.ones_like(vw)], 0)
        ow.append([])
        for g in range(G):
            s = _dot(kw_all, qbd[h][g]) + bw
            e = jnp.exp(s - jnp.max(s, 0, keepdims=True))
            r = _dot(vw, e.astype(BF16))
            ow[h].append(r[:HD] / r[HD:HD + 1])

    def scores(kt):
        off = pl.multiple_of(kt * TK, TK)
        k_t = ks_ref[pl.ds(off, TK), :]
        e_t = et_ref[pl.ds(off, TK), :]
        causal = off + kidx <= pos
        out = []
        for h in range(KV):
            bias = jnp.where((_dot(e_t, selt[h]) > 0.5) & causal, 0.0, NEG)
            out += [_dot(k_t, qbd[h][g]) + bias for g in range(G)]
        return out

    def values(kt):
        off = pl.multiple_of(kt * TK, TK)
        vs = [kvs_ref[vrow(h), pl.ds(off, TK)].astype(BF16) for h in range(KV)]
        return [jnp.concatenate([v_t, jnp.ones_like(v_t)], 0) for v_t in vs]

    def body(kt, carry):
        s_all = scores(kt)
        v_prev = values(jnp.maximum(kt - 1, 0))
        out = []
        for h in range(KV):
            for g in range(G):
                m, acc, p_prev = carry[h * G + g]
                acc = acc + _dot(v_prev[h], p_prev)
                s = s_all[h * G + g]
                m_new = jnp.maximum(m, jnp.max(s, 0, keepdims=True))
                out.append((m_new, jnp.exp(m - m_new) * acc, jnp.exp(s - m_new).astype(BF16)))
        return tuple(out)

    init = tuple((jnp.full((1, TQ), NEG, F32), jnp.zeros((2 * HD, TQ), F32), jnp.zeros((TK, TQ), BF16))
                 for _ in range(KV * G))
    res = lax.fori_loop(0, n_used, body, init)
    v_last = values(n_used - 1)

    outs = []
    for h in range(KV):
        for g in range(G):
            _, acc, p_last = res[h * G + g]
            acc = acc + _dot(v_last[h], p_last)
            gate = lambda br: gst[gate_row(br, h, g):gate_row(br, h, g) + 1, :]
            outs.append(gate(0) * oc[h][g] + gate(1) * (acc[:HD] / acc[HD:HD + 1]) + gate(2) * ow[h][g])
    o_ref[...] = jnp.concatenate(outs, 0).astype(BF16)


def _nsa_prompt(q_t, ks, kw, kvb_t, kvs_t, kvw_t, small):
    b, _, l = q_t.shape
    TQ, TK = NSA_TQ, NSA_TK
    n_cmp = l // CMP_BLOCK
    assert kvb_t.shape[2] == LANES and n_cmp <= LANES and l % TK == 0 and WINDOW % TQ == 0
    n_rows = -(-n_cmp // SUBLANES) * SUBLANES
    emat_t = _expand_matrix(n_rows, l).T
    slab = lambda t: pl.BlockSpec((None,) + t.shape[1:], lambda bi, i: (bi, 0, 0))
    return pl.pallas_call(
        functools.partial(_nsa_prompt_kernel, n_rows=n_rows),
        grid=(b, l // TQ),
        in_specs=[pl.BlockSpec((None, NSA_WIDTH, TQ), lambda bi, i: (bi, 0, i)),
                  slab(ks), slab(kw), slab(kvb_t), slab(kvs_t), slab(kvw_t),
                  pl.BlockSpec((None, TQ, LANES), lambda bi, i: (bi, i, 0)),
                  pl.BlockSpec((l, n_rows), lambda bi, i: (0, 0))],
        out_specs=pl.BlockSpec((None, NSA_WIDTH, TQ), lambda bi, i: (bi, 0, i)),
        out_shape=jax.ShapeDtypeStruct((b, NSA_WIDTH, l), BF16),
        compiler_params=_params("parallel", "arbitrary"),
        name="nsa_prompt",
    )(q_t, ks, kw, kvb_t, kvs_t, kvw_t, small, emat_t)


LS = SUBLANES
ROWS_H = NSA_GROUP * LS


def _row_token(rows):
    return lax.broadcasted_iota(jnp.int32, (rows, 1), 0) & (LS - 1)


def _nsa_sample_a_kernel(q_ref, kvb_ref, cw_ref, new_ref, oc_ref, ow_ref, sel_ref, *, past, l_new):
    HD, KV = NSA_HEAD_DIM, NSA_KV_HEADS
    NB = q_ref.shape[0]
    pos = past + lax.broadcasted_iota(jnp.int32, (LS, 1), 0)
    wb = cw_ref.shape[2]
    tok = _row_token(ROWS_H)
    c1 = lax.broadcasted_iota(jnp.int32, (1, wb), 1)
    d1 = tok + wb - c1
    m1 = jnp.where((d1 >= 0) & (d1 < WINDOW) & (past - wb + c1 >= 0), 1.0, 0.0)
    c2 = lax.broadcasted_iota(jnp.int32, (1, new_ref.shape[2]), 1)
    d2 = tok - c2
    m2 = jnp.where((d2 >= 0) & (d2 < WINDOW) & (c2 < l_new), 1.0, 0.0)
    imps = []
    for n in range(NB):
        for h in range(KV):
            krow = slice(h * HD, (h + 1) * HD)
            vrow = slice((KV + h) * HD, (KV + h + 1) * HD)
            qh = q_ref[n, h]
            p, imp = _cmp_probs(_dot(qh, kvb_ref[n, krow, :].astype(BF16)), pos, LS)
            imps.append(imp)
            oc_ref[n, h] = _dot_nt(p.astype(BF16), kvb_ref[n, vrow, :].astype(BF16))

            s1 = jnp.where(m1 > 0.5, _dot(qh, cw_ref[n, krow, :].astype(BF16)), NEG)
            s2 = jnp.where(m2 > 0.5, _dot(qh, new_ref[n, krow, :].astype(BF16)), NEG)
            m = jnp.maximum(jnp.max(s1, -1, keepdims=True), jnp.max(s2, -1, keepdims=True))
            e1 = jnp.exp(s1 - m) * m1
            e2 = jnp.exp(s2 - m) * m2
            den = jnp.sum(e1, -1, keepdims=True) + jnp.sum(e2, -1, keepdims=True)
            den = jnp.where(den > 0.0, den, 1.0)
            ow_ref[n, h] = (_dot_nt((e1 / den).astype(BF16), cw_ref[n, vrow, :].astype(BF16))
                            + _dot_nt((e2 / den).astype(BF16), new_ref[n, vrow, :].astype(BF16)))
    sel = _select_blocks(jnp.concatenate(imps, 0), jnp.concatenate([pos] * (NB * KV), 0), 1)
    for n in range(NB):
        for h in range(KV):
            sel_ref[n, h] = sel[(n * KV + h) * LS:(n * KV + h + 1) * LS]


def _nsa_sample_a(q_rows, kvb_t, cache_win_t, kvw_new_t, past, l_new):
    b, _, ncp = kvb_t.shape
    nb = SAMPLE_SEQS_PER_STEP if b % SAMPLE_SEQS_PER_STEP == 0 else 1
    per_seq = lambda t: pl.BlockSpec((nb,) + t.shape[1:], lambda i: (i,) + (0,) * (t.ndim - 1))
    out_rows = jax.ShapeDtypeStruct(q_rows.shape, F32)
    sel_shape = jax.ShapeDtypeStruct((b, NSA_KV_HEADS, LS, ncp), F32)
    return pl.pallas_call(
        functools.partial(_nsa_sample_a_kernel, past=past, l_new=l_new),
        grid=(b // nb,),
        in_specs=[per_seq(q_rows), per_seq(kvb_t), per_seq(cache_win_t), per_seq(kvw_new_t)],
        out_specs=[per_seq(out_rows), per_seq(out_rows), per_seq(sel_shape)],
        out_shape=[out_rows, out_rows, sel_shape],
        compiler_params=_params("parallel"),
        name="nsa_sample_a",
    )(q_rows, kvb_t, cache_win_t, kvw_new_t)


def _nsa_sample_sel_kernel(pt_ref, *refs, pages, l_new):
    del pt_ref
    page_refs = refs[:pages]
    q_ref, sel_ref, tail_ref, new_ref, e_ref, oc_ref, ow_ref, gate_ref, o_ref, m_scr, l_scr, acc_scr = refs[pages:]
    HD, KV, G = NSA_HEAD_DIM, NSA_KV_HEADS, NSA_GROUP
    i = pl.program_id(1)
    last = i == pl.num_programs(1) - 1

    @pl.when(i == 0)
    def _():
        m_scr[...] = jnp.full_like(m_scr, NEG)
        l_scr[...] = jnp.zeros_like(l_scr)
        acc_scr[...] = jnp.zeros_like(acc_scr)

    kv = jnp.concatenate([r[...] for r in page_refs], 1)
    emat = e_ref[...]
    for h in range(KV):
        krow = slice(h * HD, (h + 1) * HD)
        vrow = slice((KV + h) * HD, (KV + h + 1) * HD)
        selh = sel_ref[h].astype(BF16)
        mh = jnp.concatenate([_dot(selh[:, c:c + LANES], emat) for c in range(0, selh.shape[1], LANES)], 1)
        carry = _online_update((m_scr[h], l_scr[h], acc_scr[h]),
                               _dot(q_ref[h], kv[krow].astype(BF16)),
                               jnp.concatenate([mh] * G, 0), kv[vrow].astype(BF16))
        m_scr[h], l_scr[h], acc_scr[h] = carry

    @pl.when(last)
    def _():
        tok = _row_token(ROWS_H)
        c2 = lax.broadcasted_iota(jnp.int32, (1, new_ref.shape[1]), 1)
        for h in range(KV):
            krow = slice(h * HD, (h + 1) * HD)
            vrow = slice((KV + h) * HD, (KV + h + 1) * HD)
            flag = jnp.concatenate([tail_ref[h][:, 0:1]] * G, 0)
            mt = jnp.where((c2 <= tok) & (c2 < l_new) & (flag > 0.5), 1.0, 0.0)
            _, l, acc = _online_update((m_scr[h], l_scr[h], acc_scr[h]),
                                       _dot(q_ref[h], new_ref[krow, :].astype(BF16)), mt,
                                       new_ref[vrow, :].astype(BF16))
            osel = acc / jnp.where(l > 0.0, l, 1.0)
            gs = _sig(gate_ref[h])
            o_ref[h] = gs[:, 0:1] * oc_ref[h] + gs[:, 1:2] * osel + gs[:, 2:3] * ow_ref[h]


def _nsa_sample_sel(q_rows, selmask, cache_sel_t, page_table, kvs_new_t, oc, ow, gate_rows, pages, l_new):
    b, n_pages = page_table.shape
    _, rows, ps = cache_sel_t.shape
    lanes_per_step = pages * ps // CMP_BLOCK
    assert lanes_per_step % LANES == 0 and n_pages % pages == 0
    n_steps = n_pages // pages
    emat = _expand_matrix(LANES, LANES * CMP_BLOCK)
    page_spec = lambda p: pl.BlockSpec((None, rows, ps), lambda bi, i, pt: (pt[bi, i * pages + p], 0, 0))
    per_seq = lambda t: pl.BlockSpec((None,) + t.shape[1:], lambda bi, i, pt: (bi,) + (0,) * (t.ndim - 1))
    tail_block = n_steps * lanes_per_step // LANES
    kvh = NSA_KV_HEADS
    return pl.pallas_call(
        functools.partial(_nsa_sample_sel_kernel, pages=pages, l_new=l_new),
        grid_spec=pltpu.PrefetchScalarGridSpec(
            num_scalar_prefetch=1,
            grid=(b, n_steps),
            in_specs=[page_spec(p) for p in range(pages)] + [
                per_seq(q_rows),
                pl.BlockSpec((None, kvh, LS, lanes_per_step), lambda bi, i, pt: (bi, 0, 0, i)),
                pl.BlockSpec((None, kvh, LS, LANES), lambda bi, i, pt: (bi, 0, 0, tail_block)),
                per_seq(kvs_new_t),
                pl.BlockSpec(emat.shape, lambda bi, i, pt: (0, 0)),
                per_seq(oc), per_seq(ow), per_seq(gate_rows)],
            out_specs=per_seq(oc),
            scratch_shapes=[pltpu.VMEM((kvh, ROWS_H, 1), F32), pltpu.VMEM((kvh, ROWS_H, 1), F32),
                            pltpu.VMEM((kvh, ROWS_H, NSA_HEAD_DIM), F32)]),
        out_shape=jax.ShapeDtypeStruct(oc.shape, F32),
        compiler_params=_params("parallel", "arbitrary"),
        name="nsa_sample_sel",
    )(page_table, *([cache_sel_t] * pages), q_rows, selmask, selmask, kvs_new_t, emat, oc, ow, gate_rows)


def _nsa_sample(q_t, kvc_t, kvs_t, kvw_t, gate, cache_cmp_t, cache_sel_t, cache_win_t, page_table, wrow):
    b, _, l = q_t.shape
    KV, G, HD = NSA_KV_HEADS, NSA_GROUP, NSA_HEAD_DIM
    n_pages = page_table.shape[1]
    ps = cache_cmp_t.shape[2]
    past = n_pages * ps
    assert l <= LS and ps == PAGE_SIZE
    pages = min(PAGES_PER_STEP, n_pages)
    pad_new = lambda t: jnp.pad(t, ((0, 0), (0, 0), (0, ps - l)))
    kvb_t = jnp.concatenate([_pool_paged(cache_cmp_t, page_table, wrow, pages), _pool(pad_new(kvc_t), wrow)], 2)
    q_rows = jnp.pad(q_t.reshape(b, KV, G, HD, l).transpose(0, 1, 2, 4, 3),
                     ((0, 0),) * 3 + ((0, LS - l), (0, 0))).reshape(b, KV, ROWS_H, HD)
    oc, ow, selmask = _nsa_sample_a(q_rows, kvb_t, cache_win_t, pad_new(kvw_t), past, l)
    gate_rows = jnp.pad(gate.reshape(b, l, 3, KV, G).transpose(0, 3, 4, 1, 2),
                        ((0, 0),) * 3 + ((0, LS - l), (0, 0))).reshape(b, KV, ROWS_H, 3)
    o = _nsa_sample_sel(q_rows, selmask, cache_sel_t, page_table, pad_new(kvs_t), oc, ow, gate_rows, pages, l)
    o = o.reshape(b, KV, G, LS, HD)[:, :, :, :l]
    return o.transpose(0, 3, 1, 2, 4).reshape(b, l, NSA_WIDTH).astype(BF16)


def _merge_kernel(x_ref, gt_ref, og_ref, on_ref, m_ref, wbg_ref, wbn_ref, wo_ref, lg_ref, lb_ref, o_ref,
                  *, alpha, nsa_feature_major):
    d = x_ref.shape[1]
    m = m_ref[...].astype(F32)
    nsa_dot = _dot_tn if nsa_feature_major else _dot
    mix = m[:, :d] * _dot(og_ref[...], wbg_ref[...]) + m[:, d:] * nsa_dot(on_ref[...], wbn_ref[...])
    y = _dot(mix.astype(BF16), wo_ref[...])
    o_ref[...] = _layer_norm(alpha * x_ref[...] + gt_ref[...] * y, lg_ref[...], lb_ref[...])


def _merge(x, mods, kind, rows_per_seq, o_gdn, o_nsa, msig, wbg, wbn, wo, lg, lb, alpha, tm):
    n, d = x.shape
    tps = max(rows_per_seq // tm, 1)
    row = lambda wd: pl.BlockSpec((tm, wd), lambda i: (i, 0))
    full = lambda a: pl.BlockSpec(a.shape, lambda i: (0, 0))
    feature_major = o_nsa.ndim == 3
    nsa_spec = (pl.BlockSpec((None, o_nsa.shape[1], tm), lambda i: (i // tps, 0, i % tps)) if feature_major
                else row(o_nsa.shape[1]))
    return pl.pallas_call(
        functools.partial(_merge_kernel, alpha=alpha, nsa_feature_major=feature_major),
        grid=(n // tm,),
        in_specs=[row(d), _mod_spec(kind, tm, d, tps, 5), row(o_gdn.shape[1]), nsa_spec,
                  row(2 * d), full(wbg), full(wbn), full(wo), full(lg), full(lb)],
        out_specs=row(d),
        out_shape=jax.ShapeDtypeStruct((n, d), F32),
        compiler_params=_params("parallel"),
        name="merge",
    )(x, mods, o_gdn, o_nsa, msig, wbg, wbn, wo, lg, lb)


def _row_tile(n, pref):
    t = min(pref, n)
    while n % t:
        t //= 2
    return t


def _ff_tile(dff):
    for parts in (11, 4, 2, 1, 22):
        if dff % parts == 0 and (dff // parts) % LANES == 0:
            return dff // parts
    return dff


def _to_rows(kv_t):
    b, _, t = kv_t.shape
    return kv_t.reshape(b, 2, NSA_KV_HEADS, NSA_HEAD_DIM, t).transpose(0, 4, 1, 2, 3)


def _to_feature_major(kv_rows):
    n, t = kv_rows.shape[:2]
    return kv_rows.transpose(0, 2, 3, 4, 1).reshape(n, KV2, t)


def _layer(x, mod, kind, past, lw, alpha):
    b, l, d = x.shape
    n = b * l
    (wg1, wu1, wd1, wg2, wu2, wd2, w_in_parts, wk, wt, conv_w, a_log, dt_bias, norm_w, wrow,
     wbg, wbn, wo, ln_g, ln_b) = lw
    tm = _row_tile(l if kind == "seq" else n, 512)
    tm_in = _row_tile(l if kind == "seq" else n, 256)
    tf = _ff_tile(wg1.shape[1])
    tm_ff = _row_tile(l if kind == "seq" else n, 512)
    lg = lambda i: ln_g[i].reshape(1, d)
    lb = lambda i: ln_b[i].reshape(1, d)

    x1 = _ffn(x.reshape(n, d), mod, kind, l, 0, wg1, wu1, wd1, lg(0), lb(0), alpha, tm_ff, tf)

    qkv, z, msig, small, ks, kw, q_t, kvc_t, kvs_t, kvw_t = _inproj(x1, mod, kind, l, w_in_parts, wk, wt, tm_in)
    seq = lambda t: t.reshape(b, l, t.shape[-1])
    qkv, z, small, ks, kw = [seq(t) for t in (qkv, z, small, ks, kw)]
    if kind == "tok":
        q_t, kvc_t, kvs_t, kvw_t = [t.reshape(t.shape[0], b, l).transpose(1, 0, 2)
                                    for t in (q_t, kvc_t, kvs_t, kvw_t)]

    if past is None:
        conv_buf = jnp.zeros((b, GDN_CONV - 1, 3 * GDN_WIDTH), F32)
        s0 = jnp.zeros((b, GDN_HEADS, GDN_HEAD_DIM, GDN_HEAD_DIM), F32)
        o_nsa = _nsa_prompt(q_t, ks, kw, _pool(kvc_t, wrow), kvs_t, kvw_t, small)
        win_t = kvw_t[:, :, l - min(WINDOW, l):]
    else:
        s0, conv_buf, cache_cmp, cache_sel, cache_win, page_table = past
        cache_win_t = _to_feature_major(cache_win)
        gate = small[:, :, 2 * GDN_HEADS:2 * GDN_HEADS + 3 * NSA_Q_HEADS]
        o_nsa = _nsa_sample(q_t, kvc_t, kvs_t, kvw_t, gate, _to_feature_major(cache_cmp),
                            _to_feature_major(cache_sel), cache_win_t, page_table, wrow)
        o_nsa = o_nsa.reshape(n, NSA_WIDTH)
        win_t = jnp.concatenate([cache_win_t, kvw_t], 2)[:, :, l:]

    chunk = min(GDN_CHUNK, -(-l // SUBLANES) * SUBLANES)
    lp = -(-l // chunk) * chunk
    padl = lambda t: jnp.pad(t, ((0, 0), (0, lp - l), (0, 0)))
    o_gdn, s_new = _gdn(padl(qkv), padl(z), padl(small), conv_buf, s0, conv_w, a_log, dt_bias, norm_w, l, chunk)
    conv_new = jnp.concatenate([conv_buf, qkv], 1)[:, -(GDN_CONV - 1):]

    x2 = _merge(x1, mod, kind, l, o_gdn[:, :l].reshape(n, GDN_WIDTH), o_nsa, msig,
                wbg, wbn, wo, lg(1), lb(1), alpha, tm)
    x3 = _ffn(x2, mod, kind, l, 6, wg2, wu2, wd2, lg(2), lb(2), alpha, tm_ff, tf)
    return x3.reshape(b, l, d), (s_new, conv_new, _to_rows(kvc_t), _to_rows(kvs_t), _to_rows(win_t))


def _split_w_in(w_in, d):
    splits = (3 * GDN_WIDTH, GDN_WIDTH, GDN_HEADS, GDN_HEADS, NSA_WIDTH, KV2, KV2, KV2, 3 * NSA_Q_HEADS, 2 * d)
    offs = [0]
    for s in splits:
        offs.append(offs[-1] + s)
    qkv, z, a, bb, q, kc, ks, kw, gate, merge = [w_in[:, offs[i]:offs[i + 1]] for i in range(len(splits))]
    n_small = 2 * GDN_HEADS + 3 * NSA_Q_HEADS
    small = jnp.pad(jnp.concatenate([a, bb, gate], 1), ((0, 0), (0, LANES - n_small)))
    wk = jnp.concatenate([ks[:, :KV_WIDTH], kw[:, :KV_WIDTH]], 1)
    wt = jnp.concatenate([q, kc, ks, kw], 1).T
    return [t.astype(BF16) for t in (qkv, z, merge, small)], wk.astype(BF16), wt.astype(BF16)


def kernel(x_prompt, x_sample, c_prompt, c_sample, state_gdn, state_gdn_conv, cache_cmp_kv, cache_sel_kv, cache_win_kv, page_table, ln_g, ln_b, w_ada, b_ada, w_ff1_gu, w_ff1_dn, w_ff2_gu, w_ff2_dn, w_in, gdn_conv_w, gdn_a_log, gdn_dt_bias, gdn_norm_w, nsa_w_cmp, w_br_gdn, w_br_nsa, w_out):
    depth = w_in.shape[0]
    alpha = (2.0 * depth) ** 0.25
    bp, lp, d = x_prompt.shape
    bs, ls, _ = x_sample.shape
    y_p, y_s = x_prompt, x_sample
    p_st, s_st = [], []
    for l in range(depth):
        dff = w_ff1_dn.shape[1]
        bf = lambda t: t.astype(BF16)
        wrow = jnp.tile(jnp.broadcast_to(nsa_w_cmp[l].transpose(0, 2, 1)[:, :, None, :],
                                         (2, NSA_KV_HEADS, NSA_HEAD_DIM, CMP_BLOCK)).reshape(KV2, CMP_BLOCK),
                        (1, LANES // CMP_BLOCK))
        w_in_parts, wk, wt = _split_w_in(w_in[l], d)
        lw = (bf(w_ff1_gu[l][:, :dff]), bf(w_ff1_gu[l][:, dff:]), bf(w_ff1_dn[l]),
              bf(w_ff2_gu[l][:, :dff]), bf(w_ff2_gu[l][:, dff:]), bf(w_ff2_dn[l]),
              w_in_parts, wk, wt, gdn_conv_w[l], gdn_a_log[l], gdn_dt_bias[l], gdn_norm_w[l], wrow,
              bf(w_br_gdn[l]), bf(w_br_nsa[l]), bf(w_out[l]), ln_g[l], ln_b[l])
        c_all = jnp.concatenate([c_prompt, c_sample], 0)
        r = c_all.shape[0]
        rp = -(-r // SUBLANES) * SUBLANES
        mod = _ada(jnp.pad(c_all, ((0, rp - r), (0, 0))), w_ada[l], b_ada[l])
        mod_p = mod[:bp].reshape(bp * 9, 1, d)
        mod_s = jnp.repeat(mod[bp:bp + bs], ls, axis=0)
        y_p, st_p = _layer(y_p, mod_p, "seq", None, lw, alpha)
        past = (state_gdn[l], state_gdn_conv[l], cache_cmp_kv[l], cache_sel_kv[l], cache_win_kv[l], page_table)
        y_s, st_s = _layer(y_s, mod_s, "tok", past, lw, alpha)
        p_st.append(st_p)
        s_st.append(st_s)
    p_out = [jnp.stack(t) for t in zip(*p_st)]
    s_out = [jnp.stack(t) for t in zip(*s_st)]
    return (y_p, y_s, *p_out, *s_out)
```

```python
import functools

import jax
import jax.numpy as jnp
from jax import lax
from jax.experimental import pallas as pl
from jax.experimental.pallas import tpu as pltpu

F32 = jnp.float32
BF16 = jnp.bfloat16
HIGHEST = lax.Precision.HIGHEST

GDN_HEADS = 8
GDN_HEAD_DIM = 64
GDN_WIDTH = GDN_HEADS * GDN_HEAD_DIM
GDN_CONV = 4
GDN_CHUNK = 64
NSA_Q_HEADS = 8
NSA_KV_HEADS = 2
NSA_HEAD_DIM = 64
NSA_GROUP = NSA_Q_HEADS // NSA_KV_HEADS
NSA_WIDTH = NSA_Q_HEADS * NSA_HEAD_DIM
KV_WIDTH = NSA_KV_HEADS * NSA_HEAD_DIM
KV2 = 2 * KV_WIDTH
CMP_BLOCK = 32
SEL_BLOCK = 64
SEL_TOPK = 16
WINDOW = 512
PAGE_SIZE = 128
NEG = -1e30
BIG = 1e4
LN_EPS = 1e-5

SUBLANES = 8
LANES = 128
VMEM_LIMIT = 52 * 1024 * 1024

NSA_TQ = 128
NSA_TK = 256
PAGES_PER_STEP = 64
PAGES_PER_TILE = LANES * CMP_BLOCK // PAGE_SIZE
GDN_SEQS_PER_STEP = 4
SAMPLE_SEQS_PER_STEP = 8


def _sig(x):
    return 1.0 / (1.0 + jnp.exp(-x))


def _softplus(x):
    return jnp.maximum(x, 0.0) + jnp.log(1.0 + jnp.exp(-jnp.abs(x)))


def _layer_norm(r, g, b):
    mu = jnp.mean(r, -1, keepdims=True)
    d = r - mu
    var = jnp.mean(d * d, -1, keepdims=True)
    return d * lax.rsqrt(var + LN_EPS) * g + b


def _dot(a, b):
    return jnp.dot(a, b, preferred_element_type=F32)


def _dot_nt(a, b):
    return lax.dot_general(a, b, (((1,), (1,)), ((), ())), preferred_element_type=F32)


def _dot_tn(a, b):
    return lax.dot_general(a, b, (((0,), (0,)), ((), ())), preferred_element_type=F32)


def _dot_split2(a, sel):
    hi = a.astype(BF16)
    lo = (a - hi.astype(F32)).astype(BF16)
    return _dot(hi, sel) + _dot(lo, sel)


def _split3(a):
    hi = a.astype(BF16)
    r1 = a - hi.astype(F32)
    mid = r1.astype(BF16)
    return hi, mid, (r1 - mid.astype(F32)).astype(BF16)


def _params(*sem):
    return pltpu.CompilerParams(dimension_semantics=sem, vmem_limit_bytes=VMEM_LIMIT)


def _mod_spec(kind, tm, d, tiles_per_seq, k):
    if kind == "seq":
        return pl.BlockSpec((None, 1, d), lambda i, *_: ((i // tiles_per_seq) * 9 + k, 0, 0))
    return pl.BlockSpec((tm, d), lambda i, *_: (i, k))


def _ada_kernel(c_ref, w_ref, b_ref, o_ref):
    c = c_ref[...]
    h = (c * _sig(c)).astype(BF16)
    o_ref[...] = _dot(h, w_ref[...].astype(BF16)) + b_ref[...]


def _ada(c, w, b):
    r, d = c.shape
    n = w.shape[1]
    tn = d
    return pl.pallas_call(
        _ada_kernel,
        grid=(n // tn,),
        in_specs=[pl.BlockSpec((r, d), lambda j: (0, 0)),
                  pl.BlockSpec((d, tn), lambda j: (0, j)),
                  pl.BlockSpec((1, tn), lambda j: (0, j))],
        out_specs=pl.BlockSpec((r, tn), lambda j: (0, j)),
        out_shape=jax.ShapeDtypeStruct((r, n), F32),
        compiler_params=_params("arbitrary"),
        name="ada",
    )(c, w, b.reshape(1, n))


def _ffn_kernel(x_ref, sh_ref, sc_ref, gt_ref, wg_ref, wu_ref, wd_ref, lg_ref, lb_ref, o_ref, *, alpha, tf):
    x = x_ref[...]
    h = (x * (1.0 + sc_ref[...]) + sh_ref[...]).astype(BF16)
    acc = None
    for c in range(0, wg_ref.shape[1], tf):
        g = _dot(h, wg_ref[:, c:c + tf])
        u = _dot(h, wu_ref[:, c:c + tf])
        a = (g * _sig(g) * u).astype(BF16)
        part = _dot(a, wd_ref[c:c + tf, :])
        acc = part if acc is None else acc + part
    r = alpha * x + (0.5 * gt_ref[...]) * acc
    o_ref[...] = _layer_norm(r, lg_ref[...], lb_ref[...])


def _ffn(x, mods, kind, rows_per_seq, k0, wg, wu, wd, lg, lb, alpha, tm, tf):
    n, d = x.shape
    tps = max(rows_per_seq // tm, 1)
    ms = lambda k: _mod_spec(kind, tm, d, tps, k)
    full = lambda a: pl.BlockSpec(a.shape, lambda i: (0, 0))
    return pl.pallas_call(
        functools.partial(_ffn_kernel, alpha=alpha, tf=tf),
        grid=(n // tm,),
        in_specs=[pl.BlockSpec((tm, d), lambda i: (i, 0)),
                  ms(k0), ms(k0 + 1), ms(k0 + 2), full(wg), full(wu), full(wd), full(lg), full(lb)],
        out_specs=pl.BlockSpec((tm, d), lambda i: (i, 0)),
        out_shape=jax.ShapeDtypeStruct((n, d), F32),
        compiler_params=_params("parallel"),
        name="ffn",
    )(x, mods, mods, mods, wg, wu, wd, lg, lb)


def _inproj_kernel(x_ref, sh_ref, sc_ref, wqkv, wz, wm, wsm, wk, wt,
                   oqkv, oz, om, osm, oks, okw, oqt, okct, okst, okwt):
    h = (x_ref[...] * (1.0 + sc_ref[...]) + sh_ref[...]).astype(BF16)
    oqkv[...] = _dot(h, wqkv[...])
    oz[...] = _dot(h, wz[...]).astype(BF16)
    om[...] = _sig(_dot(h, wm[...])).astype(BF16)
    osm[...] = _dot(h, wsm[...])
    kt = _dot(h, wk[...]).astype(BF16)
    oks[...] = kt[:, 0:KV_WIDTH]
    okw[...] = kt[:, KV_WIDTH:2 * KV_WIDTH]
    ft = _dot_nt(wt[...], h)
    oqt[...] = (ft[0:NSA_WIDTH] * (NSA_HEAD_DIM ** -0.5)).astype(BF16)
    okct[...] = ft[NSA_WIDTH:NSA_WIDTH + KV2]
    okst[...] = ft[NSA_WIDTH + KV2:NSA_WIDTH + 2 * KV2]
    okwt[...] = ft[NSA_WIDTH + 2 * KV2:NSA_WIDTH + 3 * KV2]


def _inproj(x, mods, kind, rows_per_seq, ws, wk, wt, tm):
    n, d = x.shape
    tps = max(rows_per_seq // tm, 1)
    widths = [w.shape[1] for w in ws] + [KV_WIDTH] * 2
    dtypes = [F32, BF16, BF16, F32, BF16, BF16]
    t_rows = [NSA_WIDTH, KV2, KV2, KV2]
    t_dtypes = [BF16, F32, F32, F32]
    if kind == "seq":
        t_specs = [pl.BlockSpec((None, r, tm), lambda i: (i // tps, 0, i % tps)) for r in t_rows]
        t_shapes = [jax.ShapeDtypeStruct((n // rows_per_seq, r, rows_per_seq), dt) for r, dt in zip(t_rows, t_dtypes)]
    else:
        t_specs = [pl.BlockSpec((r, tm), lambda i: (0, i)) for r in t_rows]
        t_shapes = [jax.ShapeDtypeStruct((r, n), dt) for r, dt in zip(t_rows, t_dtypes)]
    full = lambda w: pl.BlockSpec(w.shape, lambda i: (0, 0))
    return pl.pallas_call(
        _inproj_kernel,
        grid=(n // tm,),
        in_specs=[pl.BlockSpec((tm, d), lambda i: (i, 0)),
                  _mod_spec(kind, tm, d, tps, 3), _mod_spec(kind, tm, d, tps, 4)]
                 + [full(w) for w in ws] + [full(wk), full(wt)],
        out_specs=[pl.BlockSpec((tm, wd), lambda i: (i, 0)) for wd in widths] + t_specs,
        out_shape=[jax.ShapeDtypeStruct((n, wd), dt) for wd, dt in zip(widths, dtypes)] + t_shapes,
        compiler_params=_params("parallel"),
        name="inproj",
    )(x, mods, mods, *ws, wk, wt)


def _gdn_kernel(qkv_ref, z_ref, sm_ref, abt_ref, cb_ref, cw_ref, prow_ref, alog_ref, dtb_ref, nw_ref,
                s0_ref, tri_ref, triu_ref, bd_ref, expg_ref, expb_ref, o_ref, s_ref, xbuf,
                *, chunk, l_valid, l_pad):
    C = chunk
    HD = GDN_HEAD_DIM
    W = GDN_WIDTH
    NB = qkv_ref.shape[0]
    c = pl.program_id(1)

    @pl.when(c == 0)
    def _():
        xbuf[:, 0:SUBLANES, :] = cb_ref[...]
        s_ref[...] = s0_ref[...]

    cw = cw_ref[...]
    bd = bd_ref[...]
    hw = bd.shape[0]
    pr = prow_ref[...]
    qn_l, kn_l, gx_l, g_row_l, q_dec_l, k_dec_l, kb_l, vb_l, kbg_l, e_last_l, z_l = ([] for _ in range(11))
    for n in range(NB):
        x = qkv_ref[n]
        xbuf[n, SUBLANES:SUBLANES + C, :] = x
        y = (xbuf[n, 5:5 + C, :] * cw[0:1] + xbuf[n, 6:6 + C, :] * cw[1:2]
             + xbuf[n, 7:7 + C, :] * cw[2:3] + x * cw[3:4])
        xbuf[n, 0:SUBLANES, :] = x[C - SUBLANES:C, :]
        act = y * _sig(y)
        q = act[:, 0:W]
        k = act[:, W:2 * W]
        v = act[:, 2 * W:3 * W]

        sq = jnp.concatenate([q * q, k * k], 0)
        ss = jnp.concatenate([_dot_split2(sq[:, j:j + hw], bd) for j in range(0, W, hw)], 1)
        qn = q * lax.rsqrt(ss[:C] + 1e-6) * (HD ** -0.5)
        kn = k * lax.rsqrt(ss[C:] + 1e-6)

        sm = sm_ref[n]
        gcol = -jnp.exp(pr[0:1]) * _softplus(sm + pr[1:2])
        bcol = _sig(sm)
        ab = abt_ref[n]
        grow = -jnp.exp(alog_ref[...]) * _softplus(ab[0:GDN_HEADS] + dtb_ref[...])
        if l_pad != l_valid:
            vcol = (c * C + lax.broadcasted_iota(jnp.int32, (C, 1), 0) < l_valid).astype(F32)
            vrow = (c * C + lax.broadcasted_iota(jnp.int32, (1, C), 1) < l_valid).astype(F32)
            qn, kn, v = qn * vcol, kn * vcol, v * vcol
            gcol, bcol, grow = gcol * vcol, bcol * vcol, grow * vrow

        tri_b = tri_ref[...].astype(BF16)
        g_cum = sum(_dot(tri_b, part) for part in _split3(gcol))
        gx = _dot_split2(g_cum, expg_ref[...])
        bx = _dot_split2(bcol, expb_ref[...])
        triu_b = triu_ref[...].astype(BF16)
        g_row = sum(_dot(part, triu_b) for part in _split3(grow))

        eg = jnp.exp(gx)
        g_last = gx[C - 1:C, :]
        kb = kn * bx
        for lst, val in ((qn_l, qn), (kn_l, kn), (gx_l, gx), (g_row_l, g_row), (q_dec_l, qn * eg),
                         (k_dec_l, kn * jnp.exp(g_last - gx)), (kb_l, kb), (vb_l, v * bx), (kbg_l, kb * eg),
                         (e_last_l, jnp.exp(g_last)), (z_l, z_ref[n].astype(F32))):
            lst.append(val)

    ri = lax.broadcasted_iota(jnp.int32, (C, C), 0)
    ci = lax.broadcasted_iota(jnp.int32, (C, C), 1)
    incl = ri >= ci
    strict = ri > ci
    eye = (ri == ci).astype(F32)
    sh = min(C, SUBLANES).bit_length() - 1
    diag_blk = (ri >> sh) == (ci >> sh)
    merge_masks = []
    while (1 << sh) < C:
        merge_masks.append(((ri >> (sh + 1)) == (ci >> (sh + 1))) & ((ri >> sh) == (ci >> sh) + 1))
        sh += 1

    H = GDN_HEADS
    heads = range(NB * H)
    col = lambda xs, j: xs[j // H][:, (j % H) * HD:(j % H + 1) * HD]
    bfl = lambda xs: [x.astype(BF16) for x in xs]

    dec = [jnp.where(incl, jnp.exp(jnp.where(
        incl, gx_l[j // H][:, (j % H) * HD:(j % H) * HD + C] - g_row_l[j // H][j % H:j % H + 1, :], 0.0)), 0.0)
        for j in heads]
    r = [_dot_nt(jnp.concatenate([col(kb_l, h), col(qn_l, h)], 0).astype(BF16), col(kn_l, h).astype(BF16))
         for h in heads]
    a_kk = [jnp.where(strict, r[h][:C] * dec[h], 0.0) for h in heads]
    a_qk = bfl([r[h][C:] * dec[h] for h in heads])

    nd = [jnp.where(diag_blk, -a_kk[h], 0.0) for h in heads]
    ndb = bfl(nd)
    p2 = [_dot(ndb[h], ndb[h]) for h in heads]
    t = [eye + nd[h] for h in heads]
    r2 = [_dot(p2[h].astype(BF16), jnp.concatenate([t[h], p2[h]], 1).astype(BF16)) for h in heads]
    t = [t[h] + r2[h][:, :C] for h in heads]
    t = [t[h] + _dot(r2[h][:, C:].astype(BF16), t[h].astype(BF16)) for h in heads]
    for mm in merge_masks:
        tb = bfl(t)
        tl = bfl([_dot(tb[h], jnp.where(mm, a_kk[h], 0.0).astype(BF16)) for h in heads])
        t = [t[h] - _dot(tl[h], tb[h]) for h in heads]
    tb = bfl(t)

    rhs = [jnp.concatenate([col(vb_l, h), col(kbg_l, h)], 1) for h in heads]
    x0 = [_dot(tb[h], rhs[h].astype(BF16)) for h in heads]
    res = []
    for h in heads:
        ah = a_kk[h].astype(BF16)
        al = (a_kk[h] - ah.astype(F32)).astype(BF16)
        xh = x0[h].astype(BF16)
        xl = (x0[h] - xh.astype(F32)).astype(BF16)
        hh = _dot(ah, jnp.concatenate([xh, xl], 1))
        res.append(rhs[h] - x0[h] - (hh[:, :2 * HD] + hh[:, 2 * HD:] + _dot(al, xh)))
    uw = [x0[h] + _dot(tb[h], res[h].astype(BF16)) for h in heads]

    s_old = [s_ref[h // H, h % H] for h in heads]
    wq = [_dot(jnp.concatenate([uw[h][:, HD:], col(q_dec_l, h)], 0).astype(BF16), s_old[h].astype(BF16))
          for h in heads]
    v_new = bfl([uw[h][:, :HD] - wq[h][:C] for h in heads])
    o = [wq[h][C:] + _dot(a_qk[h], v_new[h]) for h in heads]
    for h in heads:
        s_ref[h // H, h % H] = (s_old[h] * col(e_last_l, h)
                                + _dot_tn(col(k_dec_l, h).astype(BF16), v_new[h]))

    nw = nw_ref[...]
    outs = []
    for h in heads:
        o_h = o[h] * lax.rsqrt(jnp.mean(o[h] * o[h], -1, keepdims=True) + 1e-6)
        zh = col(z_l, h)
        outs.append(o_h * nw * (zh * _sig(zh)))
    for n in range(NB):
        o_ref[n] = jnp.concatenate(outs[n * H:(n + 1) * H], 1).astype(BF16)


def _gdn(qkv, z, small, conv_buf, s0, conv_w, a_log, dt_bias, norm_w, l_valid, chunk):
    b, lp, w3 = qkv.shape
    C = chunk
    nc = lp // C
    H, HD, W = GDN_HEADS, GDN_HEAD_DIM, GDN_WIDTH
    abt = small[:, :, :2 * H].reshape(b, nc, C, 2 * H).transpose(0, 1, 3, 2)
    cb = jnp.pad(conv_buf, ((0, 0), (SUBLANES - (GDN_CONV - 1), 0), (0, 0)))
    prow = jnp.zeros((2, LANES), F32).at[0, :H].set(a_log).at[1, :H].set(dt_bias)
    alog_r = jnp.broadcast_to(a_log[:, None], (H, C))
    dtb_r = jnp.broadcast_to(dt_bias[:, None], (H, C))
    ix = jnp.arange(C)
    tri = (ix[:, None] >= ix[None, :]).astype(F32)
    triu = tri.T
    hid = jnp.arange(W) // HD
    hw = 2 * LANES
    bd = (hid[:hw, None] == hid[None, :hw]).astype(BF16)
    lane = jnp.arange(LANES)
    expg = (lane[:, None] == hid[None, :]).astype(BF16)
    expb = (lane[:, None] == hid[None, :] + H).astype(BF16)
    const = lambda shape: pl.BlockSpec(shape, lambda i, j: (0,) * len(shape))
    nb = GDN_SEQS_PER_STEP if b % GDN_SEQS_PER_STEP == 0 else 1
    o, s_fin = pl.pallas_call(
        functools.partial(_gdn_kernel, chunk=C, l_valid=l_valid, l_pad=lp),
        grid=(b // nb, nc),
        in_specs=[pl.BlockSpec((nb, C, w3), lambda i, j: (i, j, 0)),
                  pl.BlockSpec((nb, C, W), lambda i, j: (i, j, 0)),
                  pl.BlockSpec((nb, C, LANES), lambda i, j: (i, j, 0)),
                  pl.BlockSpec((nb, None, 2 * H, C), lambda i, j: (i, j, 0, 0)),
                  pl.BlockSpec((nb, SUBLANES, w3), lambda i, j: (i, 0, 0)),
                  const((GDN_CONV, w3)), const((2, LANES)), const((H, C)), const((H, C)), const((1, HD)),
                  pl.BlockSpec((nb, H, HD, HD), lambda i, j: (i, 0, 0, 0)),
                  const((C, C)), const((C, C)), const((hw, hw)), const((LANES, W)), const((LANES, W))],
        out_specs=[pl.BlockSpec((nb, C, W), lambda i, j: (i, j, 0)),
                   pl.BlockSpec((nb, H, HD, HD), lambda i, j: (i, 0, 0, 0))],
        out_shape=[jax.ShapeDtypeStruct((b, lp, W), BF16), jax.ShapeDtypeStruct((b, H, HD, HD), F32)],
        scratch_shapes=[pltpu.VMEM((nb, SUBLANES + C, w3), F32)],
        compiler_params=_params("parallel", "arbitrary"),
        name="gdn",
    )(qkv, z, small, abt, cb, conv_w, prow, alog_r, dtb_r, norm_w.reshape(1, HD), s0,
      tri, triu, bd, expg, expb)
    return o, s_fin


def _pool_tile(x, w, pm):
    t = x.shape[1]
    wt = jnp.concatenate([w] * (t // LANES), 1) if t > LANES else w
    return _dot_split2(x * wt, pm)


def _pool_kernel(x_ref, w_ref, pm_ref, o_ref):
    for n in range(x_ref.shape[0]):
        o_ref[n] = _pool_tile(x_ref[n], w_ref[...], pm_ref[...])


def _pool_matrix(t):
    nb = -(-(t // CMP_BLOCK) // LANES) * LANES
    return (jnp.arange(t)[:, None] // CMP_BLOCK == jnp.arange(nb)[None, :]).astype(BF16)


def _pool(kvt, wrow):
    b, rows, t = kvt.shape
    pm = _pool_matrix(t)
    nb = SUBLANES if (t <= LANES and b % SUBLANES == 0) else 1
    return pl.pallas_call(
        _pool_kernel,
        grid=(b // nb,),
        in_specs=[pl.BlockSpec((nb, rows, t), lambda i: (i, 0, 0)),
                  pl.BlockSpec(wrow.shape, lambda i: (0, 0)),
                  pl.BlockSpec(pm.shape, lambda i: (0, 0))],
        out_specs=pl.BlockSpec((nb, rows, pm.shape[1]), lambda i: (i, 0, 0)),
        out_shape=jax.ShapeDtypeStruct((b, rows, pm.shape[1]), F32),
        compiler_params=_params("parallel"),
        name="pool",
    )(kvt, wrow, pm)


def _pool_paged_kernel(pt_ref, *refs, pages):
    del pt_ref
    w_ref, pm_ref, o_ref = refs[pages:]
    outs = []
    for c in range(0, pages, PAGES_PER_TILE):
        x = jnp.concatenate([refs[p][...] for p in range(c, c + PAGES_PER_TILE)], 1)
        outs.append(_pool_tile(x, w_ref[...], pm_ref[...]))
    o_ref[...] = jnp.concatenate(outs, 1)


def _pool_paged(cache_t, page_table, wrow, pages):
    _, rows, ps = cache_t.shape
    b, n_pages = page_table.shape
    assert ps == PAGE_SIZE and pages % PAGES_PER_TILE == 0 and n_pages % pages == 0
    pm = _pool_matrix(PAGES_PER_TILE * ps)
    out_lanes = pages * ps // CMP_BLOCK
    page_spec = lambda p: pl.BlockSpec((None, rows, ps), lambda i, j, pt: (pt[i, j * pages + p], 0, 0))
    return pl.pallas_call(
        functools.partial(_pool_paged_kernel, pages=pages),
        grid_spec=pltpu.PrefetchScalarGridSpec(
            num_scalar_prefetch=1,
            grid=(b, n_pages // pages),
            in_specs=[page_spec(p) for p in range(pages)]
                     + [pl.BlockSpec(wrow.shape, lambda i, j, pt: (0, 0)),
                        pl.BlockSpec(pm.shape, lambda i, j, pt: (0, 0))],
            out_specs=pl.BlockSpec((None, rows, out_lanes), lambda i, j, pt: (i, 0, j))),
        out_shape=jax.ShapeDtypeStruct((b, rows, n_pages * ps // CMP_BLOCK), F32),
        compiler_params=_params("parallel", "arbitrary"),
        name="pool_paged",
    )(page_table, *([cache_t] * pages), wrow, pm)


def _cmp_probs(sc, pos, tq):
    n = sc.shape[1]
    lane = lax.broadcasted_iota(jnp.int32, (1, n), 1)
    maskc = ((lane + 1) * CMP_BLOCK - 1) <= pos
    ps = []
    imp = None
    for g in range(NSA_GROUP):
        s = jnp.where(maskc, sc[g * tq:(g + 1) * tq], NEG)
        m = jnp.max(s, -1, keepdims=True)
        e = jnp.where(maskc, jnp.exp(s - m), 0.0)
        den = jnp.sum(e, -1, keepdims=True)
        p = e / jnp.where(den > 0.0, den, 1.0)
        ps.append(p)
        imp = p if imp is None else imp + p
    return jnp.concatenate(ps, 0), imp


def _select_blocks(imp, pos, axis):
    n = imp.shape[axis]
    idx = lax.broadcasted_iota(jnp.int32, (n, 1) if axis == 0 else (1, n), axis)
    even = (idx & 1) == 0
    imp2 = imp + jnp.where(even, pltpu.roll(imp, n - 1, axis), pltpu.roll(imp, 1, axis))
    blk = idx >> 1
    valid = blk * SEL_BLOCK <= pos
    cur = pos >> 6
    forced = (blk == 0) | (blk == cur) | (blk == cur - 1)
    score = jnp.where(valid, jnp.where(forced, BIG, imp2), -1.0)
    work = jnp.where(even, score, -2.0)
    idxf = idx.astype(F32)
    sel = jnp.zeros(work.shape, F32)
    for _ in range(SEL_TOPK):
        m = jnp.max(work, axis, keepdims=True)
        first = jnp.min(jnp.where(work == m, idxf, 1e9), axis, keepdims=True)
        pick = idxf == first
        sel = jnp.where(pick, 1.0, sel)
        work = jnp.where(pick, -2.0, work)
    sel = jnp.where(score >= 0.0, sel, 0.0)
    return sel + pltpu.roll(sel, 1, axis)


def _online_update(carry, s, mf, vt):
    m, l, acc = carry
    s = jnp.where(mf > 0.5, s, NEG)
    m_new = jnp.maximum(m, jnp.max(s, -1, keepdims=True))
    alpha = jnp.exp(m - m_new)
    p = jnp.exp(s - m_new) * mf
    l = alpha * l + jnp.sum(p, -1, keepdims=True)
    acc = alpha * acc + _dot_nt(p.astype(BF16), vt)
    return m_new, l, acc


def _expand_matrix(n_blocks, n_keys):
    return (jnp.arange(n_keys)[None, :] // CMP_BLOCK == jnp.arange(n_blocks)[:, None]).astype(BF16)


def _nsa_prompt_kernel(qt_ref, ks_ref, kw_ref, kvb_ref, kvs_ref, kvw_ref, sm_ref, et_ref, o_ref, *, n_rows):
    TQ, TK, G, HD, KV = NSA_TQ, NSA_TK, NSA_GROUP, NSA_HEAD_DIM, NSA_KV_HEADS
    i = pl.program_id(1)
    t0 = i * TQ
    pos = t0 + lax.broadcasted_iota(jnp.int32, (1, TQ), 1)
    gst = _sig(sm_ref[...]).T
    gate_row = lambda br, h, g: 2 * GDN_HEADS + br * NSA_Q_HEADS + h * G + g

    n_wt = WINDOW // TQ + 1
    w_off = [pl.multiple_of(jnp.maximum(i - (n_wt - 1) + j, 0) * TQ, TQ) for j in range(n_wt)]
    cidx = lax.broadcasted_iota(jnp.int32, (WINDOW + TQ, 1), 0)
    diff = lax.broadcasted_iota(jnp.int32, (1, TQ), 1) + WINDOW - cidx
    bw = jnp.where((diff >= 0) & (diff < WINDOW) & (t0 - WINDOW + cidx >= 0), 0.0, NEG)
    kidx = lax.broadcasted_iota(jnp.int32, (TK, 1), 0)
    n_used = (t0 + TQ + TK - 1) // TK

    krow = lambda h: slice(h * HD, (h + 1) * HD)
    vrow = lambda h: slice((KV + h) * HD, (KV + h + 1) * HD)
    zeros = jnp.zeros((HD, TQ), BF16)
    blk = lax.broadcasted_iota(jnp.int32, (kvb_ref.shape[1], 1), 0)
    maskc = ((blk + 1) * CMP_BLOCK - 1) <= pos
    kw_all = jnp.concatenate([kw_ref[pl.ds(o, TQ), :] for o in w_off], 0)
    qbd, oc, ow, selt = [], [], [], []
    for h in range(KV):
        qh = [qt_ref[(h * G + g) * HD:(h * G + g + 1) * HD, :] for g in range(G)]
        qbd.append([jnp.concatenate([zeros] * h + [q] + [zeros] * (KV - 1 - h), 0) for q in qh])

        kc = kvb_ref[krow(h), :].astype(BF16)
        vc = kvb_ref[vrow(h), :].astype(BF16)
        imp = None
        oc.append([])
        for g in range(G):
            s = jnp.where(maskc, _dot_tn(kc, qh[g]), NEG)
            e = jnp.where(maskc, jnp.exp(s - jnp.max(s, 0, keepdims=True)), 0.0)
            den = jnp.sum(e, 0, keepdims=True)
            p = e / jnp.where(den > 0.0, den, 1.0)
            imp = p if imp is None else imp + p
            oc[h].append(_dot(vc, p.astype(BF16)))
        selt.append(_select_blocks(imp[:n_rows], pos, 0).astype(BF16))

        vw = jnp.concatenate([kvw_ref[vrow(h), pl.ds(o, TQ)] for o in w_off], 1).astype(BF16)
        vw = jnp.concatenate([vw, jnp.ones_like(vw)], 0)
        ow.append([])
        for g in range(G):
            s = _dot(kw_all, qbd[h][g]) + bw
            e = jnp.exp(s - jnp.max(s, 0, keepdims=True))
            r = _dot(vw, e.astype(BF16))
            ow[h].append(r[:HD] / r[HD:HD + 1])

    def scores(kt):
        off = pl.multiple_of(kt * TK, TK)
        k_t = ks_ref[pl.ds(off, TK), :]
        e_t = et_ref[pl.ds(off, TK), :]
        causal = off + kidx <= pos
        out = []
        for h in range(KV):
            bias = jnp.where((_dot(e_t, selt[h]) > 0.5) & causal, 0.0, NEG)
            out += [_dot(k_t, qbd[h][g]) + bias for g in range(G)]
        return out

    def values(kt):
        off = pl.multiple_of(kt * TK, TK)
        vs = [kvs_ref[vrow(h), pl.ds(off, TK)].astype(BF16) for h in range(KV)]
        return [jnp.concatenate([v_t, jnp.ones_like(v_t)], 0) for v_t in vs]

    def body(kt, carry):
        s_all = scores(kt)
        v_prev = values(jnp.maximum(kt - 1, 0))
        out = []
        for h in range(KV):
            for g in range(G):
                m, acc, p_prev = carry[h * G + g]
                acc = acc + _dot(v_prev[h], p_prev)
                s = s_all[h * G + g]
                m_new = jnp.maximum(m, jnp.max(s, 0, keepdims=True))
                out.append((m_new, jnp.exp(m - m_new) * acc, jnp.exp(s - m_new).astype(BF16)))
        return tuple(out)

    init = tuple((jnp.full((1, TQ), NEG, F32), jnp.zeros((2 * HD, TQ), F32), jnp.zeros((TK, TQ), BF16))
                 for _ in range(KV * G))
    res = lax.fori_loop(0, n_used, body, init)
    v_last = values(n_used - 1)

    outs = []
    for h in range(KV):
        for g in range(G):
            _, acc, p_last = res[h * G + g]
            acc = acc + _dot(v_last[h], p_last)
            gate = lambda br: gst[gate_row(br, h, g):gate_row(br, h, g) + 1, :]
            outs.append(gate(0) * oc[h][g] + gate(1) * (acc[:HD] / acc[HD:HD + 1]) + gate(2) * ow[h][g])
    o_ref[...] = jnp.concatenate(outs, 0).astype(BF16)


def _nsa_prompt(q_t, ks, kw, kvb_t, kvs_t, kvw_t, small):
    b, _, l = q_t.shape
    TQ, TK = NSA_TQ, NSA_TK
    n_cmp = l // CMP_BLOCK
    assert kvb_t.shape[2] == LANES and n_cmp <= LANES and l % TK == 0 and WINDOW % TQ == 0
    n_rows = -(-n_cmp // SUBLANES) * SUBLANES
    emat_t = _expand_matrix(n_rows, l).T
    slab = lambda t: pl.BlockSpec((None,) + t.shape[1:], lambda bi, i: (bi, 0, 0))
    return pl.pallas_call(
        functools.partial(_nsa_prompt_kernel, n_rows=n_rows),
        grid=(b, l // TQ),
        in_specs=[pl.BlockSpec((None, NSA_WIDTH, TQ), lambda bi, i: (bi, 0, i)),
                  slab(ks), slab(kw), slab(kvb_t), slab(kvs_t), slab(kvw_t),
                  pl.BlockSpec((None, TQ, LANES), lambda bi, i: (bi, i, 0)),
                  pl.BlockSpec((l, n_rows), lambda bi, i: (0, 0))],
        out_specs=pl.BlockSpec((None, NSA_WIDTH, TQ), lambda bi, i: (bi, 0, i)),
        out_shape=jax.ShapeDtypeStruct((b, NSA_WIDTH, l), BF16),
        compiler_params=_params("parallel", "arbitrary"),
        name="nsa_prompt",
    )(q_t, ks, kw, kvb_t, kvs_t, kvw_t, small, emat_t)


LS = SUBLANES
ROWS_H = NSA_GROUP * LS


def _row_token(rows):
    return lax.broadcasted_iota(jnp.int32, (rows, 1), 0) & (LS - 1)


def _nsa_sample_a_kernel(q_ref, kvb_ref, cw_ref, new_ref, oc_ref, ow_ref, sel_ref, *, past, l_new):
    HD, KV = NSA_HEAD_DIM, NSA_KV_HEADS
    NB = q_ref.shape[0]
    pos = past + lax.broadcasted_iota(jnp.int32, (LS, 1), 0)
    wb = cw_ref.shape[2]
    tok = _row_token(ROWS_H)
    c1 = lax.broadcasted_iota(jnp.int32, (1, wb), 1)
    d1 = tok + wb - c1
    m1 = jnp.where((d1 >= 0) & (d1 < WINDOW) & (past - wb + c1 >= 0), 1.0, 0.0)
    c2 = lax.broadcasted_iota(jnp.int32, (1, new_ref.shape[2]), 1)
    d2 = tok - c2
    m2 = jnp.where((d2 >= 0) & (d2 < WINDOW) & (c2 < l_new), 1.0, 0.0)
    imps = []
    for n in range(NB):
        for h in range(KV):
            krow = slice(h * HD, (h + 1) * HD)
            vrow = slice((KV + h) * HD, (KV + h + 1) * HD)
            qh = q_ref[n, h]
            p, imp = _cmp_probs(_dot(qh, kvb_ref[n, krow, :].astype(BF16)), pos, LS)
            imps.append(imp)
            oc_ref[n, h] = _dot_nt(p.astype(BF16), kvb_ref[n, vrow, :].astype(BF16))

            s1 = jnp.where(m1 > 0.5, _dot(qh, cw_ref[n, krow, :].astype(BF16)), NEG)
            s2 = jnp.where(m2 > 0.5, _dot(qh, new_ref[n, krow, :].astype(BF16)), NEG)
            m = jnp.maximum(jnp.max(s1, -1, keepdims=True), jnp.max(s2, -1, keepdims=True))
            e1 = jnp.exp(s1 - m) * m1
            e2 = jnp.exp(s2 - m) * m2
            den = jnp.sum(e1, -1, keepdims=True) + jnp.sum(e2, -1, keepdims=True)
            den = jnp.where(den > 0.0, den, 1.0)
            ow_ref[n, h] = (_dot_nt((e1 / den).astype(BF16), cw_ref[n, vrow, :].astype(BF16))
                            + _dot_nt((e2 / den).astype(BF16), new_ref[n, vrow, :].astype(BF16)))
    sel = _select_blocks(jnp.concatenate(imps, 0), jnp.concatenate([pos] * (NB * KV), 0), 1)
    for n in range(NB):
        for h in range(KV):
            sel_ref[n, h] = sel[(n * KV + h) * LS:(n * KV + h + 1) * LS]


def _nsa_sample_a(q_rows, kvb_t, cache_win_t, kvw_new_t, past, l_new):
    b, _, ncp = kvb_t.shape
    nb = SAMPLE_SEQS_PER_STEP if b % SAMPLE_SEQS_PER_STEP == 0 else 1
    per_seq = lambda t: pl.BlockSpec((nb,) + t.shape[1:], lambda i: (i,) + (0,) * (t.ndim - 1))
    out_rows = jax.ShapeDtypeStruct(q_rows.shape, F32)
    sel_shape = jax.ShapeDtypeStruct((b, NSA_KV_HEADS, LS, ncp), F32)
    return pl.pallas_call(
        functools.partial(_nsa_sample_a_kernel, past=past, l_new=l_new),
        grid=(b // nb,),
        in_specs=[per_seq(q_rows), per_seq(kvb_t), per_seq(cache_win_t), per_seq(kvw_new_t)],
        out_specs=[per_seq(out_rows), per_seq(out_rows), per_seq(sel_shape)],
        out_shape=[out_rows, out_rows, sel_shape],
        compiler_params=_params("parallel"),
        name="nsa_sample_a",
    )(q_rows, kvb_t, cache_win_t, kvw_new_t)


def _nsa_sample_sel_kernel(pt_ref, *refs, pages, l_new):
    del pt_ref
    page_refs = refs[:pages]
    q_ref, sel_ref, tail_ref, new_ref, e_ref, oc_ref, ow_ref, gate_ref, o_ref, m_scr, l_scr, acc_scr = refs[pages:]
    HD, KV, G = NSA_HEAD_DIM, NSA_KV_HEADS, NSA_GROUP
    i = pl.program_id(1)
    last = i == pl.num_programs(1) - 1

    @pl.when(i == 0)
    def _():
        m_scr[...] = jnp.full_like(m_scr, NEG)
        l_scr[...] = jnp.zeros_like(l_scr)
        acc_scr[...] = jnp.zeros_like(acc_scr)

    kv = jnp.concatenate([r[...] for r in page_refs], 1)
    emat = e_ref[...]
    for h in range(KV):
        krow = slice(h * HD, (h + 1) * HD)
        vrow = slice((KV + h) * HD, (KV + h + 1) * HD)
        selh = sel_ref[h].astype(BF16)
        mh = jnp.concatenate([_dot(selh[:, c:c + LANES], emat) for c in range(0, selh.shape[1], LANES)], 1)
        carry = _online_update((m_scr[h], l_scr[h], acc_scr[h]),
                               _dot(q_ref[h], kv[krow].astype(BF16)),
                               jnp.concatenate([mh] * G, 0), kv[vrow].astype(BF16))
        m_scr[h], l_scr[h], acc_scr[h] = carry

    @pl.when(last)
    def _():
        tok = _row_token(ROWS_H)
        c2 = lax.broadcasted_iota(jnp.int32, (1, new_ref.shape[1]), 1)
        for h in range(KV):
            krow = slice(h * HD, (h + 1) * HD)
            vrow = slice((KV + h) * HD, (KV + h + 1) * HD)
            flag = jnp.concatenate([tail_ref[h][:, 0:1]] * G, 0)
            mt = jnp.where((c2 <= tok) & (c2 < l_new) & (flag > 0.5), 1.0, 0.0)
            _, l, acc = _online_update((m_scr[h], l_scr[h], acc_scr[h]),
                                       _dot(q_ref[h], new_ref[krow, :].astype(BF16)), mt,
                                       new_ref[vrow, :].astype(BF16))
            osel = acc / jnp.where(l > 0.0, l, 1.0)
            gs = _sig(gate_ref[h])
            o_ref[h] = gs[:, 0:1] * oc_ref[h] + gs[:, 1:2] * osel + gs[:, 2:3] * ow_ref[h]


def _nsa_sample_sel(q_rows, selmask, cache_sel_t, page_table, kvs_new_t, oc, ow, gate_rows, pages, l_new):
    b, n_pages = page_table.shape
    _, rows, ps = cache_sel_t.shape
    lanes_per_step = pages * ps // CMP_BLOCK
    assert lanes_per_step % LANES == 0 and n_pages % pages == 0
    n_steps = n_pages // pages
    emat = _expand_matrix(LANES, LANES * CMP_BLOCK)
    page_spec = lambda p: pl.BlockSpec((None, rows, ps), lambda bi, i, pt: (pt[bi, i * pages + p], 0, 0))
    per_seq = lambda t: pl.BlockSpec((None,) + t.shape[1:], lambda bi, i, pt: (bi,) + (0,) * (t.ndim - 1))
    tail_block = n_steps * lanes_per_step // LANES
    kvh = NSA_KV_HEADS
    return pl.pallas_call(
        functools.partial(_nsa_sample_sel_kernel, pages=pages, l_new=l_new),
        grid_spec=pltpu.PrefetchScalarGridSpec(
            num_scalar_prefetch=1,
            grid=(b, n_steps),
            in_specs=[page_spec(p) for p in range(pages)] + [
                per_seq(q_rows),
                pl.BlockSpec((None, kvh, LS, lanes_per_step), lambda bi, i, pt: (bi, 0, 0, i)),
                pl.BlockSpec((None, kvh, LS, LANES), lambda bi, i, pt: (bi, 0, 0, tail_block)),
                per_seq(kvs_new_t),
                pl.BlockSpec(emat.shape, lambda bi, i, pt: (0, 0)),
                per_seq(oc), per_seq(ow), per_seq(gate_rows)],
            out_specs=per_seq(oc),
            scratch_shapes=[pltpu.VMEM((kvh, ROWS_H, 1), F32), pltpu.VMEM((kvh, ROWS_H, 1), F32),
                            pltpu.VMEM((kvh, ROWS_H, NSA_HEAD_DIM), F32)]),
        out_shape=jax.ShapeDtypeStruct(oc.shape, F32),
        compiler_params=_params("parallel", "arbitrary"),
        name="nsa_sample_sel",
    )(page_table, *([cache_sel_t] * pages), q_rows, selmask, selmask, kvs_new_t, emat, oc, ow, gate_rows)


def _nsa_sample(q_t, kvc_t, kvs_t, kvw_t, gate, cache_cmp_t, cache_sel_t, cache_win_t, page_table, wrow):
    b, _, l = q_t.shape
    KV, G, HD = NSA_KV_HEADS, NSA_GROUP, NSA_HEAD_DIM
    n_pages = page_table.shape[1]
    ps = cache_cmp_t.shape[2]
    past = n_pages * ps
    assert l <= LS and ps == PAGE_SIZE
    pages = min(PAGES_PER_STEP, n_pages)
    pad_new = lambda t: jnp.pad(t, ((0, 0), (0, 0), (0, ps - l)))
    kvb_t = jnp.concatenate([_pool_paged(cache_cmp_t, page_table, wrow, pages), _pool(pad_new(kvc_t), wrow)], 2)
    q_rows = jnp.pad(q_t.reshape(b, KV, G, HD, l).transpose(0, 1, 2, 4, 3),
                     ((0, 0),) * 3 + ((0, LS - l), (0, 0))).reshape(b, KV, ROWS_H, HD)
    oc, ow, selmask = _nsa_sample_a(q_rows, kvb_t, cache_win_t, pad_new(kvw_t), past, l)
    gate_rows = jnp.pad(gate.reshape(b, l, 3, KV, G).transpose(0, 3, 4, 1, 2),
                        ((0, 0),) * 3 + ((0, LS - l), (0, 0))).reshape(b, KV, ROWS_H, 3)
    o = _nsa_sample_sel(q_rows, selmask, cache_sel_t, page_table, pad_new(kvs_t), oc, ow, gate_rows, pages, l)
    o = o.reshape(b, KV, G, LS, HD)[:, :, :, :l]
    return o.transpose(0, 3, 1, 2, 4).reshape(b, l, NSA_WIDTH).astype(BF16)


def _merge_kernel(x_ref, gt_ref, og_ref, on_ref, m_ref, wbg_ref, wbn_ref, wo_ref, lg_ref, lb_ref, o_ref,
                  *, alpha, nsa_feature_major):
    d = x_ref.shape[1]
    m = m_ref[...].astype(F32)
    nsa_dot = _dot_tn if nsa_feature_major else _dot
    mix = m[:, :d] * _dot(og_ref[...], wbg_ref[...]) + m[:, d:] * nsa_dot(on_ref[...], wbn_ref[...])
    y = _dot(mix.astype(BF16), wo_ref[...])
    o_ref[...] = _layer_norm(alpha * x_ref[...] + gt_ref[...] * y, lg_ref[...], lb_ref[...])


def _merge(x, mods, kind, rows_per_seq, o_gdn, o_nsa, msig, wbg, wbn, wo, lg, lb, alpha, tm):
    n, d = x.shape
    tps = max(rows_per_seq // tm, 1)
    row = lambda wd: pl.BlockSpec((tm, wd), lambda i: (i, 0))
    full = lambda a: pl.BlockSpec(a.shape, lambda i: (0, 0))
    feature_major = o_nsa.ndim == 3
    nsa_spec = (pl.BlockSpec((None, o_nsa.shape[1], tm), lambda i: (i // tps, 0, i % tps)) if feature_major
                else row(o_nsa.shape[1]))
    return pl.pallas_call(
        functools.partial(_merge_kernel, alpha=alpha, nsa_feature_major=feature_major),
        grid=(n // tm,),
        in_specs=[row(d), _mod_spec(kind, tm, d, tps, 5), row(o_gdn.shape[1]), nsa_spec,
                  row(2 * d), full(wbg), full(wbn), full(wo), full(lg), full(lb)],
        out_specs=row(d),
        out_shape=jax.ShapeDtypeStruct((n, d), F32),
        compiler_params=_params("parallel"),
        name="merge",
    )(x, mods, o_gdn, o_nsa, msig, wbg, wbn, wo, lg, lb)


def _row_tile(n, pref):
    t = min(pref, n)
    while n % t:
        t //= 2
    return t


def _ff_tile(dff):
    for parts in (11, 4, 2, 1, 22):
        if dff % parts == 0 and (dff // parts) % LANES == 0:
            return dff // parts
    return dff


def _to_rows(kv_t):
    b, _, t = kv_t.shape
    return kv_t.reshape(b, 2, NSA_KV_HEADS, NSA_HEAD_DIM, t).transpose(0, 4, 1, 2, 3)


def _to_feature_major(kv_rows):
    n, t = kv_rows.shape[:2]
    return kv_rows.transpose(0, 2, 3, 4, 1).reshape(n, KV2, t)


def _layer(x, mod, kind, past, lw, alpha):
    b, l, d = x.shape
    n = b * l
    (wg1, wu1, wd1, wg2, wu2, wd2, w_in_parts, wk, wt, conv_w, a_log, dt_bias, norm_w, wrow,
     wbg, wbn, wo, ln_g, ln_b) = lw
    tm = _row_tile(l if kind == "seq" else n, 512)
    tm_in = _row_tile(l if kind == "seq" else n, 256)
    tf = _ff_tile(wg1.shape[1])
    tm_ff = _row_tile(l if kind == "seq" else n, 512)
    lg = lambda i: ln_g[i].reshape(1, d)
    lb = lambda i: ln_b[i].reshape(1, d)

    x1 = _ffn(x.reshape(n, d), mod, kind, l, 0, wg1, wu1, wd1, lg(0), lb(0), alpha, tm_ff, tf)

    qkv, z, msig, small, ks, kw, q_t, kvc_t, kvs_t, kvw_t = _inproj(x1, mod, kind, l, w_in_parts, wk, wt, tm_in)
    seq = lambda t: t.reshape(b, l, t.shape[-1])
    qkv, z, small, ks, kw = [seq(t) for t in (qkv, z, small, ks, kw)]
    if kind == "tok":
        q_t, kvc_t, kvs_t, kvw_t = [t.reshape(t.shape[0], b, l).transpose(1, 0, 2)
                                    for t in (q_t, kvc_t, kvs_t, kvw_t)]

    if past is None:
        conv_buf = jnp.zeros((b, GDN_CONV - 1, 3 * GDN_WIDTH), F32)
        s0 = jnp.zeros((b, GDN_HEADS, GDN_HEAD_DIM, GDN_HEAD_DIM), F32)
        o_nsa = _nsa_prompt(q_t, ks, kw, _pool(kvc_t, wrow), kvs_t, kvw_t, small)
        win_t = kvw_t[:, :, l - min(WINDOW, l):]
    else:
        s0, conv_buf, cache_cmp, cache_sel, cache_win, page_table = past
        cache_win_t = _to_feature_major(cache_win)
        gate = small[:, :, 2 * GDN_HEADS:2 * GDN_HEADS + 3 * NSA_Q_HEADS]
        o_nsa = _nsa_sample(q_t, kvc_t, kvs_t, kvw_t, gate, _to_feature_major(cache_cmp),
                            _to_feature_major(cache_sel), cache_win_t, page_table, wrow)
        o_nsa = o_nsa.reshape(n, NSA_WIDTH)
        win_t = jnp.concatenate([cache_win_t, kvw_t], 2)[:, :, l:]

    chunk = min(GDN_CHUNK, -(-l // SUBLANES) * SUBLANES)
    lp = -(-l // chunk) * chunk
    padl = lambda t: jnp.pad(t, ((0, 0), (0, lp - l), (0, 0)))
    o_gdn, s_new = _gdn(padl(qkv), padl(z), padl(small), conv_buf, s0, conv_w, a_log, dt_bias, norm_w, l, chunk)
    conv_new = jnp.concatenate([conv_buf, qkv], 1)[:, -(GDN_CONV - 1):]

    x2 = _merge(x1, mod, kind, l, o_gdn[:, :l].reshape(n, GDN_WIDTH), o_nsa, msig,
                wbg, wbn, wo, lg(1), lb(1), alpha, tm)
    x3 = _ffn(x2, mod, kind, l, 6, wg2, wu2, wd2, lg(2), lb(2), alpha, tm_ff, tf)
    return x3.reshape(b, l, d), (s_new, conv_new, _to_rows(kvc_t), _to_rows(kvs_t), _to_rows(win_t))


def _split_w_in(w_in, d):
    splits = (3 * GDN_WIDTH, GDN_WIDTH, GDN_HEADS, GDN_HEADS, NSA_WIDTH, KV2, KV2, KV2, 3 * NSA_Q_HEADS, 2 * d)
    offs = [0]
    for s in splits:
        offs.append(offs[-1] + s)
    qkv, z, a, bb, q, kc, ks, kw, gate, merge = [w_in[:, offs[i]:offs[i + 1]] for i in range(len(splits))]
    n_small = 2 * GDN_HEADS + 3 * NSA_Q_HEADS
    small = jnp.pad(jnp.concatenate([a, bb, gate], 1), ((0, 0), (0, LANES - n_small)))
    wk = jnp.concatenate([ks[:, :KV_WIDTH], kw[:, :KV_WIDTH]], 1)
    wt = jnp.concatenate([q, kc, ks, kw], 1).T
    return [t.astype(BF16) for t in (qkv, z, merge, small)], wk.astype(BF16), wt.astype(BF16)


def kernel(x_prompt, x_sample, c_prompt, c_sample, state_gdn, state_gdn_conv, cache_cmp_kv, cache_sel_kv, cache_win_kv, page_table, ln_g, ln_b, w_ada, b_ada, w_ff1_gu, w_ff1_dn, w_ff2_gu, w_ff2_dn, w_in, gdn_conv_w, gdn_a_log, gdn_dt_bias, gdn_norm_w, nsa_w_cmp, w_br_gdn, w_br_nsa, w_out):
    depth = w_in.shape[0]
    alpha = (2.0 * depth) ** 0.25
    bp, lp, d = x_prompt.shape
    bs, ls, _ = x_sample.shape
    y_p, y_s = x_prompt, x_sample
    p_st, s_st = [], []
    for l in range(depth):
        dff = w_ff1_dn.shape[1]
        bf = lambda t: t.astype(BF16)
        wrow = jnp.tile(jnp.broadcast_to(nsa_w_cmp[l].transpose(0, 2, 1)[:, :, None, :],
                                         (2, NSA_KV_HEADS, NSA_HEAD_DIM, CMP_BLOCK)).reshape(KV2, CMP_BLOCK),
                        (1, LANES // CMP_BLOCK))
        w_in_parts, wk, wt = _split_w_in(w_in[l], d)
        lw = (bf(w_ff1_gu[l][:, :dff]), bf(w_ff1_gu[l][:, dff:]), bf(w_ff1_dn[l]),
              bf(w_ff2_gu[l][:, :dff]), bf(w_ff2_gu[l][:, dff:]), bf(w_ff2_dn[l]),
              w_in_parts, wk, wt, gdn_conv_w[l], gdn_a_log[l], gdn_dt_bias[l], gdn_norm_w[l], wrow,
              bf(w_br_gdn[l]), bf(w_br_nsa[l]), bf(w_out[l]), ln_g[l], ln_b[l])
        c_all = jnp.concatenate([c_prompt, c_sample], 0)
        r = c_all.shape[0]
        rp = -(-r // SUBLANES) * SUBLANES
        mod = _ada(jnp.pad(c_all, ((0, rp - r), (0, 0))), w_ada[l], b_ada[l])
        mod_p = mod[:bp].reshape(bp * 9, 1, d)
        mod_s = jnp.repeat(mod[bp:bp + bs], ls, axis=0)
        y_p, st_p = _layer(y_p, mod_p, "seq", None, lw, alpha)
        past = (state_gdn[l], state_gdn_conv[l], cache_cmp_kv[l], cache_sel_kv[l], cache_win_kv[l], page_table)
        y_s, st_s = _layer(y_s, mod_s, "tok", past, lw, alpha)
        p_st.append(st_p)
        s_st.append(st_s)
    p_out = [jnp.stack(t) for t in zip(*p_st)]
    s_out = [jnp.stack(t) for t in zip(*s_st)]
    return (y_p, y_s, *p_out, *s_out)
```
